```python
import jax, jax.numpy as jnp
from jax import lax
import numpy as np

D_MODEL = 1024
BATCH = 4
SEQ = 4096
DEPTH = 1

HEAD_DIM = 64
W_CONV = D_MODEL // 2
W_ATTN = D_MODEL - W_CONV
N_CONV_GROUPS = W_CONV // HEAD_DIM
N_ATTN_HEADS = W_ATTN // HEAD_DIM
CONV_K = 3
D_FF = 4 * D_MODEL
PLE_DIM = 256
Q_BLOCK = 128
EPS = 1e-6
IN_COLS = 3 * W_CONV + 3 * W_ATTN

kernel_name = "hymba_shortconv_stickbreaking_hybrid"


def rmsnorm(x, g):
    xf = x.astype(jnp.float32)
    y = xf * lax.rsqrt(jnp.mean(xf * xf, axis=-1, keepdims=True) + EPS)
    return (y * g.astype(jnp.float32)).astype(x.dtype)


def head_rmsnorm(y, g):
    b, s, w = y.shape
    yf = y.astype(jnp.float32).reshape(b, s, w // HEAD_DIM, HEAD_DIM)
    yf = yf * lax.rsqrt(jnp.mean(yf * yf, axis=-1, keepdims=True) + EPS)
    return (yf.reshape(b, s, w) * g.astype(jnp.float32)).astype(y.dtype)


def short_gated_conv(b_gate, c_gate, u, w_conv):
    v = c_gate * u
    y = lax.conv_general_dilated(
        v, w_conv[:, None, :].astype(v.dtype),
        window_strides=(1,), padding=[(CONV_K - 1, 0)],
        dimension_numbers=('NWC', 'WIO', 'NWC'),
        feature_group_count=v.shape[-1])
    return b_gate * y


def stick_breaking_attention(q, k, v):
    b, h, s, dh = q.shape
    n_blk = s // Q_BLOCK
    scale = dh ** -0.5
    qb = q.reshape(b, h, n_blk, Q_BLOCK, dh).transpose(2, 0, 1, 3, 4)
    kf = k.astype(jnp.float32)
    vf = v.astype(jnp.float32)
    key_pos = jnp.arange(s)

    def one_block(args):
        qi, blk = args
        z = jnp.einsum('bhqd,bhkd->bhqk', qi.astype(jnp.float32), kf) * scale
        q_pos = blk * Q_BLOCK + jnp.arange(Q_BLOCK)
        mask = key_pos[None, :] < q_pos[:, None]
        log_keep = jnp.where(mask, jax.nn.log_sigmoid(-z), 0.0)
        suffix = lax.cumsum(log_keep, axis=3, reverse=True) - log_keep
        a = jnp.where(mask, jnp.exp(jax.nn.log_sigmoid(z) + suffix), 0.0)
        return jnp.einsum('bhqk,bhkd->bhqd', a, vf)

    out = lax.map(one_block, (qb, jnp.arange(n_blk)))
    return out.transpose(1, 2, 0, 3, 4).reshape(b, h, s, dh).astype(q.dtype)


def setup_inputs(seed: int = 0) -> dict:
    key = jax.random.key(seed)
    ks = jax.random.split(key, 20)
    f32 = jnp.float32

    def nrm(k, shape, fan_in):
        return jax.random.normal(k, shape, f32) * (fan_in ** -0.5)

    def gain(k, shape):
        return 1.0 + 0.02 * jax.random.normal(k, shape, f32)

    return {
        "x": jax.random.normal(ks[0], (BATCH, SEQ, D_MODEL), f32),
        "p": jax.random.normal(ks[1], (DEPTH, BATCH, SEQ, PLE_DIM), f32),
        "g_mix": gain(ks[2], (DEPTH, D_MODEL)),
        "w_in": nrm(ks[3], (DEPTH, D_MODEL, IN_COLS), D_MODEL),
        "conv_w": nrm(ks[4], (DEPTH, CONV_K, W_CONV), CONV_K),
        "g_conv_out": gain(ks[5], (DEPTH, W_CONV)),
        "g_attn_out": gain(ks[6], (DEPTH, W_ATTN)),
        "w_out": nrm(ks[7], (DEPTH, W_CONV + W_ATTN, D_MODEL), W_CONV + W_ATTN),
        "g_mlp": gain(ks[8], (DEPTH, D_MODEL)),
        "w_up": nrm(ks[9], (DEPTH, D_MODEL, D_FF), D_MODEL),
        "w_down": nrm(ks[10], (DEPTH, D_FF, D_MODEL), D_FF),
        "g_ple": gain(ks[11], (DEPTH, D_MODEL)),
        "w_ple_gate": nrm(ks[12], (DEPTH, D_MODEL, D_MODEL), D_MODEL),
        "w_ple_proj": nrm(ks[13], (DEPTH, PLE_DIM, D_MODEL), PLE_DIM),
        "g_final": gain(ks[14], (D_MODEL,)),
    }


def reference(x, p, g_mix, w_in, conv_w, g_conv_out, g_attn_out, w_out, g_mlp, w_up, w_down,
              g_ple, w_ple_gate, w_ple_proj, g_final):
    b, s, _ = x.shape
    splits = [W_CONV, 2 * W_CONV, 3 * W_CONV, 3 * W_CONV + W_ATTN, 3 * W_CONV + 2 * W_ATTN]
    h = x
    for i in range(DEPTH):
        a = rmsnorm(h, g_mix[i])
        proj = a @ w_in[i]
        cb, cc, cu, q, k, v = jnp.split(proj, splits, axis=-1)
        conv_out = head_rmsnorm(short_gated_conv(cb, cc, cu, conv_w[i]), g_conv_out[i])
        to_heads = lambda t: t.reshape(b, s, N_ATTN_HEADS, HEAD_DIM).transpose(0, 2, 1, 3)
        attn = stick_breaking_attention(to_heads(q), to_heads(k), to_heads(v))
        attn = head_rmsnorm(attn.transpose(0, 2, 1, 3).reshape(b, s, W_ATTN), g_attn_out[i])
        h = h + jnp.concatenate([conv_out, attn], axis=-1) @ w_out[i]
        m = rmsnorm(h, g_mlp[i])
        h = h + jnp.square(jax.nn.relu(m @ w_up[i])) @ w_down[i]
        gate = jax.nn.sigmoid(rmsnorm(h, g_ple[i]) @ w_ple_gate[i])
        h = h + gate * (p[i] @ w_ple_proj[i])
    return rmsnorm(h, g_final)
```

```python
import functools

import jax
import jax.numpy as jnp
from jax import lax
from jax.experimental import pallas as pl
from jax.experimental.pallas import tpu as pltpu

HEAD_DIM = 64
CONV_K = 3
EPS = 1e-6

F32 = jnp.float32
BF16 = jnp.bfloat16

V7X_VMEM_BYTES = 64 * 1024 * 1024
VMEM_LIMIT_BYTES = V7X_VMEM_BYTES - 8 * 1024 * 1024
SUBLANES = 8

TM_IN = 512
TQ = 256
TK = 256
TM_POST = 256


def _rms_scale(x):
    return lax.rsqrt(jnp.mean(x * x, axis=-1, keepdims=True) + EPS)


def _group_mean(sq, gmat):
    hi = sq.astype(BF16)
    lo = (sq - hi.astype(F32)).astype(BF16)
    return (jnp.dot(hi, gmat, preferred_element_type=F32)
            + jnp.dot(lo, gmat, preferred_element_type=F32))


def _mixer_in_kernel(x_ref, g_ref, w_ref, cw_ref, gco_ref, gmat_ref,
                     conv_ref, q_ref, k_ref, v_ref, vbuf, *, w_conv, w_attn):
    tm = x_ref.shape[1]
    x = x_ref[0]
    a = (x * _rms_scale(x) * g_ref[...]).astype(BF16)
    proj = jnp.dot(a, w_ref[...], preferred_element_type=F32)

    cb = proj[:, 0:w_conv]
    v = proj[:, w_conv:2 * w_conv] * proj[:, 2 * w_conv:3 * w_conv]

    @pl.when(pl.program_id(1) == 0)
    def _():
        vbuf[0:SUBLANES, :] = jnp.zeros((SUBLANES, w_conv), F32)

    vbuf[SUBLANES:SUBLANES + tm, :] = v
    v1 = vbuf[SUBLANES - 1:SUBLANES - 1 + tm, :]
    v2 = vbuf[SUBLANES - 2:SUBLANES - 2 + tm, :]
    cw = cw_ref[...]
    y = cw[0:1, :] * v2 + cw[1:2, :] * v1 + cw[2:3, :] * v
    vbuf[0:SUBLANES, :] = vbuf[tm:tm + SUBLANES, :]

    c = cb * y
    ms = _group_mean(c * c, gmat_ref[...])
    conv_ref[0] = (c * lax.rsqrt(ms + EPS) * gco_ref[...]).astype(BF16)

    o = 3 * w_conv
    q_ref[0] = (proj[:, o:o + w_attn] * (HEAD_DIM ** -0.5)).astype(BF16)
    k_ref[0] = proj[:, o + w_attn:o + 2 * w_attn].astype(BF16)
    v_ref[0] = proj[:, o + 2 * w_attn:o + 3 * w_attn].astype(BF16)


def _mixer_in(x, g_mix, w_in_bf16, conv_w, g_conv_out, gmat, w_conv, w_attn):
    b, s, d = x.shape
    tm = TM_IN
    const = lambda *_: (0, 0)
    tile = lambda bi, i: (bi, i, 0)
    out_sd = lambda w: jax.ShapeDtypeStruct((b, s, w), BF16)
    return pl.pallas_call(
        functools.partial(_mixer_in_kernel, w_conv=w_conv, w_attn=w_attn),
        grid=(b, s // tm),
        in_specs=[
            pl.BlockSpec((1, tm, d), tile),
            pl.BlockSpec((1, d), const),
            pl.BlockSpec(w_in_bf16.shape, const),
            pl.BlockSpec((CONV_K, w_conv), const),
            pl.BlockSpec((1, w_conv), const),
            pl.BlockSpec((w_conv, w_conv), const),
        ],
        out_specs=[
            pl.BlockSpec((1, tm, w_conv), tile),
            pl.BlockSpec((1, tm, w_attn), tile),
            pl.BlockSpec((1, tm, w_attn), tile),
            pl.BlockSpec((1, tm, w_attn), tile),
        ],
        out_shape=[out_sd(w_conv), out_sd(w_attn), out_sd(w_attn), out_sd(w_attn)],
        scratch_shapes=[pltpu.VMEM((tm + SUBLANES, w_conv), F32)],
        compiler_params=pltpu.CompilerParams(
            dimension_semantics=("arbitrary", "arbitrary"),
            vmem_limit_bytes=VMEM_LIMIT_BYTES),
        name="mixer_in",
    )(x, g_mix, w_in_bf16, conv_w, g_conv_out, gmat)


def _attn_kernel(q_ref, k_ref, v_ref, tri_ref, g_ref, gmat_ref, o_ref):
    tq = q_ref.shape[1]
    tk = tri_ref.shape[0]
    i = pl.program_id(2)
    q = q_ref[0]
    tri = tri_ref[...]
    lane = lax.broadcasted_iota(jnp.int32, (tq, 2 * HEAD_DIM), 1)
    row = lax.broadcasted_iota(jnp.int32, (tq, tk), 0)
    col = lax.broadcasted_iota(jnp.int32, (tq, tk), 1)
    causal = col < row

    def tile(qh, j, carry, acc, diag):
        start = pl.multiple_of(j * tk, tk)
        kj = k_ref[0, pl.ds(start, tk), :]
        vj = v_ref[0, pl.ds(start, tk), :]
        z = lax.dot_general(qh, kj, (((1,), (1,)), ((), ())), preferred_element_type=F32)
        sp = jnp.maximum(z, 0.0) + jnp.log(1.0 + jnp.exp(-jnp.abs(z)))
        if diag:
            sp = jnp.where(causal, sp, 0.0)
        c = jnp.dot(sp.astype(BF16), tri, preferred_element_type=F32)
        a = jnp.exp(z - (sp + c + carry))
        if diag:
            a = jnp.where(causal, a, 0.0)
        carry = carry + (c[:, 0:1] + sp[:, 0:1])
        acc = acc + jnp.dot(a.astype(BF16), vj, preferred_element_type=F32)
        return carry, acc

    def one_head(hh):
        in_head = (lane >= hh * HEAD_DIM) & (lane < (hh + 1) * HEAD_DIM)
        qh = jnp.where(in_head, q, jnp.zeros_like(q))
        carry = jnp.zeros((tq, 1), F32)
        acc = jnp.zeros((tq, 2 * HEAD_DIM), F32)
        carry, acc = tile(qh, i, carry, acc, True)

        def body(jj, st):
            return tile(qh, i - 1 - jj, st[0], st[1], False)

        carry, acc = lax.fori_loop(0, i, body, (carry, acc))
        return acc

    acc0 = one_head(0)
    acc1 = one_head(1)
    out = jnp.where(lane < HEAD_DIM, acc0, acc1)
    ms = _group_mean(out * out, gmat_ref[...])
    o_ref[0] = (out * lax.rsqrt(ms + EPS) * g_ref[...]).astype(BF16)


def _attention(q, k, v, tri, g_attn_out, gmat):
    b, s, w = q.shape
    hp = 2 * HEAD_DIM
    return pl.pallas_call(
        _attn_kernel,
        grid=(b, w // hp, s // TQ),
        in_specs=[
            pl.BlockSpec((1, TQ, hp), lambda bi, p, i: (bi, i, p)),
            pl.BlockSpec((1, s, hp), lambda bi, p, i: (bi, 0, p)),
            pl.BlockSpec((1, s, hp), lambda bi, p, i: (bi, 0, p)),
            pl.BlockSpec((TK, TK), lambda bi, p, i: (0, 0)),
            pl.BlockSpec((1, hp), lambda bi, p, i: (0, p)),
            pl.BlockSpec((hp, hp), lambda bi, p, i: (0, 0)),
        ],
        out_specs=pl.BlockSpec((1, TQ, hp), lambda bi, p, i: (bi, i, p)),
        out_shape=jax.ShapeDtypeStruct((b, s, w), BF16),
        compiler_params=pltpu.CompilerParams(
            dimension_semantics=("arbitrary", "arbitrary", "arbitrary"),
            vmem_limit_bytes=VMEM_LIMIT_BYTES),
        name="sb_attention",
    )(q, k, v, tri, g_attn_out, gmat)


def _post_kernel(x_ref, conv_ref, attn_ref, p_ref, wo_ref, gmlp_ref, wup_ref, wdn_ref,
                 gple_ref, wg_ref, wp_ref, gfin_ref, o_ref, *, apply_final):
    w_conv = conv_ref.shape[1]
    h = x_ref[...]
    h = h + jnp.dot(conv_ref[...], wo_ref[0:w_conv, :], preferred_element_type=F32)
    h = h + jnp.dot(attn_ref[...], wo_ref[w_conv:, :], preferred_element_type=F32)

    m = (h * _rms_scale(h) * gmlp_ref[...]).astype(BF16)
    u = jnp.dot(m, wup_ref[...], preferred_element_type=F32)
    r = jnp.square(jnp.maximum(u, 0.0)).astype(BF16)
    h = h + jnp.dot(r, wdn_ref[...], preferred_element_type=F32)

    n = (h * _rms_scale(h) * gple_ref[...]).astype(BF16)
    gate = jax.nn.sigmoid(jnp.dot(n, wg_ref[...], preferred_element_type=F32))
    pp = jnp.dot(p_ref[...].astype(BF16), wp_ref[...], preferred_element_type=F32)
    h = h + gate * pp
    if apply_final:
        h = h * _rms_scale(h) * gfin_ref[...]
    o_ref[...] = h


def _post(x2, conv2, attn2, p2, w_out, g_mlp, w_up, w_down, g_ple, w_gate, w_proj, g_final,
          apply_final):
    t, d = x2.shape
    tm = TM_POST
    tile = lambda i: (i, 0)
    const = lambda i: (0, 0)
    wspec = lambda w: pl.BlockSpec(w.shape, const, pipeline_mode=pl.Buffered(1))
    gspec = pl.BlockSpec((1, d), const)
    return pl.pallas_call(
        functools.partial(_post_kernel, apply_final=apply_final),
        grid=(t // tm,),
        in_specs=[
            pl.BlockSpec((tm, d), tile),
            pl.BlockSpec((tm, conv2.shape[1]), tile),
            pl.BlockSpec((tm, attn2.shape[1]), tile),
            pl.BlockSpec((tm, p2.shape[1]), tile),
            wspec(w_out), gspec, wspec(w_up), wspec(w_down), gspec, wspec(w_gate), wspec(w_proj),
            gspec,
        ],
        out_specs=pl.BlockSpec((tm, d), tile),
        out_shape=jax.ShapeDtypeStruct((t, d), F32),
        compiler_params=pltpu.CompilerParams(
            dimension_semantics=("arbitrary",),
            vmem_limit_bytes=VMEM_LIMIT_BYTES),
        name="post",
    )(x2, conv2, attn2, p2, w_out, g_mlp, w_up, w_down, g_ple, w_gate, w_proj, g_final)


def _group_matrix(width):
    g = jnp.arange(width) // HEAD_DIM
    return jnp.where(g[:, None] == g[None, :], 1.0 / HEAD_DIM, 0.0).astype(BF16)


def kernel(x, p, g_mix, w_in, conv_w, g_conv_out, g_attn_out, w_out, g_mlp, w_up, w_down,
           g_ple, w_ple_gate, w_ple_proj, g_final):
    b, s, d = x.shape
    depth = p.shape[0]
    w_conv = conv_w.shape[-1]
    w_attn = g_attn_out.shape[-1]
    assert s % TM_IN == 0 and s % TQ == 0 and TQ == TK and (b * s) % TM_POST == 0
    assert w_attn % (2 * HEAD_DIM) == 0 and w_conv % HEAD_DIM == 0

    idx = jnp.arange(TK)
    tri = (idx[:, None] > idx[None, :]).astype(BF16)
    gmat_conv = _group_matrix(w_conv)
    gmat_pair = _group_matrix(2 * HEAD_DIM)
    row = lambda g: g.reshape(1, -1)

    h = x
    for i in range(depth):
        conv, q, k, v = _mixer_in(h, row(g_mix[i]), w_in[i].astype(BF16), conv_w[i],
                                  row(g_conv_out[i]), gmat_conv, w_conv, w_attn)
        attn = _attention(q, k, v, tri, row(g_attn_out[i]), gmat_pair)
        h = _post(h.reshape(b * s, d), conv.reshape(b * s, w_conv), attn.reshape(b * s, w_attn),
                  p[i].reshape(b * s, -1), w_out[i].astype(BF16), row(g_mlp[i]),
                  w_up[i].astype(BF16), w_down[i].astype(BF16), row(g_ple[i]),
                  w_ple_gate[i].astype(BF16), w_ple_proj[i].astype(BF16), row(g_final),
                  apply_final=(i == depth - 1)).reshape(b, s, d)
    return h
```

```python
import functools

import jax
import jax.numpy as jnp
from jax import lax
from jax.experimental import pallas as pl
from jax.experimental.pallas import tpu as pltpu

HEAD_DIM = 64
CONV_K = 3
EPS = 1e-6
LOG2_E = 1.4426950408889634

F32 = jnp.float32
BF16 = jnp.bfloat16

V7X_VMEM_BYTES = 64 * 1024 * 1024
VMEM_LIMIT_BYTES = V7X_VMEM_BYTES - 8 * 1024 * 1024
SUBLANES = 8

TM_IN = 512
TQ = 256
TK = 256
ATTN_HEADS_PER_STEP = 4
TM_POST = 256


def _rms_scale(x):
    return lax.rsqrt(jnp.mean(x * x, axis=-1, keepdims=True) + EPS)


def _group_mean(sq, gmat):
    hi = sq.astype(BF16)
    lo = (sq - hi.astype(F32)).astype(BF16)
    return (jnp.dot(hi, gmat, preferred_element_type=F32)
            + jnp.dot(lo, gmat, preferred_element_type=F32))


def _mixer_in_kernel(x_ref, g_ref, w_ref, cw_ref, gco_ref, gmat_ref,
                     conv_ref, q_ref, k_ref, v_ref, vbuf, *, w_conv, w_attn):
    tm = x_ref.shape[1]
    x = x_ref[0]
    a = (x * _rms_scale(x) * g_ref[...]).astype(BF16)
    proj = jnp.dot(a, w_ref[...], preferred_element_type=F32)

    cb = proj[:, 0:w_conv]
    v = proj[:, w_conv:2 * w_conv] * proj[:, 2 * w_conv:3 * w_conv]

    @pl.when(pl.program_id(1) == 0)
    def _():
        vbuf[0:SUBLANES, :] = jnp.zeros((SUBLANES, w_conv), F32)

    vbuf[SUBLANES:SUBLANES + tm, :] = v
    v1 = vbuf[SUBLANES - 1:SUBLANES - 1 + tm, :]
    v2 = vbuf[SUBLANES - 2:SUBLANES - 2 + tm, :]
    cw = cw_ref[...]
    y = cw[0:1, :] * v2 + cw[1:2, :] * v1 + cw[2:3, :] * v
    vbuf[0:SUBLANES, :] = vbuf[tm:tm + SUBLANES, :]

    c = cb * y
    ms = _group_mean(c * c, gmat_ref[...])
    conv_ref[0] = (c * lax.rsqrt(ms + EPS) * gco_ref[...]).astype(BF16)

    o = 3 * w_conv
    q_ref[0] = (proj[:, o:o + w_attn] * (HEAD_DIM ** -0.5 * LOG2_E)).astype(BF16)
    k_ref[0] = proj[:, o + w_attn:o + 2 * w_attn].astype(BF16)
    v_ref[0] = proj[:, o + 2 * w_attn:o + 3 * w_attn].astype(BF16)


def _mixer_in(x, g_mix, w_in_bf16, conv_w, g_conv_out, gmat, w_conv, w_attn):
    b, s, d = x.shape
    tm = TM_IN
    const = lambda *_: (0, 0)
    tile = lambda bi, i: (bi, i, 0)
    out_sd = lambda w: jax.ShapeDtypeStruct((b, s, w), BF16)
    return pl.pallas_call(
        functools.partial(_mixer_in_kernel, w_conv=w_conv, w_attn=w_attn),
        grid=(b, s // tm),
        in_specs=[
            pl.BlockSpec((1, tm, d), tile),
            pl.BlockSpec((1, d), const),
            pl.BlockSpec(w_in_bf16.shape, const),
            pl.BlockSpec((CONV_K, w_conv), const),
            pl.BlockSpec((1, w_conv), const),
            pl.BlockSpec((w_conv, w_conv), const),
        ],
        out_specs=[
            pl.BlockSpec((1, tm, w_conv), tile),
            pl.BlockSpec((1, tm, w_attn), tile),
            pl.BlockSpec((1, tm, w_attn), tile),
            pl.BlockSpec((1, tm, w_attn), tile),
        ],
        out_shape=[out_sd(w_conv), out_sd(w_attn), out_sd(w_attn), out_sd(w_attn)],
        scratch_shapes=[pltpu.VMEM((tm + SUBLANES, w_conv), F32)],
        compiler_params=pltpu.CompilerParams(
            dimension_semantics=("arbitrary", "arbitrary"),
            vmem_limit_bytes=VMEM_LIMIT_BYTES),
        name="mixer_in",
    )(x, g_mix, w_in_bf16, conv_w, g_conv_out, gmat)


def _attn_kernel(q_ref, k_ref, v_ref, tri_ref, g_ref, gmat_ref, o_ref, qst_ref, acc_ref):
    tq = q_ref.shape[1]
    tk = tri_ref.shape[0]
    width = q_ref.shape[2]
    heads = width // HEAD_DIM
    pair = 2 * HEAD_DIM
    i = pl.program_id(2)

    q = q_ref[0]
    lane = lax.broadcasted_iota(jnp.int32, (tq, width), 1)
    for hh in range(heads):
        in_head = (lane >= hh * HEAD_DIM) & (lane < (hh + 1) * HEAD_DIM)
        qst_ref[hh * tq:(hh + 1) * tq, :] = jnp.where(in_head, q, jnp.zeros_like(q))

    def key_block(j, carry, diag):
        start = pl.multiple_of(j * tk, tk)
        kj = k_ref[0, pl.ds(start, tk), :]
        vj = v_ref[0, pl.ds(start, tk), :]
        z = lax.dot_general(qst_ref[...], kj, (((1,), (1,)), ((), ())),
                            preferred_element_type=F32)
        sp = jnp.maximum(z, 0.0) + jnp.log2(1.0 + jnp.exp2(-jnp.abs(z)))
        if diag:
            row = lax.broadcasted_iota(jnp.int32, (heads * tq, tk), 0) & (tq - 1)
            col = lax.broadcasted_iota(jnp.int32, (heads * tq, tk), 1)
            causal = col < row
            sp = jnp.where(causal, sp, 0.0)
        c = jnp.dot(sp.astype(BF16), tri_ref[...], preferred_element_type=F32)
        a = jnp.exp2(z - (sp + c + carry))
        if diag:
            a = jnp.where(causal, a, 0.0)
        carry = carry + (c[:, 0:1] + sp[:, 0:1])
        ab = a.astype(BF16)
        for p in range(heads // 2):
            r0, r1 = 2 * p * tq, 2 * (p + 1) * tq
            av = jnp.dot(ab[r0:r1, :], vj[:, p * pair:(p + 1) * pair], preferred_element_type=F32)
            if diag:
                acc_ref[r0:r1, :] = av
            else:
                acc_ref[r0:r1, :] += av
        return carry

    carry = key_block(i, jnp.zeros((heads * tq, 1), F32), True)
    lax.fori_loop(0, i, lambda jj, cr: key_block(i - 1 - jj, cr, False), carry)

    lane_p = lax.broadcasted_iota(jnp.int32, (tq, pair), 1)
    out = jnp.concatenate(
        [jnp.where(lane_p < HEAD_DIM, acc_ref[2 * p * tq:(2 * p + 1) * tq, :],
                   acc_ref[(2 * p + 1) * tq:(2 * p + 2) * tq, :])
         for p in range(heads // 2)], axis=1)
    ms = _group_mean(out * out, gmat_ref[...])
    o_ref[0] = (out * lax.rsqrt(ms + EPS) * g_ref[...]).astype(BF16)


def _attention(q, k, v, tri, g_attn_out, gmat):
    b, s, w = q.shape
    hp = ATTN_HEADS_PER_STEP * HEAD_DIM
    return pl.pallas_call(
        _attn_kernel,
        grid=(b, w // hp, s // TQ),
        in_specs=[
            pl.BlockSpec((1, TQ, hp), lambda bi, p, i: (bi, i, p)),
            pl.BlockSpec((1, s, hp), lambda bi, p, i: (bi, 0, p)),
            pl.BlockSpec((1, s, hp), lambda bi, p, i: (bi, 0, p)),
            pl.BlockSpec((TK, TK), lambda bi, p, i: (0, 0)),
            pl.BlockSpec((1, hp), lambda bi, p, i: (0, p)),
            pl.BlockSpec((hp, hp), lambda bi, p, i: (0, 0)),
        ],
        out_specs=pl.BlockSpec((1, TQ, hp), lambda bi, p, i: (bi, i, p)),
        out_shape=jax.ShapeDtypeStruct((b, s, w), BF16),
        scratch_shapes=[pltpu.VMEM((ATTN_HEADS_PER_STEP * TQ, hp), BF16),
                        pltpu.VMEM((ATTN_HEADS_PER_STEP * TQ, 2 * HEAD_DIM), F32)],
        compiler_params=pltpu.CompilerParams(
            dimension_semantics=("arbitrary", "arbitrary", "arbitrary"),
            vmem_limit_bytes=VMEM_LIMIT_BYTES),
        name="sb_attention",
    )(q, k, v, tri, g_attn_out, gmat)


def _post_kernel(x_ref, conv_ref, attn_ref, p_ref, wo_ref, gmlp_ref, wup_ref, wdn_ref,
                 gple_ref, wg_ref, wp_ref, gfin_ref, o_ref, *, apply_final):
    w_conv = conv_ref.shape[1]
    h = x_ref[...]
    h = h + jnp.dot(conv_ref[...], wo_ref[0:w_conv, :], preferred_element_type=F32)
    h = h + jnp.dot(attn_ref[...], wo_ref[w_conv:, :], preferred_element_type=F32)

    m = (h * _rms_scale(h) * gmlp_ref[...]).astype(BF16)
    u = jnp.dot(m, wup_ref[...], preferred_element_type=F32)
    r = jnp.square(jnp.maximum(u, 0.0)).astype(BF16)
    h = h + jnp.dot(r, wdn_ref[...], preferred_element_type=F32)

    n = (h * _rms_scale(h) * gple_ref[...]).astype(BF16)
    gate = jax.nn.sigmoid(jnp.dot(n, wg_ref[...], preferred_element_type=F32))
    pp = jnp.dot(p_ref[...].astype(BF16), wp_ref[...], preferred_element_type=F32)
    h = h + gate * pp
    if apply_final:
        h = h * _rms_scale(h) * gfin_ref[...]
    o_ref[...] = h


def _post(x2, conv2, attn2, p2, w_out, g_mlp, w_up, w_down, g_ple, w_gate, w_proj, g_final,
          apply_final):
    t, d = x2.shape
    tm = TM_POST
    tile = lambda i: (i, 0)
    const = lambda i: (0, 0)
    wspec = lambda w: pl.BlockSpec(w.shape, const, pipeline_mode=pl.Buffered(1))
    gspec = pl.BlockSpec((1, d), const)
    return pl.pallas_call(
        functools.partial(_post_kernel, apply_final=apply_final),
        grid=(t // tm,),
        in_specs=[
            pl.BlockSpec((tm, d), tile),
            pl.BlockSpec((tm, conv2.shape[1]), tile),
            pl.BlockSpec((tm, attn2.shape[1]), tile),
            pl.BlockSpec((tm, p2.shape[1]), tile),
            wspec(w_out), gspec, wspec(w_up), wspec(w_down), gspec, wspec(w_gate), wspec(w_proj),
            gspec,
        ],
        out_specs=pl.BlockSpec((tm, d), tile),
        out_shape=jax.ShapeDtypeStruct((t, d), F32),
        compiler_params=pltpu.CompilerParams(
            dimension_semantics=("arbitrary",),
            vmem_limit_bytes=VMEM_LIMIT_BYTES),
        name="post",
    )(x2, conv2, attn2, p2, w_out, g_mlp, w_up, w_down, g_ple, w_gate, w_proj, g_final)


def _group_matrix(width):
    g = jnp.arange(width) // HEAD_DIM
    return jnp.where(g[:, None] == g[None, :], 1.0 / HEAD_DIM, 0.0).astype(BF16)


def kernel(x, p, g_mix, w_in, conv_w, g_conv_out, g_attn_out, w_out, g_mlp, w_up, w_down,
           g_ple, w_ple_gate, w_ple_proj, g_final):
    b, s, d = x.shape
    depth = p.shape[0]
    w_conv = conv_w.shape[-1]
    w_attn = g_attn_out.shape[-1]
    assert s % TM_IN == 0 and s % TQ == 0 and TQ == TK and (b * s) % TM_POST == 0
    assert w_attn % (ATTN_HEADS_PER_STEP * HEAD_DIM) == 0 and w_conv % HEAD_DIM == 0

    idx = jnp.arange(TK)
    tri = (idx[:, None] > idx[None, :]).astype(BF16)
    gmat_conv = _group_matrix(w_conv)
    gmat_attn = _group_matrix(ATTN_HEADS_PER_STEP * HEAD_DIM)
    row = lambda g: g.reshape(1, -1)

    h = x
    for i in range(depth):
        conv, q, k, v = _mixer_in(h, row(g_mix[i]), w_in[i].astype(BF16), conv_w[i],
                                  row(g_conv_out[i]), gmat_conv, w_conv, w_attn)
        attn = _attention(q, k, v, tri, row(g_attn_out[i]), gmat_attn)
        h = _post(h.reshape(b * s, d), conv.reshape(b * s, w_conv), attn.reshape(b * s, w_attn),
                  p[i].reshape(b * s, -1), w_out[i].astype(BF16), row(g_mlp[i]),
                  w_up[i].astype(BF16), w_down[i].astype(BF16), row(g_ple[i]),
                  w_ple_gate[i].astype(BF16), w_ple_proj[i].astype(BF16), row(g_final),
                  apply_final=(i == depth - 1)).reshape(b, s, d)
    return h
```

```python
import functools

import jax
import jax.numpy as jnp
from jax import lax
from jax.experimental import pallas as pl
from jax.experimental.pallas import tpu as pltpu

HEAD_DIM = 64
CONV_K = 3
EPS = 1e-6
LOG2_E = 1.4426950408889634

F32 = jnp.float32
BF16 = jnp.bfloat16

V7X_VMEM_BYTES = 64 * 1024 * 1024
VMEM_LIMIT_BYTES = V7X_VMEM_BYTES - 8 * 1024 * 1024
SUBLANES = 8

TM_IN = 512
TQ = 256
TK = 256
GROUP_HEADS = 4
TM_POST = 256


def _rms_scale(x):
    return lax.rsqrt(jnp.mean(x * x, axis=-1, keepdims=True) + EPS)


def _group_mean(sq, gmat):
    hi = sq.astype(BF16)
    lo = (sq - hi.astype(F32)).astype(BF16)
    return (jnp.dot(hi, gmat, preferred_element_type=F32)
            + jnp.dot(lo, gmat, preferred_element_type=F32))


def _mixer_in_kernel(x_ref, g_ref, w_ref, cw_ref, gco_ref, gmat_ref,
                     conv_ref, q_ref, k_ref, v_ref, vbuf, *, w_conv, w_attn):
    tm = x_ref.shape[1]
    x = x_ref[0]
    a = (x * _rms_scale(x) * g_ref[...]).astype(BF16)
    proj = jnp.dot(a, w_ref[...], preferred_element_type=F32)

    cb = proj[:, 0:w_conv]
    v = proj[:, w_conv:2 * w_conv] * proj[:, 2 * w_conv:3 * w_conv]

    @pl.when(pl.program_id(1) == 0)
    def _():
        vbuf[0:SUBLANES, :] = jnp.zeros((SUBLANES, w_conv), F32)

    vbuf[SUBLANES:SUBLANES + tm, :] = v
    v1 = vbuf[SUBLANES - 1:SUBLANES - 1 + tm, :]
    v2 = vbuf[SUBLANES - 2:SUBLANES - 2 + tm, :]
    cw = cw_ref[...]
    y = cw[0:1, :] * v2 + cw[1:2, :] * v1 + cw[2:3, :] * v
    vbuf[0:SUBLANES, :] = vbuf[tm:tm + SUBLANES, :]

    c = cb * y
    ms = _group_mean(c * c, gmat_ref[...])
    conv_ref[0] = (c * lax.rsqrt(ms + EPS) * gco_ref[...]).astype(BF16)

    o = 3 * w_conv
    q_ref[0] = (proj[:, o:o + w_attn] * (HEAD_DIM ** -0.5 * LOG2_E)).astype(BF16)
    k_ref[0] = proj[:, o + w_attn:o + 2 * w_attn].astype(BF16)
    v_ref[0] = proj[:, o + 2 * w_attn:o + 3 * w_attn].astype(BF16)


def _mixer_in(x, g_mix, w_in_bf16, conv_w, g_conv_out, gmat, w_conv, w_attn):
    b, s, d = x.shape
    tm = TM_IN
    const = lambda *_: (0, 0)
    tile = lambda bi, i: (bi, i, 0)
    out_sd = lambda w: jax.ShapeDtypeStruct((b, s, w), BF16)
    return pl.pallas_call(
        functools.partial(_mixer_in_kernel, w_conv=w_conv, w_attn=w_attn),
        grid=(b, s // tm),
        in_specs=[
            pl.BlockSpec((1, tm, d), tile),
            pl.BlockSpec((1, d), const),
            pl.BlockSpec(w_in_bf16.shape, const),
            pl.BlockSpec((CONV_K, w_conv), const),
            pl.BlockSpec((1, w_conv), const),
            pl.BlockSpec((w_conv, w_conv), const),
        ],
        out_specs=[
            pl.BlockSpec((1, tm, w_conv), tile),
            pl.BlockSpec((1, tm, w_attn), tile),
            pl.BlockSpec((1, tm, w_attn), tile),
            pl.BlockSpec((1, tm, w_attn), tile),
        ],
        out_shape=[out_sd(w_conv), out_sd(w_attn), out_sd(w_attn), out_sd(w_attn)],
        scratch_shapes=[pltpu.VMEM((tm + SUBLANES, w_conv), F32)],
        compiler_params=pltpu.CompilerParams(
            dimension_semantics=("arbitrary", "arbitrary"),
            vmem_limit_bytes=VMEM_LIMIT_BYTES),
        name="mixer_in",
    )(x, g_mix, w_in_bf16, conv_w, g_conv_out, gmat)


def _attn_kernel(q_ref, k_ref, v_ref, tri_ref, g_ref, gmat_ref, o_ref,
                 qst_ref, z_ref, ab_ref, acc_ref, carry_ref):
    tq = q_ref.shape[1]
    tk = tri_ref.shape[0]
    groups = qst_ref.shape[0]
    gw = qst_ref.shape[2]
    rows = qst_ref.shape[1]
    pair = 2 * HEAD_DIM
    i = pl.program_id(1)

    lane = lax.broadcasted_iota(jnp.int32, (tq, gw), 1)
    for g in range(groups):
        q = q_ref[0, :, g * gw:(g + 1) * gw]
        for hh in range(GROUP_HEADS):
            in_head = (lane >= hh * HEAD_DIM) & (lane < (hh + 1) * HEAD_DIM)
            qst_ref[g, hh * tq:(hh + 1) * tq, :] = jnp.where(in_head, q, jnp.zeros_like(q))

    def scores(g, j):
        start = pl.multiple_of(j * tk, tk)
        kj = k_ref[0, pl.ds(start, tk), g * gw:(g + 1) * gw]
        z_ref[g] = lax.dot_general(qst_ref[g], kj, (((1,), (1,)), ((), ())),
                                   preferred_element_type=F32)

    def weights(g, diag):
        z = z_ref[g]
        sp = jnp.maximum(z, 0.0) + jnp.log(1.0 + jnp.exp2(-jnp.abs(z))) * LOG2_E
        if diag:
            row = lax.broadcasted_iota(jnp.int32, (rows, tk), 0) & (tq - 1)
            col = lax.broadcasted_iota(jnp.int32, (rows, tk), 1)
            causal = col < row
            sp = jnp.where(causal, sp, 0.0)
        c = jnp.dot(sp.astype(BF16), tri_ref[...], preferred_element_type=F32)
        total = c[:, 0:1] + sp[:, 0:1]
        if diag:
            a = jnp.where(causal, jnp.exp2(z - (sp + c)), 0.0)
            carry_ref[g] = total
        else:
            carry = carry_ref[g]
            a = jnp.exp2(z - (sp + c + carry))
            carry_ref[g] = carry + total
        ab_ref[g] = a.astype(BF16)

    def accumulate(g, j):
        start = pl.multiple_of(j * tk, tk)
        for p in range(GROUP_HEADS // 2):
            r0, r1 = 2 * p * tq, 2 * (p + 1) * tq
            vj = v_ref[0, pl.ds(start, tk), g * gw + p * pair:g * gw + (p + 1) * pair]
            acc_ref[g, r0:r1, :] += jnp.dot(ab_ref[g, r0:r1, :], vj, preferred_element_type=F32)

    def body(jj, _):
        j = i - 1 - jj
        for g in range(groups):
            accumulate(g, j + 1)
            weights(g, False)
            scores(g, jnp.maximum(j - 1, 0))
        return 0

    acc_ref[...] = jnp.zeros(acc_ref.shape, F32)
    for g in range(groups):
        scores(g, i)
    for g in range(groups):
        weights(g, True)
        scores(g, jnp.maximum(i - 1, 0))
    lax.fori_loop(0, i, body, 0)
    for g in range(groups):
        accumulate(g, 0)

    lane_p = lax.broadcasted_iota(jnp.int32, (tq, pair), 1)
    out = jnp.concatenate(
        [jnp.where(lane_p < HEAD_DIM, acc_ref[g, 2 * p * tq:(2 * p + 1) * tq, :],
                   acc_ref[g, (2 * p + 1) * tq:(2 * p + 2) * tq, :])
         for g in range(groups) for p in range(GROUP_HEADS // 2)], axis=1)
    ms = _group_mean(out * out, gmat_ref[...])
    o_ref[0] = (out * lax.rsqrt(ms + EPS) * g_ref[...]).astype(BF16)


def _attention(q, k, v, tri, g_attn_out, gmat):
    b, s, w = q.shape
    gw = GROUP_HEADS * HEAD_DIM
    groups = w // gw
    rows = GROUP_HEADS * TQ
    return pl.pallas_call(
        _attn_kernel,
        grid=(b, s // TQ),
        in_specs=[
            pl.BlockSpec((1, TQ, w), lambda bi, i: (bi, i, 0)),
            pl.BlockSpec((1, s, w), lambda bi, i: (bi, 0, 0)),
            pl.BlockSpec((1, s, w), lambda bi, i: (bi, 0, 0)),
            pl.BlockSpec((TK, TK), lambda bi, i: (0, 0)),
            pl.BlockSpec((1, w), lambda bi, i: (0, 0)),
            pl.BlockSpec((w, w), lambda bi, i: (0, 0)),
        ],
        out_specs=pl.BlockSpec((1, TQ, w), lambda bi, i: (bi, i, 0)),
        out_shape=jax.ShapeDtypeStruct((b, s, w), BF16),
        scratch_shapes=[pltpu.VMEM((groups, rows, gw), BF16),
                        pltpu.VMEM((groups, rows, TK), F32),
                        pltpu.VMEM((groups, rows, TK), BF16),
                        pltpu.VMEM((groups, rows, 2 * HEAD_DIM), F32),
                        pltpu.VMEM((groups, rows, 1), F32)],
        compiler_params=pltpu.CompilerParams(
            dimension_semantics=("arbitrary", "arbitrary"),
            vmem_limit_bytes=VMEM_LIMIT_BYTES),
        name="sb_attention",
    )(q, k, v, tri, g_attn_out, gmat)


def _post_kernel(x_ref, conv_ref, attn_ref, p_ref, wo_ref, gmlp_ref, wup_ref, wdn_ref,
                 gple_ref, wg_ref, wp_ref, gfin_ref, o_ref, *, apply_final):
    w_conv = conv_ref.shape[1]
    h = x_ref[...]
    h = h + jnp.dot(conv_ref[...], wo_ref[0:w_conv, :], preferred_element_type=F32)
    h = h + jnp.dot(attn_ref[...], wo_ref[w_conv:, :], preferred_element_type=F32)

    m = (h * _rms_scale(h) * gmlp_ref[...]).astype(BF16)
    u = jnp.dot(m, wup_ref[...], preferred_element_type=F32)
    r = jnp.square(jnp.maximum(u, 0.0)).astype(BF16)
    h = h + jnp.dot(r, wdn_ref[...], preferred_element_type=F32)

    n = (h * _rms_scale(h) * gple_ref[...]).astype(BF16)
    gate = jax.nn.sigmoid(jnp.dot(n, wg_ref[...], preferred_element_type=F32))
    pp = jnp.dot(p_ref[...].astype(BF16), wp_ref[...], preferred_element_type=F32)
    h = h + gate * pp
    if apply_final:
        h = h * _rms_scale(h) * gfin_ref[...]
    o_ref[...] = h


def _post(x2, conv2, attn2, p2, w_out, g_mlp, w_up, w_down, g_ple, w_gate, w_proj, g_final,
          apply_final):
    t, d = x2.shape
    tm = TM_POST
    tile = lambda i: (i, 0)
    const = lambda i: (0, 0)
    wspec = lambda w: pl.BlockSpec(w.shape, const, pipeline_mode=pl.Buffered(1))
    gspec = pl.BlockSpec((1, d), const)
    return pl.pallas_call(
        functools.partial(_post_kernel, apply_final=apply_final),
        grid=(t // tm,),
        in_specs=[
            pl.BlockSpec((tm, d), tile),
            pl.BlockSpec((tm, conv2.shape[1]), tile),
            pl.BlockSpec((tm, attn2.shape[1]), tile),
            pl.BlockSpec((tm, p2.shape[1]), tile),
            wspec(w_out), gspec, wspec(w_up), wspec(w_down), gspec, wspec(w_gate), wspec(w_proj),
            gspec,
        ],
        out_specs=pl.BlockSpec((tm, d), tile),
        out_shape=jax.ShapeDtypeStruct((t, d), F32),
        compiler_params=pltpu.CompilerParams(
            dimension_semantics=("arbitrary",),
            vmem_limit_bytes=VMEM_LIMIT_BYTES),
        name="post",
    )(x2, conv2, attn2, p2, w_out, g_mlp, w_up, w_down, g_ple, w_gate, w_proj, g_final)


def _group_matrix(width):
    g = jnp.arange(width) // HEAD_DIM
    return jnp.where(g[:, None] == g[None, :], 1.0 / HEAD_DIM, 0.0).astype(BF16)


def kernel(x, p, g_mix, w_in, conv_w, g_conv_out, g_attn_out, w_out, g_mlp, w_up, w_down,
           g_ple, w_ple_gate, w_ple_proj, g_final):
    b, s, d = x.shape
    depth = p.shape[0]
    w_conv = conv_w.shape[-1]
    w_attn = g_attn_out.shape[-1]
    assert s % TM_IN == 0 and s % TQ == 0 and TQ == TK and (b * s) % TM_POST == 0
    assert w_attn % (GROUP_HEADS * HEAD_DIM) == 0 and w_conv % HEAD_DIM == 0

    idx = jnp.arange(TK)
    tri = (idx[:, None] > idx[None, :]).astype(BF16)
    gmat_conv = _group_matrix(w_conv)
    gmat_attn = _group_matrix(w_attn)
    row = lambda g: g.reshape(1, -1)

    h = x
    for i in range(depth):
        conv, q, k, v = _mixer_in(h, row(g_mix[i]), w_in[i].astype(BF16), conv_w[i],
                                  row(g_conv_out[i]), gmat_conv, w_conv, w_attn)
        attn = _attention(q, k, v, tri, row(g_attn_out[i]), gmat_attn)
        h = _post(h.reshape(b * s, d), conv.reshape(b * s, w_conv), attn.reshape(b * s, w_attn),
                  p[i].reshape(b * s, -1), w_out[i].astype(BF16), row(g_mlp[i]),
                  w_up[i].astype(BF16), w_down[i].astype(BF16), row(g_ple[i]),
                  w_ple_gate[i].astype(BF16), w_ple_proj[i].astype(BF16), row(g_final),
                  apply_final=(i == depth - 1)).reshape(b, s, d)
    return h
```

```python
import functools

import jax
import jax.numpy as jnp
from jax import lax
from jax.experimental import pallas as pl
from jax.experimental.pallas import tpu as pltpu

HEAD_DIM = 64
CONV_K = 3
EPS = 1e-6
LOG2_E = 1.4426950408889634
UNDERFLOW_BITS = 150.0

F32 = jnp.float32
BF16 = jnp.bfloat16

V7X_VMEM_BYTES = 64 * 1024 * 1024
VMEM_LIMIT_BYTES = V7X_VMEM_BYTES - 8 * 1024 * 1024
SUBLANES = 8

TM_IN = 512
TQ = 256
TK = 256
GROUP_HEADS = 4
TM_POST = 256


def _rms_scale(x):
    return lax.rsqrt(jnp.mean(x * x, axis=-1, keepdims=True) + EPS)


def _group_mean(sq, gmat):
    hi = sq.astype(BF16)
    lo = (sq - hi.astype(F32)).astype(BF16)
    return (jnp.dot(hi, gmat, preferred_element_type=F32)
            + jnp.dot(lo, gmat, preferred_element_type=F32))


def _mixer_in_kernel(x_ref, g_ref, w_ref, cw_ref, gco_ref, gmat_ref,
                     conv_ref, q_ref, k_ref, v_ref, vbuf, *, w_conv, w_attn):
    tm = x_ref.shape[1]
    x = x_ref[0]
    a = (x * _rms_scale(x) * g_ref[...]).astype(BF16)
    proj = jnp.dot(a, w_ref[...], preferred_element_type=F32)

    cb = proj[:, 0:w_conv]
    v = proj[:, w_conv:2 * w_conv] * proj[:, 2 * w_conv:3 * w_conv]

    @pl.when(pl.program_id(1) == 0)
    def _():
        vbuf[0:SUBLANES, :] = jnp.zeros((SUBLANES, w_conv), F32)

    vbuf[SUBLANES:SUBLANES + tm, :] = v
    v1 = vbuf[SUBLANES - 1:SUBLANES - 1 + tm, :]
    v2 = vbuf[SUBLANES - 2:SUBLANES - 2 + tm, :]
    cw = cw_ref[...]
    y = cw[0:1, :] * v2 + cw[1:2, :] * v1 + cw[2:3, :] * v
    vbuf[0:SUBLANES, :] = vbuf[tm:tm + SUBLANES, :]

    c = cb * y
    ms = _group_mean(c * c, gmat_ref[...])
    conv_ref[0] = (c * lax.rsqrt(ms + EPS) * gco_ref[...]).astype(BF16)

    o = 3 * w_conv
    q_ref[0] = (proj[:, o:o + w_attn] * (HEAD_DIM ** -0.5 * LOG2_E)).astype(BF16)
    k_ref[0] = proj[:, o + w_attn:o + 2 * w_attn].astype(BF16)
    v_ref[0] = proj[:, o + 2 * w_attn:o + 3 * w_attn].astype(BF16)


def _mixer_in(x, g_mix, w_in_bf16, conv_w, g_conv_out, gmat, w_conv, w_attn):
    b, s, d = x.shape
    tm = TM_IN
    const = lambda *_: (0, 0)
    tile = lambda bi, i: (bi, i, 0)
    out_sd = lambda w: jax.ShapeDtypeStruct((b, s, w), BF16)
    return pl.pallas_call(
        functools.partial(_mixer_in_kernel, w_conv=w_conv, w_attn=w_attn),
        grid=(b, s // tm),
        in_specs=[
            pl.BlockSpec((1, tm, d), tile),
            pl.BlockSpec((1, d), const),
            pl.BlockSpec(w_in_bf16.shape, const),
            pl.BlockSpec((CONV_K, w_conv), const),
            pl.BlockSpec((1, w_conv), const),
            pl.BlockSpec((w_conv, w_conv), const),
        ],
        out_specs=[
            pl.BlockSpec((1, tm, w_conv), tile),
            pl.BlockSpec((1, tm, w_attn), tile),
            pl.BlockSpec((1, tm, w_attn), tile),
            pl.BlockSpec((1, tm, w_attn), tile),
        ],
        out_shape=[out_sd(w_conv), out_sd(w_attn), out_sd(w_attn), out_sd(w_attn)],
        scratch_shapes=[pltpu.VMEM((tm + SUBLANES, w_conv), F32)],
        compiler_params=pltpu.CompilerParams(
            dimension_semantics=("arbitrary", "arbitrary"),
            vmem_limit_bytes=VMEM_LIMIT_BYTES),
        name="mixer_in",
    )(x, g_mix, w_in_bf16, conv_w, g_conv_out, gmat)


def _attn_kernel(q_ref, k_ref, v_ref, tri_ref, g_ref, gmat_ref, o_ref,
                 qst_ref, z_ref, ab_ref, acc_ref, carry_ref):
    tq = q_ref.shape[1]
    tk = tri_ref.shape[0]
    groups = qst_ref.shape[0]
    gw = qst_ref.shape[2]
    rows = qst_ref.shape[1]
    pair = 2 * HEAD_DIM
    i = pl.program_id(1)

    lane = lax.broadcasted_iota(jnp.int32, (tq, gw), 1)
    for g in range(groups):
        q = q_ref[0, :, g * gw:(g + 1) * gw]
        for hh in range(GROUP_HEADS):
            in_head = (lane >= hh * HEAD_DIM) & (lane < (hh + 1) * HEAD_DIM)
            qst_ref[g, hh * tq:(hh + 1) * tq, :] = jnp.where(in_head, q, jnp.zeros_like(q))

    def scores(g, j):
        start = pl.multiple_of(j * tk, tk)
        kj = k_ref[0, pl.ds(start, tk), g * gw:(g + 1) * gw]
        z_ref[g] = lax.dot_general(qst_ref[g], kj, (((1,), (1,)), ((), ())),
                                   preferred_element_type=F32)

    def weights(g, diag):
        z = z_ref[g]
        sp = jnp.maximum(z, 0.0) + jnp.log(1.0 + jnp.exp2(-jnp.abs(z))) * LOG2_E
        if diag:
            row = lax.broadcasted_iota(jnp.int32, (rows, tk), 0) & (tq - 1)
            col = lax.broadcasted_iota(jnp.int32, (rows, tk), 1)
            causal = col < row
            sp = jnp.where(causal, sp, 0.0)
        c = jnp.dot(sp.astype(BF16), tri_ref[...], preferred_element_type=F32)
        total = c[:, 0:1] + sp[:, 0:1]
        if diag:
            a = jnp.where(causal, jnp.exp2(z - (sp + c)), 0.0)
            new_carry = total
        else:
            carry = carry_ref[g]
            a = jnp.exp2(z - (sp + c + carry))
            new_carry = carry + total
        carry_ref[g] = new_carry
        ab_ref[g] = a.astype(BF16)
        return jnp.min(new_carry)

    def accumulate(g, j):
        start = pl.multiple_of(j * tk, tk)
        for p in range(GROUP_HEADS // 2):
            r0, r1 = 2 * p * tq, 2 * (p + 1) * tq
            vj = v_ref[0, pl.ds(start, tk), g * gw + p * pair:g * gw + (p + 1) * pair]
            acc_ref[g, r0:r1, :] += jnp.dot(ab_ref[g, r0:r1, :], vj, preferred_element_type=F32)

    def body(state):
        n, _ = state
        j = i - 1 - n
        low = []
        for g in range(groups):
            accumulate(g, j + 1)
            low.append(weights(g, False))
            scores(g, jnp.maximum(j - 1, 0))
        return n + 1, functools.reduce(jnp.minimum, low)

    def more(state):
        n, low = state
        return (n < i) & (low < UNDERFLOW_BITS)

    acc_ref[...] = jnp.zeros(acc_ref.shape, F32)
    for g in range(groups):
        scores(g, i)
    low = []
    for g in range(groups):
        low.append(weights(g, True))
        scores(g, jnp.maximum(i - 1, 0))
    n, _ = lax.while_loop(more, body, (jnp.int32(0), functools.reduce(jnp.minimum, low)))
    for g in range(groups):
        accumulate(g, i - n)

    lane_p = lax.broadcasted_iota(jnp.int32, (tq, pair), 1)
    out = jnp.concatenate(
        [jnp.where(lane_p < HEAD_DIM, acc_ref[g, 2 * p * tq:(2 * p + 1) * tq, :],
                   acc_ref[g, (2 * p + 1) * tq:(2 * p + 2) * tq, :])
         for g in range(groups) for p in range(GROUP_HEADS // 2)], axis=1)
    ms = _group_mean(out * out, gmat_ref[...])
    o_ref[0] = (out * lax.rsqrt(ms + EPS) * g_ref[...]).astype(BF16)


def _attention(q, k, v, tri, g_attn_out, gmat):
    b, s, w = q.shape
    gw = GROUP_HEADS * HEAD_DIM
    groups = w // gw
    rows = GROUP_HEADS * TQ
    return pl.pallas_call(
        _attn_kernel,
        grid=(b, s // TQ),
        in_specs=[
            pl.BlockSpec((1, TQ, w), lambda bi, i: (bi, i, 0)),
            pl.BlockSpec((1, s, w), lambda bi, i: (bi, 0, 0)),
            pl.BlockSpec((1, s, w), lambda bi, i: (bi, 0, 0)),
            pl.BlockSpec((TK, TK), lambda bi, i: (0, 0)),
            pl.BlockSpec((1, w), lambda bi, i: (0, 0)),
            pl.BlockSpec((w, w), lambda bi, i: (0, 0)),
        ],
        out_specs=pl.BlockSpec((1, TQ, w), lambda bi, i: (bi, i, 0)),
        out_shape=jax.ShapeDtypeStruct((b, s, w), BF16),
        scratch_shapes=[pltpu.VMEM((groups, rows, gw), BF16),
                        pltpu.VMEM((groups, rows, TK), F32),
                        pltpu.VMEM((groups, rows, TK), BF16),
                        pltpu.VMEM((groups, rows, 2 * HEAD_DIM), F32),
                        pltpu.VMEM((groups, rows, 1), F32)],
        compiler_params=pltpu.CompilerParams(
            dimension_semantics=("arbitrary", "arbitrary"),
            vmem_limit_bytes=VMEM_LIMIT_BYTES),
        name="sb_attention",
    )(q, k, v, tri, g_attn_out, gmat)


def _post_kernel(x_ref, conv_ref, attn_ref, p_ref, wo_ref, gmlp_ref, wup_ref, wdn_ref,
                 gple_ref, wg_ref, wp_ref, gfin_ref, o_ref, *, apply_final):
    w_conv = conv_ref.shape[1]
    h = x_ref[...]
    h = h + jnp.dot(conv_ref[...], wo_ref[0:w_conv, :], preferred_element_type=F32)
    h = h + jnp.dot(attn_ref[...], wo_ref[w_conv:, :], preferred_element_type=F32)

    m = (h * _rms_scale(h) * gmlp_ref[...]).astype(BF16)
    u = jnp.dot(m, wup_ref[...], preferred_element_type=F32)
    r = jnp.square(jnp.maximum(u, 0.0)).astype(BF16)
    h = h + jnp.dot(r, wdn_ref[...], preferred_element_type=F32)

    n = (h * _rms_scale(h) * gple_ref[...]).astype(BF16)
    gate = jax.nn.sigmoid(jnp.dot(n, wg_ref[...], preferred_element_type=F32))
    pp = jnp.dot(p_ref[...].astype(BF16), wp_ref[...], preferred_element_type=F32)
    h = h + gate * pp
    if apply_final:
        h = h * _rms_scale(h) * gfin_ref[...]
    o_ref[...] = h


def _post(x2, conv2, attn2, p2, w_out, g_mlp, w_up, w_down, g_ple, w_gate, w_proj, g_final,
          apply_final):
    t, d = x2.shape
    tm = TM_POST
    tile = lambda i: (i, 0)
    const = lambda i: (0, 0)
    wspec = lambda w: pl.BlockSpec(w.shape, const, pipeline_mode=pl.Buffered(1))
    gspec = pl.BlockSpec((1, d), const)
    return pl.pallas_call(
        functools.partial(_post_kernel, apply_final=apply_final),
        grid=(t // tm,),
        in_specs=[
            pl.BlockSpec((tm, d), tile),
            pl.BlockSpec((tm, conv2.shape[1]), tile),
            pl.BlockSpec((tm, attn2.shape[1]), tile),
            pl.BlockSpec((tm, p2.shape[1]), tile),
            wspec(w_out), gspec, wspec(w_up), wspec(w_down), gspec, wspec(w_gate), wspec(w_proj),
            gspec,
        ],
        out_specs=pl.BlockSpec((tm, d), tile),
        out_shape=jax.ShapeDtypeStruct((t, d), F32),
        compiler_params=pltpu.CompilerParams(
            dimension_semantics=("arbitrary",),
            vmem_limit_bytes=VMEM_LIMIT_BYTES),
        name="post",
    )(x2, conv2, attn2, p2, w_out, g_mlp, w_up, w_down, g_ple, w_gate, w_proj, g_final)


def _group_matrix(width):
    g = jnp.arange(width) // HEAD_DIM
    return jnp.where(g[:, None] == g[None, :], 1.0 / HEAD_DIM, 0.0).astype(BF16)


def kernel(x, p, g_mix, w_in, conv_w, g_conv_out, g_attn_out, w_out, g_mlp, w_up, w_down,
           g_ple, w_ple_gate, w_ple_proj, g_final):
    b, s, d = x.shape
    depth = p.shape[0]
    w_conv = conv_w.shape[-1]
    w_attn = g_attn_out.shape[-1]
    assert s % TM_IN == 0 and s % TQ == 0 and TQ == TK and (b * s) % TM_POST == 0
    assert w_attn % (GROUP_HEADS * HEAD_DIM) == 0 and w_conv % HEAD_DIM == 0

    idx = jnp.arange(TK)
    tri = (idx[:, None] > idx[None, :]).astype(BF16)
    gmat_conv = _group_matrix(w_conv)
    gmat_attn = _group_matrix(w_attn)
    row = lambda g: g.reshape(1, -1)

    h = x
    for i in range(depth):
        conv, q, k, v = _mixer_in(h, row(g_mix[i]), w_in[i].astype(BF16), conv_w[i],
                                  row(g_conv_out[i]), gmat_conv, w_conv, w_attn)
        attn = _attention(q, k, v, tri, row(g_attn_out[i]), gmat_attn)
        h = _post(h.reshape(b * s, d), conv.reshape(b * s, w_conv), attn.reshape(b * s, w_attn),
                  p[i].reshape(b * s, -1), w_out[i].astype(BF16), row(g_mlp[i]),
                  w_up[i].astype(BF16), w_down[i].astype(BF16), row(g_ple[i]),
                  w_ple_gate[i].astype(BF16), w_ple_proj[i].astype(BF16), row(g_final),
                  apply_final=(i == depth - 1)).reshape(b, s, d)
    return h
```

```python
import functools

import jax
import jax.numpy as jnp
from jax import lax
from jax.experimental import pallas as pl
from jax.experimental.pallas import tpu as pltpu

HEAD_DIM = 64
CONV_K = 3
EPS = 1e-6
LOG2_E = 1.4426950408889634
UNDERFLOW_BITS = 150.0

F32 = jnp.float32
BF16 = jnp.bfloat16

V7X_VMEM_BYTES = 64 * 1024 * 1024
VMEM_LIMIT_BYTES = V7X_VMEM_BYTES - 8 * 1024 * 1024
SUBLANES = 8

TM_IN = 1024
IN_SUB_ROWS = 256
TQ = 256
TK = 256
GROUP_HEADS = 4
TM_POST = 512
POST_SUB_ROWS = 256


def _rms_scale(x):
    return lax.rsqrt(jnp.mean(x * x, axis=-1, keepdims=True) + EPS)


def _group_mean(sq, gmat):
    hi = sq.astype(BF16)
    lo = (sq - hi.astype(F32)).astype(BF16)
    return (jnp.dot(hi, gmat, preferred_element_type=F32)
            + jnp.dot(lo, gmat, preferred_element_type=F32))


def _mixer_in_kernel(x_ref, g_ref, w_ref, cw_ref, gco_ref, gmat_ref,
                     conv_ref, q_ref, k_ref, v_ref, vbuf, *, w_conv, w_attn):
    tm = x_ref.shape[1]
    @pl.when(pl.program_id(1) == 0)
    def _():
        vbuf[0:SUBLANES, :] = jnp.zeros((SUBLANES, w_conv), F32)

    starts = list(range(0, tm, IN_SUB_ROWS))
    a = []
    for r0 in starts:
        x = x_ref[0, r0:r0 + IN_SUB_ROWS, :]
        a.append((x * _rms_scale(x) * g_ref[...]).astype(BF16))
    proj = [jnp.dot(as_, w_ref[...], preferred_element_type=F32) for as_ in a]

    o = 3 * w_conv
    for r0, pr in zip(starts, proj):
        rs = slice(r0, r0 + IN_SUB_ROWS)
        vbuf[SUBLANES + r0:SUBLANES + r0 + IN_SUB_ROWS, :] = (
            pr[:, w_conv:2 * w_conv] * pr[:, 2 * w_conv:3 * w_conv])
        q_ref[0, rs, :] = (pr[:, o:o + w_attn] * (HEAD_DIM ** -0.5 * LOG2_E)).astype(BF16)
        k_ref[0, rs, :] = pr[:, o + w_attn:o + 2 * w_attn].astype(BF16)
        v_ref[0, rs, :] = pr[:, o + 2 * w_attn:o + 3 * w_attn].astype(BF16)

    cw = cw_ref[...]
    for r0, pr in zip(starts, proj):
        v0 = vbuf[SUBLANES + r0:SUBLANES + r0 + IN_SUB_ROWS, :]
        v1 = vbuf[SUBLANES - 1 + r0:SUBLANES - 1 + r0 + IN_SUB_ROWS, :]
        v2 = vbuf[SUBLANES - 2 + r0:SUBLANES - 2 + r0 + IN_SUB_ROWS, :]
        c = pr[:, 0:w_conv] * (cw[0:1, :] * v2 + cw[1:2, :] * v1 + cw[2:3, :] * v0)
        ms = _group_mean(c * c, gmat_ref[...])
        conv_ref[0, r0:r0 + IN_SUB_ROWS, :] = (
            c * lax.rsqrt(ms + EPS) * gco_ref[...]).astype(BF16)
    vbuf[0:SUBLANES, :] = vbuf[tm:tm + SUBLANES, :]


def _mixer_in(x, g_mix, w_in_bf16, conv_w, g_conv_out, gmat, w_conv, w_attn):
    b, s, d = x.shape
    tm = TM_IN
    const = lambda *_: (0, 0)
    tile = lambda bi, i: (bi, i, 0)
    out_sd = lambda w: jax.ShapeDtypeStruct((b, s, w), BF16)
    return pl.pallas_call(
        functools.partial(_mixer_in_kernel, w_conv=w_conv, w_attn=w_attn),
        grid=(b, s // tm),
        in_specs=[
            pl.BlockSpec((1, tm, d), tile),
            pl.BlockSpec((1, d), const),
            pl.BlockSpec(w_in_bf16.shape, const),
            pl.BlockSpec((CONV_K, w_conv), const),
            pl.BlockSpec((1, w_conv), const),
            pl.BlockSpec((w_conv, w_conv), const),
        ],
        out_specs=[
            pl.BlockSpec((1, tm, w_conv), tile),
            pl.BlockSpec((1, tm, w_attn), tile),
            pl.BlockSpec((1, tm, w_attn), tile),
            pl.BlockSpec((1, tm, w_attn), tile),
        ],
        out_shape=[out_sd(w_conv), out_sd(w_attn), out_sd(w_attn), out_sd(w_attn)],
        scratch_shapes=[pltpu.VMEM((tm + SUBLANES, w_conv), F32)],
        compiler_params=pltpu.CompilerParams(
            dimension_semantics=("arbitrary", "arbitrary"),
            vmem_limit_bytes=VMEM_LIMIT_BYTES),
        name="mixer_in",
    )(x, g_mix, w_in_bf16, conv_w, g_conv_out, gmat)


def _attn_kernel(q_ref, k_ref, v_ref, tri_ref, g_ref, gmat_ref, o_ref,
                 qst_ref, z_ref, ab_ref, acc_ref, carry_ref):
    tq = q_ref.shape[1]
    tk = tri_ref.shape[0]
    groups = qst_ref.shape[0]
    gw = qst_ref.shape[2]
    rows = qst_ref.shape[1]
    pair = 2 * HEAD_DIM
    i = pl.program_id(1)

    lane = lax.broadcasted_iota(jnp.int32, (tq, gw), 1)
    for g in range(groups):
        q = q_ref[0, :, g * gw:(g + 1) * gw]
        for hh in range(GROUP_HEADS):
            in_head = (lane >= hh * HEAD_DIM) & (lane < (hh + 1) * HEAD_DIM)
            qst_ref[g, hh * tq:(hh + 1) * tq, :] = jnp.where(in_head, q, jnp.zeros_like(q))

    def scores(g, j):
        start = pl.multiple_of(j * tk, tk)
        kj = k_ref[0, pl.ds(start, tk), g * gw:(g + 1) * gw]
        z_ref[g] = lax.dot_general(qst_ref[g], kj, (((1,), (1,)), ((), ())),
                                   preferred_element_type=F32)

    def weights(g, diag):
        z = z_ref[g]
        sp = jnp.maximum(z, 0.0) + jnp.log(1.0 + jnp.exp2(-jnp.abs(z))) * LOG2_E
        if diag:
            row = lax.broadcasted_iota(jnp.int32, (rows, tk), 0) & (tq - 1)
            col = lax.broadcasted_iota(jnp.int32, (rows, tk), 1)
            causal = col < row
            sp = jnp.where(causal, sp, 0.0)
        c = jnp.dot(sp.astype(BF16), tri_ref[...], preferred_element_type=F32)
        total = c[:, 0:1] + sp[:, 0:1]
        if diag:
            a = jnp.where(causal, jnp.exp2(z - (sp + c)), 0.0)
            new_carry = total
        else:
            carry = carry_ref[g]
            a = jnp.exp2(z - (sp + c + carry))
            new_carry = carry + total
        carry_ref[g] = new_carry
        ab_ref[g] = a.astype(BF16)
        return jnp.min(new_carry)

    def accumulate(g, j):
        start = pl.multiple_of(j * tk, tk)
        for p in range(GROUP_HEADS // 2):
            r0, r1 = 2 * p * tq, 2 * (p + 1) * tq
            vj = v_ref[0, pl.ds(start, tk), g * gw + p * pair:g * gw + (p + 1) * pair]
            acc_ref[g, r0:r1, :] += jnp.dot(ab_ref[g, r0:r1, :], vj, preferred_element_type=F32)

    def body(state):
        n, _ = state
        j = i - 1 - n
        low = []
        for g in range(groups):
            accumulate(g, j + 1)
            low.append(weights(g, False))
            scores(g, jnp.maximum(j - 1, 0))
        return n + 1, functools.reduce(jnp.minimum, low)

    def more(state):
        n, low = state
        return (n < i) & (low < UNDERFLOW_BITS)

    acc_ref[...] = jnp.zeros(acc_ref.shape, F32)
    for g in range(groups):
        scores(g, i)
    low = []
    for g in range(groups):
        low.append(weights(g, True))
        scores(g, jnp.maximum(i - 1, 0))
    n, _ = lax.while_loop(more, body, (jnp.int32(0), functools.reduce(jnp.minimum, low)))
    for g in range(groups):
        accumulate(g, i - n)

    lane_p = lax.broadcasted_iota(jnp.int32, (tq, pair), 1)
    out = jnp.concatenate(
        [jnp.where(lane_p < HEAD_DIM, acc_ref[g, 2 * p * tq:(2 * p + 1) * tq, :],
                   acc_ref[g, (2 * p + 1) * tq:(2 * p + 2) * tq, :])
         for g in range(groups) for p in range(GROUP_HEADS // 2)], axis=1)
    ms = _group_mean(out * out, gmat_ref[...])
    o_ref[0] = (out * lax.rsqrt(ms + EPS) * g_ref[...]).astype(BF16)


def _attention(q, k, v, tri, g_attn_out, gmat):
    b, s, w = q.shape
    gw = GROUP_HEADS * HEAD_DIM
    groups = w // gw
    rows = GROUP_HEADS * TQ
    return pl.pallas_call(
        _attn_kernel,
        grid=(b, s // TQ),
        in_specs=[
            pl.BlockSpec((1, TQ, w), lambda bi, i: (bi, i, 0)),
            pl.BlockSpec((1, s, w), lambda bi, i: (bi, 0, 0)),
            pl.BlockSpec((1, s, w), lambda bi, i: (bi, 0, 0)),
            pl.BlockSpec((TK, TK), lambda bi, i: (0, 0)),
            pl.BlockSpec((1, w), lambda bi, i: (0, 0)),
            pl.BlockSpec((w, w), lambda bi, i: (0, 0)),
        ],
        out_specs=pl.BlockSpec((1, TQ, w), lambda bi, i: (bi, i, 0)),
        out_shape=jax.ShapeDtypeStruct((b, s, w), BF16),
        scratch_shapes=[pltpu.VMEM((groups, rows, gw), BF16),
                        pltpu.VMEM((groups, rows, TK), F32),
                        pltpu.VMEM((groups, rows, TK), BF16),
                        pltpu.VMEM((groups, rows, 2 * HEAD_DIM), F32),
                        pltpu.VMEM((groups, rows, 1), F32)],
        compiler_params=pltpu.CompilerParams(
            dimension_semantics=("arbitrary", "arbitrary"),
            vmem_limit_bytes=VMEM_LIMIT_BYTES),
        name="sb_attention",
    )(q, k, v, tri, g_attn_out, gmat)


def _post_kernel(x_ref, conv_ref, attn_ref, p_ref, wo_ref, gmlp_ref, wup_ref, wdn_ref,
                 gple_ref, wg_ref, wp_ref, gfin_ref, o_ref, *, apply_final):
    w_conv = conv_ref.shape[1]
    subs = [slice(r0, r0 + POST_SUB_ROWS) for r0 in range(0, x_ref.shape[0], POST_SUB_ROWS)]
    dot = functools.partial(jnp.dot, preferred_element_type=F32)
    h = [x_ref[rs, :] + dot(conv_ref[rs, :], wo_ref[0:w_conv, :]) for rs in subs]
    h = [hs + dot(attn_ref[rs, :], wo_ref[w_conv:, :]) for hs, rs in zip(h, subs)]
    m = [(hs * _rms_scale(hs) * gmlp_ref[...]).astype(BF16) for hs in h]
    r = [jnp.square(jnp.maximum(dot(ms, wup_ref[...]), 0.0)).astype(BF16) for ms in m]
    h = [hs + dot(rr, wdn_ref[...]) for hs, rr in zip(h, r)]
    n = [(hs * _rms_scale(hs) * gple_ref[...]).astype(BF16) for hs in h]
    pp = [dot(p_ref[rs, :].astype(BF16), wp_ref[...]) for rs in subs]
    gate = [jax.nn.sigmoid(dot(ns, wg_ref[...])) for ns in n]
    h = [hs + gs * ps for hs, gs, ps in zip(h, gate, pp)]
    for hs, rs in zip(h, subs):
        if apply_final:
            hs = hs * _rms_scale(hs) * gfin_ref[...]
        o_ref[rs, :] = hs


def _post(x2, conv2, attn2, p2, w_out, g_mlp, w_up, w_down, g_ple, w_gate, w_proj, g_final,
          apply_final):
    t, d = x2.shape
    tm = TM_POST
    tile = lambda i: (i, 0)
    const = lambda i: (0, 0)
    wspec = lambda w: pl.BlockSpec(w.shape, const, pipeline_mode=pl.Buffered(1))
    gspec = pl.BlockSpec((1, d), const)
    return pl.pallas_call(
        functools.partial(_post_kernel, apply_final=apply_final),
        grid=(t // tm,),
        in_specs=[
            pl.BlockSpec((tm, d), tile),
            pl.BlockSpec((tm, conv2.shape[1]), tile),
            pl.BlockSpec((tm, attn2.shape[1]), tile),
            pl.BlockSpec((tm, p2.shape[1]), tile),
            wspec(w_out), gspec, wspec(w_up), wspec(w_down), gspec, wspec(w_gate), wspec(w_proj),
            gspec,
        ],
        out_specs=pl.BlockSpec((tm, d), tile),
        out_shape=jax.ShapeDtypeStruct((t, d), F32),
        compiler_params=pltpu.CompilerParams(
            dimension_semantics=("arbitrary",),
            vmem_limit_bytes=VMEM_LIMIT_BYTES),
        name="post",
    )(x2, conv2, attn2, p2, w_out, g_mlp, w_up, w_down, g_ple, w_gate, w_proj, g_final)


def _group_matrix(width):
    g = jnp.arange(width) // HEAD_DIM
    return jnp.where(g[:, None] == g[None, :], 1.0 / HEAD_DIM, 0.0).astype(BF16)


def kernel(x, p, g_mix, w_in, conv_w, g_conv_out, g_attn_out, w_out, g_mlp, w_up, w_down,
           g_ple, w_ple_gate, w_ple_proj, g_final):
    b, s, d = x.shape
    depth = p.shape[0]
    w_conv = conv_w.shape[-1]
    w_attn = g_attn_out.shape[-1]
    assert s % TM_IN == 0 and s % TQ == 0 and TQ == TK and (b * s) % TM_POST == 0
    assert w_attn % (GROUP_HEADS * HEAD_DIM) == 0 and w_conv % HEAD_DIM == 0

    idx = jnp.arange(TK)
    tri = (idx[:, None] > idx[None, :]).astype(BF16)
    gmat_conv = _group_matrix(w_conv)
    gmat_attn = _group_matrix(w_attn)
    row = lambda g: g.reshape(1, -1)

    h = x
    for i in range(depth):
        conv, q, k, v = _mixer_in(h, row(g_mix[i]), w_in[i].astype(BF16), conv_w[i],
                                  row(g_conv_out[i]), gmat_conv, w_conv, w_attn)
        attn = _attention(q, k, v, tri, row(g_attn_out[i]), gmat_attn)
        h = _post(h.reshape(b * s, d), conv.reshape(b * s, w_conv), attn.reshape(b * s, w_attn),
                  p[i].reshape(b * s, -1), w_out[i].astype(BF16), row(g_mlp[i]),
                  w_up[i].astype(BF16), w_down[i].astype(BF16), row(g_ple[i]),
                  w_ple_gate[i].astype(BF16), w_ple_proj[i].astype(BF16), row(g_final),
                  apply_final=(i == depth - 1)).reshape(b, s, d)
    return h
```

```python
import functools

import jax
import jax.numpy as jnp
from jax import lax
from jax.experimental import pallas as pl
from jax.experimental.pallas import tpu as pltpu

HEAD_DIM = 64
CONV_K = 3
EPS = 1e-6
LOG2_E = 1.4426950408889634
UNDERFLOW_BITS = 150.0

F32 = jnp.float32
BF16 = jnp.bfloat16

V7X_VMEM_BYTES = 64 * 1024 * 1024
VMEM_LIMIT_BYTES = V7X_VMEM_BYTES - 8 * 1024 * 1024
SUBLANES = 8

TM_IN = 1024
IN_SUB_ROWS = 256
TQ = 256
TK = 256
GROUP_HEADS = 4
ATTN_BATCHES = 2
TM_POST = 512
POST_SUB_ROWS = 256


def _rms_scale(x):
    return lax.rsqrt(jnp.mean(x * x, axis=-1, keepdims=True) + EPS)


def _group_mean(sq, gmat):
    hi = sq.astype(BF16)
    lo = (sq - hi.astype(F32)).astype(BF16)
    return (jnp.dot(hi, gmat, preferred_element_type=F32)
            + jnp.dot(lo, gmat, preferred_element_type=F32))


def _mixer_in_kernel(x_ref, g_ref, w_ref, cw_ref, gco_ref, gmat_ref,
                     conv_ref, q_ref, k_ref, v_ref, vbuf, *, w_conv, w_attn):
    tm = x_ref.shape[1]
    @pl.when(pl.program_id(1) == 0)
    def _():
        vbuf[0:SUBLANES, :] = jnp.zeros((SUBLANES, w_conv), F32)

    starts = list(range(0, tm, IN_SUB_ROWS))
    a = []
    for r0 in starts:
        x = x_ref[0, r0:r0 + IN_SUB_ROWS, :]
        a.append((x * _rms_scale(x) * g_ref[...]).astype(BF16))
    proj = [jnp.dot(as_, w_ref[...], preferred_element_type=F32) for as_ in a]

    o = 3 * w_conv
    for r0, pr in zip(starts, proj):
        rs = slice(r0, r0 + IN_SUB_ROWS)
        vbuf[SUBLANES + r0:SUBLANES + r0 + IN_SUB_ROWS, :] = (
            pr[:, w_conv:2 * w_conv] * pr[:, 2 * w_conv:3 * w_conv])
        q_ref[0, rs, :] = (pr[:, o:o + w_attn] * (HEAD_DIM ** -0.5 * LOG2_E)).astype(BF16)
        k_ref[0, rs, :] = pr[:, o + w_attn:o + 2 * w_attn].astype(BF16)
        v_ref[0, rs, :] = pr[:, o + 2 * w_attn:o + 3 * w_attn].astype(BF16)

    cw = cw_ref[...]
    for r0, pr in zip(starts, proj):
        v0 = vbuf[SUBLANES + r0:SUBLANES + r0 + IN_SUB_ROWS, :]
        v1 = vbuf[SUBLANES - 1 + r0:SUBLANES - 1 + r0 + IN_SUB_ROWS, :]
        v2 = vbuf[SUBLANES - 2 + r0:SUBLANES - 2 + r0 + IN_SUB_ROWS, :]
        c = pr[:, 0:w_conv] * (cw[0:1, :] * v2 + cw[1:2, :] * v1 + cw[2:3, :] * v0)
        ms = _group_mean(c * c, gmat_ref[...])
        conv_ref[0, r0:r0 + IN_SUB_ROWS, :] = (
            c * lax.rsqrt(ms + EPS) * gco_ref[...]).astype(BF16)
    vbuf[0:SUBLANES, :] = vbuf[tm:tm + SUBLANES, :]


def _mixer_in(x, g_mix, w_in_bf16, conv_w, g_conv_out, gmat, w_conv, w_attn):
    b, s, d = x.shape
    tm = TM_IN
    const = lambda *_: (0, 0)
    tile = lambda bi, i: (bi, i, 0)
    out_sd = lambda w: jax.ShapeDtypeStruct((b, s, w), BF16)
    return pl.pallas_call(
        functools.partial(_mixer_in_kernel, w_conv=w_conv, w_attn=w_attn),
        grid=(b, s // tm),
        in_specs=[
            pl.BlockSpec((1, tm, d), tile),
            pl.BlockSpec((1, d), const),
            pl.BlockSpec(w_in_bf16.shape, const),
            pl.BlockSpec((CONV_K, w_conv), const),
            pl.BlockSpec((1, w_conv), const),
            pl.BlockSpec((w_conv, w_conv), const),
        ],
        out_specs=[
            pl.BlockSpec((1, tm, w_conv), tile),
            pl.BlockSpec((1, tm, w_attn), tile),
            pl.BlockSpec((1, tm, w_attn), tile),
            pl.BlockSpec((1, tm, w_attn), tile),
        ],
        out_shape=[out_sd(w_conv), out_sd(w_attn), out_sd(w_attn), out_sd(w_attn)],
        scratch_shapes=[pltpu.VMEM((tm + SUBLANES, w_conv), F32)],
        compiler_params=pltpu.CompilerParams(
            dimension_semantics=("arbitrary", "arbitrary"),
            vmem_limit_bytes=VMEM_LIMIT_BYTES),
        name="mixer_in",
    )(x, g_mix, w_in_bf16, conv_w, g_conv_out, gmat)


def _attn_kernel(q_ref, k_ref, v_ref, tri_ref, g_ref, gmat_ref, o_ref,
                 qst_ref, z_ref, ab_ref, acc_ref, carry_ref):
    nb, tq = q_ref.shape[0], q_ref.shape[1]
    tk = tri_ref.shape[0]
    chains, rows, gw = qst_ref.shape
    groups = chains // nb
    pair = 2 * HEAD_DIM
    i = pl.program_id(1)
    where = [(c // groups, (c % groups) * gw) for c in range(chains)]

    lane = lax.broadcasted_iota(jnp.int32, (tq, gw), 1)
    for c, (bl, l0) in enumerate(where):
        q = q_ref[bl, :, l0:l0 + gw]
        for hh in range(GROUP_HEADS):
            in_head = (lane >= hh * HEAD_DIM) & (lane < (hh + 1) * HEAD_DIM)
            qst_ref[c, hh * tq:(hh + 1) * tq, :] = jnp.where(in_head, q, jnp.zeros_like(q))

    def scores(c, j):
        bl, l0 = where[c]
        start = pl.multiple_of(j * tk, tk)
        kj = k_ref[bl, pl.ds(start, tk), l0:l0 + gw]
        z_ref[c] = lax.dot_general(qst_ref[c], kj, (((1,), (1,)), ((), ())),
                                   preferred_element_type=F32)

    def weights(c, diag):
        z = z_ref[c]
        sp = jnp.maximum(z, 0.0) + jnp.log(1.0 + jnp.exp2(-jnp.abs(z))) * LOG2_E
        if diag:
            row = lax.broadcasted_iota(jnp.int32, (rows, tk), 0) & (tq - 1)
            col = lax.broadcasted_iota(jnp.int32, (rows, tk), 1)
            causal = col < row
            sp = jnp.where(causal, sp, 0.0)
        cs = jnp.dot(sp.astype(BF16), tri_ref[...], preferred_element_type=F32)
        total = cs[:, 0:1] + sp[:, 0:1]
        if diag:
            a = jnp.where(causal, jnp.exp2(z - (sp + cs)), 0.0)
            new_carry = total
        else:
            carry = carry_ref[c]
            a = jnp.exp2(z - (sp + cs + carry))
            new_carry = carry + total
        carry_ref[c] = new_carry
        ab_ref[c] = a.astype(BF16)
        return jnp.min(new_carry)

    def accumulate(c, j):
        bl, l0 = where[c]
        start = pl.multiple_of(j * tk, tk)
        for p in range(GROUP_HEADS // 2):
            r0, r1 = 2 * p * tq, 2 * (p + 1) * tq
            vj = v_ref[bl, pl.ds(start, tk), l0 + p * pair:l0 + (p + 1) * pair]
            acc_ref[c, r0:r1, :] += jnp.dot(ab_ref[c, r0:r1, :], vj, preferred_element_type=F32)

    last = chains - 1

    def body(state):
        n, _ = state
        j = i - 1 - n
        accumulate(last, j + 1)
        low = []
        for c in range(chains):
            low.append(weights(c, False))
            if c != last:
                accumulate(c, j)
            scores(c, jnp.maximum(j - 1, 0))
        return n + 1, functools.reduce(jnp.minimum, low)

    def more(state):
        n, low = state
        return (n < i) & (low < UNDERFLOW_BITS)

    acc_ref[...] = jnp.zeros(acc_ref.shape, F32)
    for c in range(chains):
        scores(c, i)
    low = []
    for c in range(chains):
        low.append(weights(c, True))
        if c != last:
            accumulate(c, i)
        scores(c, jnp.maximum(i - 1, 0))
    n, _ = lax.while_loop(more, body, (jnp.int32(0), functools.reduce(jnp.minimum, low)))
    accumulate(last, i - n)

    lane_p = lax.broadcasted_iota(jnp.int32, (tq, pair), 1)
    for bl in range(nb):
        out = jnp.concatenate(
            [jnp.where(lane_p < HEAD_DIM, acc_ref[c, 2 * p * tq:(2 * p + 1) * tq, :],
                       acc_ref[c, (2 * p + 1) * tq:(2 * p + 2) * tq, :])
             for c in range(bl * groups, (bl + 1) * groups) for p in range(GROUP_HEADS // 2)],
            axis=1)
        ms = _group_mean(out * out, gmat_ref[...])
        o_ref[bl] = (out * lax.rsqrt(ms + EPS) * g_ref[...]).astype(BF16)


def _attention(q, k, v, tri, g_attn_out, gmat):
    b, s, w = q.shape
    nb = ATTN_BATCHES
    gw = GROUP_HEADS * HEAD_DIM
    chains = nb * (w // gw)
    rows = GROUP_HEADS * TQ
    return pl.pallas_call(
        _attn_kernel,
        grid=(b // nb, s // TQ),
        in_specs=[
            pl.BlockSpec((nb, TQ, w), lambda bi, i: (bi, i, 0)),
            pl.BlockSpec((nb, s, w), lambda bi, i: (bi, 0, 0)),
            pl.BlockSpec((nb, s, w), lambda bi, i: (bi, 0, 0)),
            pl.BlockSpec((TK, TK), lambda bi, i: (0, 0)),
            pl.BlockSpec((1, w), lambda bi, i: (0, 0)),
            pl.BlockSpec((w, w), lambda bi, i: (0, 0)),
        ],
        out_specs=pl.BlockSpec((nb, TQ, w), lambda bi, i: (bi, i, 0)),
        out_shape=jax.ShapeDtypeStruct((b, s, w), BF16),
        scratch_shapes=[pltpu.VMEM((chains, rows, gw), BF16),
                        pltpu.VMEM((chains, rows, TK), F32),
                        pltpu.VMEM((chains, rows, TK), BF16),
                        pltpu.VMEM((chains, rows, 2 * HEAD_DIM), F32),
                        pltpu.VMEM((chains, rows, 1), F32)],
        compiler_params=pltpu.CompilerParams(
            dimension_semantics=("arbitrary", "arbitrary"),
            vmem_limit_bytes=VMEM_LIMIT_BYTES),
        name="sb_attention",
    )(q, k, v, tri, g_attn_out, gmat)


def _post_kernel(x_ref, conv_ref, attn_ref, p_ref, wo_ref, gmlp_ref, wup_ref, wdn_ref,
                 gple_ref, wg_ref, wp_ref, gfin_ref, o_ref, *, apply_final):
    w_conv = conv_ref.shape[1]
    subs = [slice(r0, r0 + POST_SUB_ROWS) for r0 in range(0, x_ref.shape[0], POST_SUB_ROWS)]
    dot = functools.partial(jnp.dot, preferred_element_type=F32)
    h = [x_ref[rs, :] + dot(conv_ref[rs, :], wo_ref[0:w_conv, :]) for rs in subs]
    h = [hs + dot(attn_ref[rs, :], wo_ref[w_conv:, :]) for hs, rs in zip(h, subs)]
    m = [(hs * _rms_scale(hs) * gmlp_ref[...]).astype(BF16) for hs in h]
    r = [jnp.square(jnp.maximum(dot(ms, wup_ref[...]), 0.0)).astype(BF16) for ms in m]
    h = [hs + dot(rr, wdn_ref[...]) for hs, rr in zip(h, r)]
    n = [(hs * _rms_scale(hs) * gple_ref[...]).astype(BF16) for hs in h]
    pp = [dot(p_ref[rs, :].astype(BF16), wp_ref[...]) for rs in subs]
    gate = [jax.nn.sigmoid(dot(ns, wg_ref[...])) for ns in n]
    h = [hs + gs * ps for hs, gs, ps in zip(h, gate, pp)]
    for hs, rs in zip(h, subs):
        if apply_final:
            hs = hs * _rms_scale(hs) * gfin_ref[...]
        o_ref[rs, :] = hs


def _post(x2, conv2, attn2, p2, w_out, g_mlp, w_up, w_down, g_ple, w_gate, w_proj, g_final,
          apply_final):
    t, d = x2.shape
    tm = TM_POST
    tile = lambda i: (i, 0)
    const = lambda i: (0, 0)
    wspec = lambda w: pl.BlockSpec(w.shape, const, pipeline_mode=pl.Buffered(1))
    gspec = pl.BlockSpec((1, d), const)
    return pl.pallas_call(
        functools.partial(_post_kernel, apply_final=apply_final),
        grid=(t // tm,),
        in_specs=[
            pl.BlockSpec((tm, d), tile),
            pl.BlockSpec((tm, conv2.shape[1]), tile),
            pl.BlockSpec((tm, attn2.shape[1]), tile),
            pl.BlockSpec((tm, p2.shape[1]), tile),
            wspec(w_out), gspec, wspec(w_up), wspec(w_down), gspec, wspec(w_gate), wspec(w_proj),
            gspec,
        ],
        out_specs=pl.BlockSpec((tm, d), tile),
        out_shape=jax.ShapeDtypeStruct((t, d), F32),
        compiler_params=pltpu.CompilerParams(
            dimension_semantics=("arbitrary",),
            vmem_limit_bytes=VMEM_LIMIT_BYTES),
        name="post",
    )(x2, conv2, attn2, p2, w_out, g_mlp, w_up, w_down, g_ple, w_gate, w_proj, g_final)


def _group_matrix(width):
    g = jnp.arange(width) // HEAD_DIM
    return jnp.where(g[:, None] == g[None, :], 1.0 / HEAD_DIM, 0.0).astype(BF16)


def kernel(x, p, g_mix, w_in, conv_w, g_conv_out, g_attn_out, w_out, g_mlp, w_up, w_down,
           g_ple, w_ple_gate, w_ple_proj, g_final):
    b, s, d = x.shape
    depth = p.shape[0]
    w_conv = conv_w.shape[-1]
    w_attn = g_attn_out.shape[-1]
    assert s % TM_IN == 0 and s % TQ == 0 and TQ == TK and (b * s) % TM_POST == 0
    assert w_attn % (GROUP_HEADS * HEAD_DIM) == 0 and w_conv % HEAD_DIM == 0
    assert b % ATTN_BATCHES == 0

    idx = jnp.arange(TK)
    tri = (idx[:, None] > idx[None, :]).astype(BF16)
    gmat_conv = _group_matrix(w_conv)
    gmat_attn = _group_matrix(w_attn)
    row = lambda g: g.reshape(1, -1)

    h = x
    for i in range(depth):
        conv, q, k, v = _mixer_in(h, row(g_mix[i]), w_in[i].astype(BF16), conv_w[i],
                                  row(g_conv_out[i]), gmat_conv, w_conv, w_attn)
        attn = _attention(q, k, v, tri, row(g_attn_out[i]), gmat_attn)
        h = _post(h.reshape(b * s, d), conv.reshape(b * s, w_conv), attn.reshape(b * s, w_attn),
                  p[i].reshape(b * s, -1), w_out[i].astype(BF16), row(g_mlp[i]),
                  w_up[i].astype(BF16), w_down[i].astype(BF16), row(g_ple[i]),
                  w_ple_gate[i].astype(BF16), w_ple_proj[i].astype(BF16), row(g_final),
                  apply_final=(i == depth - 1)).reshape(b, s, d)
    return h
```

```python
import functools

import jax
import jax.numpy as jnp
from jax import lax
from jax.experimental import pallas as pl
from jax.experimental.pallas import tpu as pltpu

HEAD_DIM = 64
CONV_K = 3
EPS = 1e-6
LOG2_E = 1.4426950408889634
UNDERFLOW_BITS = 150.0

F32 = jnp.float32
BF16 = jnp.bfloat16

V7X_VMEM_BYTES = 64 * 1024 * 1024
VMEM_LIMIT_BYTES = V7X_VMEM_BYTES - 8 * 1024 * 1024
SUBLANES = 8
BF16_SUBLANES = 16

TM_IN = 1024
IN_SUB_ROWS = 256
TQ = 256
TK = 256
GROUP_HEADS = 4
ATTN_BATCHES = 2
TM_POST = 512
POST_SUB_ROWS = 256


def _rms_scale(x):
    return lax.rsqrt(jnp.mean(x * x, axis=-1, keepdims=True) + EPS)


def _group_mean(sq, gmat):
    hi = sq.astype(BF16)
    lo = (sq - hi.astype(F32)).astype(BF16)
    return (jnp.dot(hi, gmat, preferred_element_type=F32)
            + jnp.dot(lo, gmat, preferred_element_type=F32))


def _mixer_in_kernel(x_ref, g_ref, w32_ref, cw_ref, gco_ref, gmat_ref, *rest, w_conv, w_attn,
                     n_side):
    side_in, rest = rest[:n_side], rest[n_side:]
    conv_ref, q_ref, k_ref, v_ref = rest[:4]
    side_out = rest[4:4 + n_side]
    vbuf, w_ref = rest[4 + n_side:]
    tm = x_ref.shape[1]

    @pl.when((pl.program_id(0) == 0) & (pl.program_id(1) == 0))
    def _():
        for r0 in range(0, w32_ref.shape[0], IN_SUB_ROWS):
            w_ref[r0:r0 + IN_SUB_ROWS, :] = w32_ref[r0:r0 + IN_SUB_ROWS, :].astype(BF16)

    for src, dst in zip(side_in, side_out):
        dst[...] = src[...].astype(BF16)

    @pl.when(pl.program_id(1) == 0)
    def _():
        vbuf[0:SUBLANES, :] = jnp.zeros((SUBLANES, w_conv), F32)

    starts = list(range(0, tm, IN_SUB_ROWS))
    a = []
    for r0 in starts:
        x = x_ref[0, r0:r0 + IN_SUB_ROWS, :]
        a.append((x * _rms_scale(x) * g_ref[...]).astype(BF16))
    proj = [jnp.dot(as_, w_ref[...], preferred_element_type=F32) for as_ in a]

    o = 3 * w_conv
    for r0, pr in zip(starts, proj):
        rs = slice(r0, r0 + IN_SUB_ROWS)
        vbuf[SUBLANES + r0:SUBLANES + r0 + IN_SUB_ROWS, :] = (
            pr[:, w_conv:2 * w_conv] * pr[:, 2 * w_conv:3 * w_conv])
        q_ref[0, rs, :] = (pr[:, o:o + w_attn] * (HEAD_DIM ** -0.5 * LOG2_E)).astype(BF16)
        k_ref[0, rs, :] = pr[:, o + w_attn:o + 2 * w_attn].astype(BF16)
        v_ref[0, rs, :] = pr[:, o + 2 * w_attn:o + 3 * w_attn].astype(BF16)

    cw = cw_ref[...]
    for r0, pr in zip(starts, proj):
        v0 = vbuf[SUBLANES + r0:SUBLANES + r0 + IN_SUB_ROWS, :]
        v1 = vbuf[SUBLANES - 1 + r0:SUBLANES - 1 + r0 + IN_SUB_ROWS, :]
        v2 = vbuf[SUBLANES - 2 + r0:SUBLANES - 2 + r0 + IN_SUB_ROWS, :]
        c = pr[:, 0:w_conv] * (cw[0:1, :] * v2 + cw[1:2, :] * v1 + cw[2:3, :] * v0)
        ms = _group_mean(c * c, gmat_ref[...])
        conv_ref[0, r0:r0 + IN_SUB_ROWS, :] = (
            c * lax.rsqrt(ms + EPS) * gco_ref[...]).astype(BF16)
    vbuf[0:SUBLANES, :] = vbuf[tm:tm + SUBLANES, :]


def _mixer_in(x, g_mix, w_in, conv_w, g_conv_out, gmat, w_conv, w_attn, side_weights):
    b, s, d = x.shape
    tm = TM_IN
    n_i = s // tm
    steps = b * n_i
    const = lambda *_: (0, 0)
    tile = lambda bi, i: (bi, i, 0)
    out_sd = lambda w: jax.ShapeDtypeStruct((b, s, w), BF16)
    side_rows = [w.shape[0] // steps for w in side_weights]
    assert all(w.shape[0] % steps == 0 and r % BF16_SUBLANES == 0
               for w, r in zip(side_weights, side_rows))
    side_specs = [pl.BlockSpec((r, w.shape[1]), lambda bi, i: (bi * n_i + i, 0))
                  for w, r in zip(side_weights, side_rows)]
    outs = pl.pallas_call(
        functools.partial(_mixer_in_kernel, w_conv=w_conv, w_attn=w_attn,
                          n_side=len(side_weights)),
        grid=(b, n_i),
        in_specs=[
            pl.BlockSpec((1, tm, d), tile),
            pl.BlockSpec((1, d), const),
            pl.BlockSpec(w_in.shape, const, pipeline_mode=pl.Buffered(1)),
            pl.BlockSpec((CONV_K, w_conv), const),
            pl.BlockSpec((1, w_conv), const),
            pl.BlockSpec((w_conv, w_conv), const),
        ] + side_specs,
        out_specs=[
            pl.BlockSpec((1, tm, w_conv), tile),
            pl.BlockSpec((1, tm, w_attn), tile),
            pl.BlockSpec((1, tm, w_attn), tile),
            pl.BlockSpec((1, tm, w_attn), tile),
        ] + side_specs,
        out_shape=[out_sd(w_conv), out_sd(w_attn), out_sd(w_attn), out_sd(w_attn)]
        + [jax.ShapeDtypeStruct(w.shape, BF16) for w in side_weights],
        scratch_shapes=[pltpu.VMEM((tm + SUBLANES, w_conv), F32),
                        pltpu.VMEM(w_in.shape, BF16)],
        compiler_params=pltpu.CompilerParams(
            dimension_semantics=("arbitrary", "arbitrary"),
            vmem_limit_bytes=VMEM_LIMIT_BYTES),
        name="mixer_in",
    )(x, g_mix, w_in, conv_w, g_conv_out, gmat, *side_weights)
    return outs[:4], outs[4:]


def _attn_kernel(q_ref, k_ref, v_ref, tri_ref, g_ref, gmat_ref, o_ref,
                 qst_ref, z_ref, ab_ref, acc_ref, carry_ref):
    nb, tq = q_ref.shape[0], q_ref.shape[1]
    tk = tri_ref.shape[0]
    chains, rows, gw = qst_ref.shape
    groups = chains // nb
    pair = 2 * HEAD_DIM
    i = pl.program_id(1)
    where = [(c // groups, (c % groups) * gw) for c in range(chains)]

    lane = lax.broadcasted_iota(jnp.int32, (tq, gw), 1)
    for c, (bl, l0) in enumerate(where):
        q = q_ref[bl, :, l0:l0 + gw]
        for hh in range(GROUP_HEADS):
            in_head = (lane >= hh * HEAD_DIM) & (lane < (hh + 1) * HEAD_DIM)
            qst_ref[c, hh * tq:(hh + 1) * tq, :] = jnp.where(in_head, q, jnp.zeros_like(q))

    def scores(c, j):
        bl, l0 = where[c]
        start = pl.multiple_of(j * tk, tk)
        kj = k_ref[bl, pl.ds(start, tk), l0:l0 + gw]
        z_ref[c] = lax.dot_general(qst_ref[c], kj, (((1,), (1,)), ((), ())),
                                   preferred_element_type=F32)

    def weights(c, diag):
        z = z_ref[c]
        sp = jnp.maximum(z, 0.0) + jnp.log(1.0 + jnp.exp2(-jnp.abs(z))) * LOG2_E
        if diag:
            row = lax.broadcasted_iota(jnp.int32, (rows, tk), 0) & (tq - 1)
            col = lax.broadcasted_iota(jnp.int32, (rows, tk), 1)
            causal = col < row
            sp = jnp.where(causal, sp, 0.0)
        cs = jnp.dot(sp.astype(BF16), tri_ref[...], preferred_element_type=F32)
        total = cs[:, 0:1] + sp[:, 0:1]
        if diag:
            a = jnp.where(causal, jnp.exp2(z - (sp + cs)), 0.0)
            new_carry = total
        else:
            carry = carry_ref[c]
            a = jnp.exp2(z - (sp + cs + carry))
            new_carry = carry + total
        carry_ref[c] = new_carry
        ab_ref[c] = a.astype(BF16)
        return jnp.min(new_carry)

    def accumulate(c, j):
        bl, l0 = where[c]
        start = pl.multiple_of(j * tk, tk)
        for p in range(GROUP_HEADS // 2):
            r0, r1 = 2 * p * tq, 2 * (p + 1) * tq
            vj = v_ref[bl, pl.ds(start, tk), l0 + p * pair:l0 + (p + 1) * pair]
            acc_ref[c, r0:r1, :] += jnp.dot(ab_ref[c, r0:r1, :], vj, preferred_element_type=F32)

    last = chains - 1

    def body(state):
        n, _ = state
        j = i - 1 - n
        accumulate(last, j + 1)
        low = []
        for c in range(chains):
            low.append(weights(c, False))
            if c != last:
                accumulate(c, j)
            scores(c, jnp.maximum(j - 1, 0))
        return n + 1, functools.reduce(jnp.minimum, low)

    def more(state):
        n, low = state
        return (n < i) & (low < UNDERFLOW_BITS)

    acc_ref[...] = jnp.zeros(acc_ref.shape, F32)
    for c in range(chains):
        scores(c, i)
    low = []
    for c in range(chains):
        low.append(weights(c, True))
        if c != last:
            accumulate(c, i)
        scores(c, jnp.maximum(i - 1, 0))
    n, _ = lax.while_loop(more, body, (jnp.int32(0), functools.reduce(jnp.minimum, low)))
    accumulate(last, i - n)

    lane_p = lax.broadcasted_iota(jnp.int32, (tq, pair), 1)
    for bl in range(nb):
        out = jnp.concatenate(
            [jnp.where(lane_p < HEAD_DIM, acc_ref[c, 2 * p * tq:(2 * p + 1) * tq, :],
                       acc_ref[c, (2 * p + 1) * tq:(2 * p + 2) * tq, :])
             for c in range(bl * groups, (bl + 1) * groups) for p in range(GROUP_HEADS // 2)],
            axis=1)
        ms = _group_mean(out * out, gmat_ref[...])
        o_ref[bl] = (out * lax.rsqrt(ms + EPS) * g_ref[...]).astype(BF16)


def _attention(q, k, v, tri, g_attn_out, gmat):
    b, s, w = q.shape
    nb = ATTN_BATCHES
    gw = GROUP_HEADS * HEAD_DIM
    chains = nb * (w // gw)
    rows = GROUP_HEADS * TQ
    return pl.pallas_call(
        _attn_kernel,
        grid=(b // nb, s // TQ),
        in_specs=[
            pl.BlockSpec((nb, TQ, w), lambda bi, i: (bi, i, 0)),
            pl.BlockSpec((nb, s, w), lambda bi, i: (bi, 0, 0)),
            pl.BlockSpec((nb, s, w), lambda bi, i: (bi, 0, 0)),
            pl.BlockSpec((TK, TK), lambda bi, i: (0, 0)),
            pl.BlockSpec((1, w), lambda bi, i: (0, 0)),
            pl.BlockSpec((w, w), lambda bi, i: (0, 0)),
        ],
        out_specs=pl.BlockSpec((nb, TQ, w), lambda bi, i: (bi, i, 0)),
        out_shape=jax.ShapeDtypeStruct((b, s, w), BF16),
        scratch_shapes=[pltpu.VMEM((chains, rows, gw), BF16),
                        pltpu.VMEM((chains, rows, TK), F32),
                        pltpu.VMEM((chains, rows, TK), BF16),
                        pltpu.VMEM((chains, rows, 2 * HEAD_DIM), F32),
                        pltpu.VMEM((chains, rows, 1), F32)],
        compiler_params=pltpu.CompilerParams(
            dimension_semantics=("arbitrary", "arbitrary"),
            vmem_limit_bytes=VMEM_LIMIT_BYTES),
        name="sb_attention",
    )(q, k, v, tri, g_attn_out, gmat)


def _post_kernel(x_ref, conv_ref, attn_ref, p_ref, wo_ref, gmlp_ref, wup_ref, wdn_ref,
                 gple_ref, wg_ref, wp_ref, gfin_ref, o_ref, *, apply_final):
    w_conv = conv_ref.shape[1]
    subs = [slice(r0, r0 + POST_SUB_ROWS) for r0 in range(0, x_ref.shape[0], POST_SUB_ROWS)]
    dot = functools.partial(jnp.dot, preferred_element_type=F32)
    h = [x_ref[rs, :] + dot(conv_ref[rs, :], wo_ref[0:w_conv, :]) for rs in subs]
    h = [hs + dot(attn_ref[rs, :], wo_ref[w_conv:, :]) for hs, rs in zip(h, subs)]
    m = [(hs * _rms_scale(hs) * gmlp_ref[...]).astype(BF16) for hs in h]
    r = [jnp.square(jnp.maximum(dot(ms, wup_ref[...]), 0.0)).astype(BF16) for ms in m]
    h = [hs + dot(rr, wdn_ref[...]) for hs, rr in zip(h, r)]
    n = [(hs * _rms_scale(hs) * gple_ref[...]).astype(BF16) for hs in h]
    pp = [dot(p_ref[rs, :].astype(BF16), wp_ref[...]) for rs in subs]
    gate = [jax.nn.sigmoid(dot(ns, wg_ref[...])) for ns in n]
    h = [hs + gs * ps for hs, gs, ps in zip(h, gate, pp)]
    for hs, rs in zip(h, subs):
        if apply_final:
            hs = hs * _rms_scale(hs) * gfin_ref[...]
        o_ref[rs, :] = hs


def _post(x2, conv2, attn2, p2, w_out, g_mlp, w_up, w_down, g_ple, w_gate, w_proj, g_final,
          apply_final):
    t, d = x2.shape
    tm = TM_POST
    tile = lambda i: (i, 0)
    const = lambda i: (0, 0)
    wspec = lambda w: pl.BlockSpec(w.shape, const, pipeline_mode=pl.Buffered(1))
    gspec = pl.BlockSpec((1, d), const)
    return pl.pallas_call(
        functools.partial(_post_kernel, apply_final=apply_final),
        grid=(t // tm,),
        in_specs=[
            pl.BlockSpec((tm, d), tile),
            pl.BlockSpec((tm, conv2.shape[1]), tile),
            pl.BlockSpec((tm, attn2.shape[1]), tile),
            pl.BlockSpec((tm, p2.shape[1]), tile),
            wspec(w_out), gspec, wspec(w_up), wspec(w_down), gspec, wspec(w_gate), wspec(w_proj),
            gspec,
        ],
        out_specs=pl.BlockSpec((tm, d), tile),
        out_shape=jax.ShapeDtypeStruct((t, d), F32),
        compiler_params=pltpu.CompilerParams(
            dimension_semantics=("arbitrary",),
            vmem_limit_bytes=VMEM_LIMIT_BYTES),
        name="post",
    )(x2, conv2, attn2, p2, w_out, g_mlp, w_up, w_down, g_ple, w_gate, w_proj, g_final)


def _group_matrix(width):
    g = jnp.arange(width) // HEAD_DIM
    return jnp.where(g[:, None] == g[None, :], 1.0 / HEAD_DIM, 0.0).astype(BF16)


def kernel(x, p, g_mix, w_in, conv_w, g_conv_out, g_attn_out, w_out, g_mlp, w_up, w_down,
           g_ple, w_ple_gate, w_ple_proj, g_final):
    b, s, d = x.shape
    depth = p.shape[0]
    w_conv = conv_w.shape[-1]
    w_attn = g_attn_out.shape[-1]
    assert s % TM_IN == 0 and s % TQ == 0 and TQ == TK and (b * s) % TM_POST == 0
    assert w_attn % (GROUP_HEADS * HEAD_DIM) == 0 and w_conv % HEAD_DIM == 0
    assert b % ATTN_BATCHES == 0

    idx = jnp.arange(TK)
    tri = (idx[:, None] > idx[None, :]).astype(BF16)
    gmat_conv = _group_matrix(w_conv)
    gmat_attn = _group_matrix(w_attn)
    row = lambda g: g.reshape(1, -1)

    h = x
    for i in range(depth):
        (conv, q, k, v), (wo, wu, wd, wg, wp) = _mixer_in(
            h, row(g_mix[i]), w_in[i], conv_w[i], row(g_conv_out[i]), gmat_conv, w_conv, w_attn,
            [w_out[i], w_up[i], w_down[i], w_ple_gate[i], w_ple_proj[i]])
        attn = _attention(q, k, v, tri, row(g_attn_out[i]), gmat_attn)
        h = _post(h.reshape(b * s, d), conv.reshape(b * s, w_conv), attn.reshape(b * s, w_attn),
                  p[i].reshape(b * s, -1), wo, row(g_mlp[i]), wu, wd, row(g_ple[i]), wg, wp,
                  row(g_final), apply_final=(i == depth - 1)).reshape(b, s, d)
    return h
```

```python
import functools

import jax
import jax.numpy as jnp
from jax import lax
from jax.experimental import pallas as pl
from jax.experimental.pallas import tpu as pltpu

HEAD_DIM = 64
CONV_K = 3
EPS = 1e-6
LOG2_E = 1.4426950408889634
UNDERFLOW_BITS = 150.0
MASKED_SCORE = -1e30

F32 = jnp.float32
BF16 = jnp.bfloat16

V7X_VMEM_BYTES = 64 * 1024 * 1024
VMEM_LIMIT_BYTES = V7X_VMEM_BYTES - 8 * 1024 * 1024
SUBLANES = 8
BF16_SUBLANES = 16

TM_IN = 1024
IN_SUB_ROWS = 256
TQ = 256
TK = 256
GROUP_HEADS = 4
ATTN_BATCHES = 2
TM_POST = 512
POST_SUB_ROWS = 256


def _rms_scale(x):
    return lax.rsqrt(jnp.mean(x * x, axis=-1, keepdims=True) + EPS)


def _group_mean(sq, gmat):
    hi = sq.astype(BF16)
    lo = (sq - hi.astype(F32)).astype(BF16)
    return (jnp.dot(hi, gmat, preferred_element_type=F32)
            + jnp.dot(lo, gmat, preferred_element_type=F32))


def _mixer_in_kernel(x_ref, g_ref, w32_ref, cw_ref, gco_ref, gmat_ref, *rest, w_conv, w_attn,
                     n_side):
    side_in, rest = rest[:n_side], rest[n_side:]
    conv_ref, q_ref, k_ref, v_ref = rest[:4]
    side_out = rest[4:4 + n_side]
    vbuf, w_ref = rest[4 + n_side:]
    tm = x_ref.shape[1]

    @pl.when((pl.program_id(0) == 0) & (pl.program_id(1) == 0))
    def _():
        for r0 in range(0, w32_ref.shape[0], IN_SUB_ROWS):
            w_ref[r0:r0 + IN_SUB_ROWS, :] = w32_ref[r0:r0 + IN_SUB_ROWS, :].astype(BF16)

    for src, dst in zip(side_in, side_out):
        dst[...] = src[...].astype(BF16)

    @pl.when(pl.program_id(1) == 0)
    def _():
        vbuf[0:SUBLANES, :] = jnp.zeros((SUBLANES, w_conv), F32)

    starts = list(range(0, tm, IN_SUB_ROWS))
    a = []
    for r0 in starts:
        x = x_ref[0, r0:r0 + IN_SUB_ROWS, :]
        a.append((x * _rms_scale(x) * g_ref[...]).astype(BF16))
    proj = [jnp.dot(as_, w_ref[...], preferred_element_type=F32) for as_ in a]

    o = 3 * w_conv
    for r0, pr in zip(starts, proj):
        rs = slice(r0, r0 + IN_SUB_ROWS)
        vbuf[SUBLANES + r0:SUBLANES + r0 + IN_SUB_ROWS, :] = (
            pr[:, w_conv:2 * w_conv] * pr[:, 2 * w_conv:3 * w_conv])
        q_ref[0, rs, :] = (pr[:, o:o + w_attn] * (HEAD_DIM ** -0.5 * LOG2_E)).astype(BF16)
        k_ref[0, rs, :] = pr[:, o + w_attn:o + 2 * w_attn].astype(BF16)
        v_ref[0, rs, :] = pr[:, o + 2 * w_attn:o + 3 * w_attn].astype(BF16)

    cw = cw_ref[...]
    for r0, pr in zip(starts, proj):
        v0 = vbuf[SUBLANES + r0:SUBLANES + r0 + IN_SUB_ROWS, :]
        v1 = vbuf[SUBLANES - 1 + r0:SUBLANES - 1 + r0 + IN_SUB_ROWS, :]
        v2 = vbuf[SUBLANES - 2 + r0:SUBLANES - 2 + r0 + IN_SUB_ROWS, :]
        c = pr[:, 0:w_conv] * (cw[0:1, :] * v2 + cw[1:2, :] * v1 + cw[2:3, :] * v0)
        ms = _group_mean(c * c, gmat_ref[...])
        conv_ref[0, r0:r0 + IN_SUB_ROWS, :] = (
            c * lax.rsqrt(ms + EPS) * gco_ref[...]).astype(BF16)
    vbuf[0:SUBLANES, :] = vbuf[tm:tm + SUBLANES, :]


def _mixer_in(x, g_mix, w_in, conv_w, g_conv_out, gmat, w_conv, w_attn, side_weights):
    b, s, d = x.shape
    tm = TM_IN
    n_i = s // tm
    steps = b * n_i
    const = lambda *_: (0, 0)
    tile = lambda bi, i: (bi, i, 0)
    out_sd = lambda w: jax.ShapeDtypeStruct((b, s, w), BF16)
    side_rows = [w.shape[0] // steps for w in side_weights]
    assert all(w.shape[0] % steps == 0 and r % BF16_SUBLANES == 0
               for w, r in zip(side_weights, side_rows))
    side_specs = [pl.BlockSpec((r, w.shape[1]), lambda bi, i: (bi * n_i + i, 0))
                  for w, r in zip(side_weights, side_rows)]
    outs = pl.pallas_call(
        functools.partial(_mixer_in_kernel, w_conv=w_conv, w_attn=w_attn,
                          n_side=len(side_weights)),
        grid=(b, n_i),
        in_specs=[
            pl.BlockSpec((1, tm, d), tile),
            pl.BlockSpec((1, d), const),
            pl.BlockSpec(w_in.shape, const, pipeline_mode=pl.Buffered(1)),
            pl.BlockSpec((CONV_K, w_conv), const),
            pl.BlockSpec((1, w_conv), const),
            pl.BlockSpec((w_conv, w_conv), const),
        ] + side_specs,
        out_specs=[
            pl.BlockSpec((1, tm, w_conv), tile),
            pl.BlockSpec((1, tm, w_attn), tile),
            pl.BlockSpec((1, tm, w_attn), tile),
            pl.BlockSpec((1, tm, w_attn), tile),
        ] + side_specs,
        out_shape=[out_sd(w_conv), out_sd(w_attn), out_sd(w_attn), out_sd(w_attn)]
        + [jax.ShapeDtypeStruct(w.shape, BF16) for w in side_weights],
        scratch_shapes=[pltpu.VMEM((tm + SUBLANES, w_conv), F32),
                        pltpu.VMEM(w_in.shape, BF16)],
        compiler_params=pltpu.CompilerParams(
            dimension_semantics=("arbitrary", "arbitrary"),
            vmem_limit_bytes=VMEM_LIMIT_BYTES),
        name="mixer_in",
    )(x, g_mix, w_in, conv_w, g_conv_out, gmat, *side_weights)
    return outs[:4], outs[4:]


def _attn_kernel(q_ref, k_ref, v_ref, tri_ref, bias_ref, g_ref, gmat_ref, o_ref,
                 qst_ref, acc_ref, carry_ref):
    nb, tq = q_ref.shape[0], q_ref.shape[1]
    tk = tri_ref.shape[0]
    chains, rows, gw = qst_ref.shape
    groups = chains // nb
    pair = 2 * HEAD_DIM
    i = pl.program_id(1)
    where = [(c // groups, (c % groups) * gw) for c in range(chains)]

    lane = lax.broadcasted_iota(jnp.int32, (tq, gw), 1)
    for c, (bl, l0) in enumerate(where):
        q = q_ref[bl, :, l0:l0 + gw]
        for hh in range(GROUP_HEADS):
            in_head = (lane >= hh * HEAD_DIM) & (lane < (hh + 1) * HEAD_DIM)
            qst_ref[c, hh * tq:(hh + 1) * tq, :] = jnp.where(in_head, q, jnp.zeros_like(q))

    def scores(c, j):
        bl, l0 = where[c]
        start = pl.multiple_of(j * tk, tk)
        kj = k_ref[bl, pl.ds(start, tk), l0:l0 + gw]
        return lax.dot_general(qst_ref[c], kj, (((1,), (1,)), ((), ())),
                               preferred_element_type=F32)

    def weights(z, carry):
        sp = jnp.maximum(z, 0.0) + jnp.log(1.0 + jnp.exp2(-jnp.abs(z))) * LOG2_E
        cs = jnp.dot(sp.astype(BF16), tri_ref[...], preferred_element_type=F32)
        total = cs[:, 0:1] + sp[:, 0:1]
        if carry is None:
            return jnp.exp2(z - (sp + cs)).astype(BF16), total
        return jnp.exp2(z - (sp + cs + carry)).astype(BF16), carry + total

    def accumulate(c, ab, j, assign=False):
        bl, l0 = where[c]
        start = pl.multiple_of(j * tk, tk)
        for p in range(GROUP_HEADS // 2):
            r0, r1 = 2 * p * tq, 2 * (p + 1) * tq
            vj = v_ref[bl, pl.ds(start, tk), l0 + p * pair:l0 + (p + 1) * pair]
            av = jnp.dot(ab[r0:r1, :], vj, preferred_element_type=F32)
            if assign:
                acc_ref[c, r0:r1, :] = av
            else:
                acc_ref[c, r0:r1, :] += av

    prev = jnp.maximum(i - 1, 0)
    no_prev = jnp.where(i == 0, -MASKED_SCORE, 0.0).astype(F32)
    zd = [(scores(c, i).reshape(GROUP_HEADS, tq, tk) + bias_ref[...][None]).reshape(rows, tk)
          for c in range(chains)]
    zp = [scores(c, prev) for c in range(chains)]
    carry = []
    for c in range(chains):
        ab, cr = weights(zd[c], None)
        accumulate(c, ab, i, assign=True)
        carry.append(cr + no_prev)
    low = []
    for c in range(chains):
        ab, cr = weights(zp[c], carry[c])
        accumulate(c, ab, prev)
        carry_ref[c] = cr
        low.append(jnp.min(cr))

    def body(state):
        n, _ = state
        j = i - 1 - n
        low = []
        for c in range(chains):
            ab, cr = weights(scores(c, j), carry_ref[c])
            accumulate(c, ab, j)
            carry_ref[c] = cr
            low.append(jnp.min(cr))
        return n + 1, functools.reduce(jnp.minimum, low)

    def more(state):
        n, low = state
        return (n < i) & (low < UNDERFLOW_BITS)

    lax.while_loop(more, body, (jnp.int32(1), functools.reduce(jnp.minimum, low)))

    lane_p = lax.broadcasted_iota(jnp.int32, (tq, pair), 1)
    for bl in range(nb):
        out = jnp.concatenate(
            [jnp.where(lane_p < HEAD_DIM, acc_ref[c, 2 * p * tq:(2 * p + 1) * tq, :],
                       acc_ref[c, (2 * p + 1) * tq:(2 * p + 2) * tq, :])
             for c in range(bl * groups, (bl + 1) * groups) for p in range(GROUP_HEADS // 2)],
            axis=1)
        ms = _group_mean(out * out, gmat_ref[...])
        o_ref[bl] = (out * lax.rsqrt(ms + EPS) * g_ref[...]).astype(BF16)


def _attention(q, k, v, tri, bias, g_attn_out, gmat):
    b, s, w = q.shape
    nb = ATTN_BATCHES
    gw = GROUP_HEADS * HEAD_DIM
    chains = nb * (w // gw)
    rows = GROUP_HEADS * TQ
    return pl.pallas_call(
        _attn_kernel,
        grid=(b // nb, s // TQ),
        in_specs=[
            pl.BlockSpec((nb, TQ, w), lambda bi, i: (bi, i, 0)),
            pl.BlockSpec((nb, s, w), lambda bi, i: (bi, 0, 0), pipeline_mode=pl.Buffered(1)),
            pl.BlockSpec((nb, s, w), lambda bi, i: (bi, 0, 0), pipeline_mode=pl.Buffered(1)),
            pl.BlockSpec((TK, TK), lambda bi, i: (0, 0)),
            pl.BlockSpec((TQ, TK), lambda bi, i: (0, 0)),
            pl.BlockSpec((1, w), lambda bi, i: (0, 0)),
            pl.BlockSpec((w, w), lambda bi, i: (0, 0)),
        ],
        out_specs=pl.BlockSpec((nb, TQ, w), lambda bi, i: (bi, i, 0)),
        out_shape=jax.ShapeDtypeStruct((b, s, w), BF16),
        scratch_shapes=[pltpu.VMEM((chains, rows, gw), BF16),
                        pltpu.VMEM((chains, rows, 2 * HEAD_DIM), F32),
                        pltpu.VMEM((chains, rows, 1), F32)],
        compiler_params=pltpu.CompilerParams(
            dimension_semantics=("arbitrary", "arbitrary"),
            vmem_limit_bytes=VMEM_LIMIT_BYTES),
        name="sb_attention",
    )(q, k, v, tri, bias, g_attn_out, gmat)


def _post_kernel(x_ref, conv_ref, attn_ref, p_ref, wo_ref, gmlp_ref, wup_ref, wdn_ref,
                 gple_ref, wg_ref, wp_ref, gfin_ref, o_ref, *, apply_final):
    w_conv = conv_ref.shape[1]
    subs = [slice(r0, r0 + POST_SUB_ROWS) for r0 in range(0, x_ref.shape[0], POST_SUB_ROWS)]
    dot = functools.partial(jnp.dot, preferred_element_type=F32)
    h = [x_ref[rs, :] + dot(conv_ref[rs, :], wo_ref[0:w_conv, :]) for rs in subs]
    h = [hs + dot(attn_ref[rs, :], wo_ref[w_conv:, :]) for hs, rs in zip(h, subs)]
    m = [(hs * _rms_scale(hs) * gmlp_ref[...]).astype(BF16) for hs in h]
    r = [jnp.square(jnp.maximum(dot(ms, wup_ref[...]), 0.0)).astype(BF16) for ms in m]
    h = [hs + dot(rr, wdn_ref[...]) for hs, rr in zip(h, r)]
    n = [(hs * _rms_scale(hs) * gple_ref[...]).astype(BF16) for hs in h]
    pp = [dot(p_ref[rs, :].astype(BF16), wp_ref[...]) for rs in subs]
    gate = [jax.nn.sigmoid(dot(ns, wg_ref[...])) for ns in n]
    h = [hs + gs * ps for hs, gs, ps in zip(h, gate, pp)]
    for hs, rs in zip(h, subs):
        if apply_final:
            hs = hs * _rms_scale(hs) * gfin_ref[...]
        o_ref[rs, :] = hs


def _post(x2, conv2, attn2, p2, w_out, g_mlp, w_up, w_down, g_ple, w_gate, w_proj, g_final,
          apply_final):
    t, d = x2.shape
    tm = TM_POST
    tile = lambda i: (i, 0)
    const = lambda i: (0, 0)
    wspec = lambda w: pl.BlockSpec(w.shape, const, pipeline_mode=pl.Buffered(1))
    gspec = pl.BlockSpec((1, d), const)
    return pl.pallas_call(
        functools.partial(_post_kernel, apply_final=apply_final),
        grid=(t // tm,),
        in_specs=[
            pl.BlockSpec((tm, d), tile),
            pl.BlockSpec((tm, conv2.shape[1]), tile),
            pl.BlockSpec((tm, attn2.shape[1]), tile),
            pl.BlockSpec((tm, p2.shape[1]), tile),
            wspec(w_out), gspec, wspec(w_up), wspec(w_down), gspec, wspec(w_gate), wspec(w_proj),
            gspec,
        ],
        out_specs=pl.BlockSpec((tm, d), tile),
        out_shape=jax.ShapeDtypeStruct((t, d), F32),
        compiler_params=pltpu.CompilerParams(
            dimension_semantics=("arbitrary",),
            vmem_limit_bytes=VMEM_LIMIT_BYTES),
        name="post",
    )(x2, conv2, attn2, p2, w_out, g_mlp, w_up, w_down, g_ple, w_gate, w_proj, g_final)


def _group_matrix(width):
    g = jnp.arange(width) // HEAD_DIM
    return jnp.where(g[:, None] == g[None, :], 1.0 / HEAD_DIM, 0.0).astype(BF16)


def kernel(x, p, g_mix, w_in, conv_w, g_conv_out, g_attn_out, w_out, g_mlp, w_up, w_down,
           g_ple, w_ple_gate, w_ple_proj, g_final):
    b, s, d = x.shape
    depth = p.shape[0]
    w_conv = conv_w.shape[-1]
    w_attn = g_attn_out.shape[-1]
    assert s % TM_IN == 0 and s % TQ == 0 and TQ == TK and (b * s) % TM_POST == 0
    assert w_attn % (GROUP_HEADS * HEAD_DIM) == 0 and w_conv % HEAD_DIM == 0
    assert b % ATTN_BATCHES == 0

    idx = jnp.arange(TK)
    tri = (idx[:, None] > idx[None, :]).astype(BF16)
    bias = jnp.where(idx[None, :] < idx[:, None], 0.0, MASKED_SCORE).astype(F32)
    gmat_conv = _group_matrix(w_conv)
    gmat_attn = _group_matrix(w_attn)
    row = lambda g: g.reshape(1, -1)

    h = x
    for i in range(depth):
        (conv, q, k, v), (wo, wu, wd, wg, wp) = _mixer_in(
            h, row(g_mix[i]), w_in[i], conv_w[i], row(g_conv_out[i]), gmat_conv, w_conv, w_attn,
            [w_out[i], w_up[i], w_down[i], w_ple_gate[i], w_ple_proj[i]])
        attn = _attention(q, k, v, tri, bias, row(g_attn_out[i]), gmat_attn)
        h = _post(h.reshape(b * s, d), conv.reshape(b * s, w_conv), attn.reshape(b * s, w_attn),
                  p[i].reshape(b * s, -1), wo, row(g_mlp[i]), wu, wd, row(g_ple[i]), wg, wp,
                  row(g_final), apply_final=(i == depth - 1)).reshape(b, s, d)
    return h
```

```python
import functools

import jax
import jax.numpy as jnp
from jax import lax
from jax.experimental import pallas as pl
from jax.experimental.pallas import tpu as pltpu

HEAD_DIM = 64
CONV_K = 3
EPS = 1e-6
LOG2_E = 1.4426950408889634
UNDERFLOW_BITS = 150.0
MASKED_SCORE = -1e30

F32 = jnp.float32
BF16 = jnp.bfloat16

V7X_VMEM_BYTES = 64 * 1024 * 1024
VMEM_LIMIT_BYTES = V7X_VMEM_BYTES - 8 * 1024 * 1024
SUBLANES = 8
BF16_SUBLANES = 16

TM_IN = 1024
IN_SUB_ROWS = 256
TQ = 256
TK = 256
GROUP_HEADS = 4
ATTN_BATCHES = 2
TM_POST = 512
POST_SUB_ROWS = 256


def _rms_scale(x):
    return lax.rsqrt(jnp.mean(x * x, axis=-1, keepdims=True) + EPS)


def _group_mean(sq, gmat):
    hi = sq.astype(BF16)
    lo = (sq - hi.astype(F32)).astype(BF16)
    return (jnp.dot(hi, gmat, preferred_element_type=F32)
            + jnp.dot(lo, gmat, preferred_element_type=F32))


def _mixer_in_kernel(x_ref, g_ref, w32_ref, cw_ref, gco_ref, gmat_ref, *rest, w_conv, w_attn,
                     n_side):
    side_in, rest = rest[:n_side], rest[n_side:]
    conv_ref, q_ref, k_ref, v_ref = rest[:4]
    side_out = rest[4:4 + n_side]
    vbuf, w_ref = rest[4 + n_side:]
    tm = x_ref.shape[1]

    @pl.when((pl.program_id(0) == 0) & (pl.program_id(1) == 0))
    def _():
        for r0 in range(0, w32_ref.shape[0], IN_SUB_ROWS):
            w_ref[r0:r0 + IN_SUB_ROWS, :] = w32_ref[r0:r0 + IN_SUB_ROWS, :].astype(BF16)

    for src, dst in zip(side_in, side_out):
        dst[...] = src[...].astype(BF16)

    @pl.when(pl.program_id(1) == 0)
    def _():
        vbuf[0:SUBLANES, :] = jnp.zeros((SUBLANES, w_conv), F32)

    starts = list(range(0, tm, IN_SUB_ROWS))
    a = []
    for r0 in starts:
        x = x_ref[0, r0:r0 + IN_SUB_ROWS, :]
        a.append((x * _rms_scale(x) * g_ref[...]).astype(BF16))
    proj = [jnp.dot(as_, w_ref[...], preferred_element_type=F32) for as_ in a]

    o = 3 * w_conv
    for r0, pr in zip(starts, proj):
        rs = slice(r0, r0 + IN_SUB_ROWS)
        vbuf[SUBLANES + r0:SUBLANES + r0 + IN_SUB_ROWS, :] = (
            pr[:, w_conv:2 * w_conv] * pr[:, 2 * w_conv:3 * w_conv])
        q_ref[0, rs, :] = (pr[:, o:o + w_attn] * (HEAD_DIM ** -0.5 * LOG2_E)).astype(BF16)
        k_ref[0, rs, :] = pr[:, o + w_attn:o + 2 * w_attn].astype(BF16)
        v_ref[0, rs, :] = pr[:, o + 2 * w_attn:o + 3 * w_attn].astype(BF16)

    cw = cw_ref[...]
    for r0, pr in zip(starts, proj):
        v0 = vbuf[SUBLANES + r0:SUBLANES + r0 + IN_SUB_ROWS, :]
        v1 = vbuf[SUBLANES - 1 + r0:SUBLANES - 1 + r0 + IN_SUB_ROWS, :]
        v2 = vbuf[SUBLANES - 2 + r0:SUBLANES - 2 + r0 + IN_SUB_ROWS, :]
        c = pr[:, 0:w_conv] * (cw[0:1, :] * v2 + cw[1:2, :] * v1 + cw[2:3, :] * v0)
        ms = _group_mean(c * c, gmat_ref[...])
        conv_ref[0, r0:r0 + IN_SUB_ROWS, :] = (
            c * lax.rsqrt(ms + EPS) * gco_ref[...]).astype(BF16)
    vbuf[0:SUBLANES, :] = vbuf[tm:tm + SUBLANES, :]


def _mixer_in(x, g_mix, w_in, conv_w, g_conv_out, gmat, w_conv, w_attn, side_weights):
    b, s, d = x.shape
    tm = TM_IN
    n_i = s // tm
    steps = b * n_i
    const = lambda *_: (0, 0)
    tile = lambda bi, i: (bi, i, 0)
    out_sd = lambda w: jax.ShapeDtypeStruct((b, s, w), BF16)
    side_rows = [w.shape[0] // steps for w in side_weights]
    assert all(w.shape[0] % steps == 0 and r % BF16_SUBLANES == 0
               for w, r in zip(side_weights, side_rows))
    side_specs = [pl.BlockSpec((r, w.shape[1]), lambda bi, i: (bi * n_i + i, 0))
                  for w, r in zip(side_weights, side_rows)]
    outs = pl.pallas_call(
        functools.partial(_mixer_in_kernel, w_conv=w_conv, w_attn=w_attn,
                          n_side=len(side_weights)),
        grid=(b, n_i),
        in_specs=[
            pl.BlockSpec((1, tm, d), tile),
            pl.BlockSpec((1, d), const),
            pl.BlockSpec(w_in.shape, const, pipeline_mode=pl.Buffered(1)),
            pl.BlockSpec((CONV_K, w_conv), const),
            pl.BlockSpec((1, w_conv), const),
            pl.BlockSpec((w_conv, w_conv), const),
        ] + side_specs,
        out_specs=[
            pl.BlockSpec((1, tm, w_conv), tile),
            pl.BlockSpec((1, tm, w_attn), tile),
            pl.BlockSpec((1, tm, w_attn), tile),
            pl.BlockSpec((1, tm, w_attn), tile),
        ] + side_specs,
        out_shape=[out_sd(w_conv), out_sd(w_attn), out_sd(w_attn), out_sd(w_attn)]
        + [jax.ShapeDtypeStruct(w.shape, BF16) for w in side_weights],
        scratch_shapes=[pltpu.VMEM((tm + SUBLANES, w_conv), F32),
                        pltpu.VMEM(w_in.shape, BF16)],
        compiler_params=pltpu.CompilerParams(
            dimension_semantics=("arbitrary", "arbitrary"),
            vmem_limit_bytes=VMEM_LIMIT_BYTES),
        name="mixer_in",
    )(x, g_mix, w_in, conv_w, g_conv_out, gmat, *side_weights)
    return outs[:4], outs[4:]


def _attn_kernel(q_ref, k_ref, v_ref, tri_ref, bias_a_ref, bias_bc_ref, g_ref, gmat_ref,
                 o_ref, qst_ref, acc_ref, carry_ref):
    nb, tq = q_ref.shape[0], q_ref.shape[1]
    tk = tri_ref.shape[0]
    chains, rows, gw = qst_ref.shape
    groups = chains // nb
    pair = 2 * HEAD_DIM
    i = pl.program_id(1)
    where = [(c // groups, (c % groups) * gw) for c in range(chains)]
    segs = []
    for q0, qn in ((0, tq // 2), (tq // 2, tq // 4), (3 * tq // 4, tq // 4)):
        segs.append((q0, qn, slice(GROUP_HEADS * q0, GROUP_HEADS * (q0 + qn))))
    (_, _, ra), (_, _, rb), (_, _, rc) = segs
    rab, rbc, full = slice(0, rb.stop), slice(rb.start, rows), slice(0, rows)

    for c, (bl, l0) in enumerate(where):
        for q0, qn, rs in segs:
            q = q_ref[bl, q0:q0 + qn, l0:l0 + gw]
            lane = lax.broadcasted_iota(jnp.int32, (qn, gw), 1)
            for hh in range(GROUP_HEADS):
                in_head = (lane >= hh * HEAD_DIM) & (lane < (hh + 1) * HEAD_DIM)
                r0 = rs.start + hh * qn
                qst_ref[c, r0:r0 + qn, :] = jnp.where(in_head, q, jnp.zeros_like(q))

    def scores(c, rs, start, nk):
        bl, l0 = where[c]
        kj = k_ref[bl, pl.ds(start, nk), l0:l0 + gw]
        return lax.dot_general(qst_ref[c, rs, :], kj, (((1,), (1,)), ((), ())),
                               preferred_element_type=F32)

    def weights(z, carry):
        nk = z.shape[1]
        sp = jnp.maximum(z, 0.0) + jnp.log(1.0 + jnp.exp2(-jnp.abs(z))) * LOG2_E
        cs = jnp.dot(sp.astype(BF16), tri_ref[0:nk, 0:nk],
                     preferred_element_type=F32)
        total = cs[:, 0:1] + sp[:, 0:1]
        if carry is None:
            return jnp.exp2(z - (sp + cs)).astype(BF16), total
        return jnp.exp2(z - (sp + cs + carry)).astype(BF16), carry + total

    def accumulate(c, rs, ab, start, assign=False):
        bl, l0 = where[c]
        nk = ab.shape[1]
        for _, qn, seg in segs:
            if seg.start < rs.start or seg.stop > rs.stop:
                continue
            half = (seg.stop - seg.start) // 2
            for p in range(GROUP_HEADS // 2):
                vj = v_ref[bl, pl.ds(start, nk), l0 + p * pair:l0 + (p + 1) * pair]
                src = seg.start - rs.start + p * half
                av = jnp.dot(ab[src:src + half, :], vj, preferred_element_type=F32)
                dst = slice(seg.start + p * half, seg.start + (p + 1) * half)
                if assign:
                    acc_ref[c, dst, :] = av
                else:
                    acc_ref[c, dst, :] += av

    d0 = pl.multiple_of(i * tk, tk)
    p0 = pl.multiple_of(jnp.maximum(i - 1, 0) * tk, tk)
    no_prev = jnp.where(i == 0, -MASKED_SCORE, 0.0).astype(F32)
    z_da = [scores(c, ra, d0, tk // 2) + bias_a_ref[...] for c in range(chains)]
    z_dbc = [scores(c, rbc, d0, tk) + bias_bc_ref[...] for c in range(chains)]
    z_pab = [scores(c, rab, p0, tk) for c in range(chains)]
    carry_ab = []
    low_c = []
    nb_rows = rb.stop - rb.start
    for c in range(chains):
        ab, cr_a = weights(z_da[c], None)
        accumulate(c, ra, ab, d0, assign=True)
        ab, cr_bc = weights(z_dbc[c], None)
        accumulate(c, rbc, ab, d0, assign=True)
        cr_bc = cr_bc + no_prev
        carry_ab.append(jnp.concatenate([cr_a + no_prev, cr_bc[0:nb_rows]], axis=0))
        carry_ref[c, rc, :] = cr_bc[nb_rows:]
        low_c.append(jnp.min(cr_bc[nb_rows:]))
    for c in range(chains):
        ab, cr = weights(z_pab[c], carry_ab[c])
        accumulate(c, rab, ab, p0)
        carry_ref[c, rab, :] = cr

    @pl.when(functools.reduce(jnp.minimum, low_c) < UNDERFLOW_BITS)
    def _():
        for c in range(chains):
            ab, cr = weights(scores(c, rc, p0, tk), carry_ref[c, rc, :])
            accumulate(c, rc, ab, p0)
            carry_ref[c, rc, :] = cr

    def lowest_carry():
        return functools.reduce(jnp.minimum, [jnp.min(carry_ref[c]) for c in range(chains)])

    def body(state):
        n, _ = state
        start = pl.multiple_of((i - 1 - n) * tk, tk)
        for c in range(chains):
            ab, cr = weights(scores(c, full, start, tk), carry_ref[c])
            accumulate(c, full, ab, start)
            carry_ref[c] = cr
        return n + 1, lowest_carry()

    def more(state):
        n, low = state
        return (n < i) & (low < UNDERFLOW_BITS)

    lax.while_loop(more, body, (jnp.int32(1), lowest_carry()))

    for bl in range(nb):
        for q0, qn, rs in segs:
            lane_p = lax.broadcasted_iota(jnp.int32, (qn, pair), 1)
            def head_rows(c, hh):
                r0 = rs.start + hh * qn
                return acc_ref[c, r0:r0 + qn, :]
            out = jnp.concatenate(
                [jnp.where(lane_p < HEAD_DIM, head_rows(c, 2 * p), head_rows(c, 2 * p + 1))
                 for c in range(bl * groups, (bl + 1) * groups) for p in range(GROUP_HEADS // 2)],
                axis=1)
            ms = _group_mean(out * out, gmat_ref[...])
            o_ref[bl, q0:q0 + qn, :] = (out * lax.rsqrt(ms + EPS) * g_ref[...]).astype(BF16)


def _attention(q, k, v, tri, bias_a, bias_bc, g_attn_out, gmat):
    b, s, w = q.shape
    nb = ATTN_BATCHES
    gw = GROUP_HEADS * HEAD_DIM
    chains = nb * (w // gw)
    rows = GROUP_HEADS * TQ
    const = lambda bi, i: (0, 0)
    return pl.pallas_call(
        _attn_kernel,
        grid=(b // nb, s // TQ),
        in_specs=[
            pl.BlockSpec((nb, TQ, w), lambda bi, i: (bi, i, 0)),
            pl.BlockSpec((nb, s, w), lambda bi, i: (bi, 0, 0), pipeline_mode=pl.Buffered(1)),
            pl.BlockSpec((nb, s, w), lambda bi, i: (bi, 0, 0), pipeline_mode=pl.Buffered(1)),
            pl.BlockSpec(tri.shape, const),
            pl.BlockSpec(bias_a.shape, const),
            pl.BlockSpec(bias_bc.shape, const),
            pl.BlockSpec((1, w), const),
            pl.BlockSpec((w, w), const),
        ],
        out_specs=pl.BlockSpec((nb, TQ, w), lambda bi, i: (bi, i, 0)),
        out_shape=jax.ShapeDtypeStruct((b, s, w), BF16),
        scratch_shapes=[pltpu.VMEM((chains, rows, gw), BF16),
                        pltpu.VMEM((chains, rows, 2 * HEAD_DIM), F32),
                        pltpu.VMEM((chains, rows, 1), F32)],
        compiler_params=pltpu.CompilerParams(
            dimension_semantics=("arbitrary", "arbitrary"),
            vmem_limit_bytes=VMEM_LIMIT_BYTES),
        name="sb_attention",
    )(q, k, v, tri, bias_a, bias_bc, g_attn_out, gmat)


def _post_kernel(x_ref, conv_ref, attn_ref, p_ref, wo_ref, gmlp_ref, wup_ref, wdn_ref,
                 gple_ref, wg_ref, wp_ref, gfin_ref, o_ref, *, apply_final):
    w_conv = conv_ref.shape[1]
    subs = [slice(r0, r0 + POST_SUB_ROWS) for r0 in range(0, x_ref.shape[0], POST_SUB_ROWS)]
    dot = functools.partial(jnp.dot, preferred_element_type=F32)
    h = [x_ref[rs, :] + dot(conv_ref[rs, :], wo_ref[0:w_conv, :]) for rs in subs]
    h = [hs + dot(attn_ref[rs, :], wo_ref[w_conv:, :]) for hs, rs in zip(h, subs)]
    m = [(hs * _rms_scale(hs) * gmlp_ref[...]).astype(BF16) for hs in h]
    r = [jnp.square(jnp.maximum(dot(ms, wup_ref[...]), 0.0)).astype(BF16) for ms in m]
    h = [hs + dot(rr, wdn_ref[...]) for hs, rr in zip(h, r)]
    n = [(hs * _rms_scale(hs) * gple_ref[...]).astype(BF16) for hs in h]
    pp = [dot(p_ref[rs, :].astype(BF16), wp_ref[...]) for rs in subs]
    gate = [jax.nn.sigmoid(dot(ns, wg_ref[...])) for ns in n]
    h = [hs + gs * ps for hs, gs, ps in zip(h, gate, pp)]
    for hs, rs in zip(h, subs):
        if apply_final:
            hs = hs * _rms_scale(hs) * gfin_ref[...]
        o_ref[rs, :] = hs


def _post(x2, conv2, attn2, p2, w_out, g_mlp, w_up, w_down, g_ple, w_gate, w_proj, g_final,
          apply_final):
    t, d = x2.shape
    tm = TM_POST
    tile = lambda i: (i, 0)
    const = lambda i: (0, 0)
    wspec = lambda w: pl.BlockSpec(w.shape, const, pipeline_mode=pl.Buffered(1))
    gspec = pl.BlockSpec((1, d), const)
    return pl.pallas_call(
        functools.partial(_post_kernel, apply_final=apply_final),
        grid=(t // tm,),
        in_specs=[
            pl.BlockSpec((tm, d), tile),
            pl.BlockSpec((tm, conv2.shape[1]), tile),
            pl.BlockSpec((tm, attn2.shape[1]), tile),
            pl.BlockSpec((tm, p2.shape[1]), tile),
            wspec(w_out), gspec, wspec(w_up), wspec(w_down), gspec, wspec(w_gate), wspec(w_proj),
            gspec,
        ],
        out_specs=pl.BlockSpec((tm, d), tile),
        out_shape=jax.ShapeDtypeStruct((t, d), F32),
        compiler_params=pltpu.CompilerParams(
            dimension_semantics=("arbitrary",),
            vmem_limit_bytes=VMEM_LIMIT_BYTES),
        name="post",
    )(x2, conv2, attn2, p2, w_out, g_mlp, w_up, w_down, g_ple, w_gate, w_proj, g_final)


def _group_matrix(width):
    g = jnp.arange(width) // HEAD_DIM
    return jnp.where(g[:, None] == g[None, :], 1.0 / HEAD_DIM, 0.0).astype(BF16)


def kernel(x, p, g_mix, w_in, conv_w, g_conv_out, g_attn_out, w_out, g_mlp, w_up, w_down,
           g_ple, w_ple_gate, w_ple_proj, g_final):
    b, s, d = x.shape
    depth = p.shape[0]
    w_conv = conv_w.shape[-1]
    w_attn = g_attn_out.shape[-1]
    assert s % TM_IN == 0 and s % TQ == 0 and TQ == TK and (b * s) % TM_POST == 0
    assert w_attn % (GROUP_HEADS * HEAD_DIM) == 0 and w_conv % HEAD_DIM == 0
    assert b % ATTN_BATCHES == 0

    idx = jnp.arange(TK)
    tri = (idx[:, None] > idx[None, :]).astype(BF16)
    causal = jnp.where(idx[None, :] < idx[:, None], 0.0, MASKED_SCORE).astype(F32)
    per_head = lambda m: jnp.tile(m, (GROUP_HEADS, 1))
    bias_a = per_head(causal[:TQ // 2, :TK // 2])
    bias_bc = jnp.concatenate([per_head(causal[TQ // 2:3 * TQ // 4]),
                               per_head(causal[3 * TQ // 4:])], axis=0)
    gmat_conv = _group_matrix(w_conv)
    gmat_attn = _group_matrix(w_attn)
    row = lambda g: g.reshape(1, -1)

    h = x
    for i in range(depth):
        (conv, q, k, v), (wo, wu, wd, wg, wp) = _mixer_in(
            h, row(g_mix[i]), w_in[i], conv_w[i], row(g_conv_out[i]), gmat_conv, w_conv, w_attn,
            [w_out[i], w_up[i], w_down[i], w_ple_gate[i], w_ple_proj[i]])
        attn = _attention(q, k, v, tri, bias_a, bias_bc, row(g_attn_out[i]), gmat_attn)
        h = _post(h.reshape(b * s, d), conv.reshape(b * s, w_conv), attn.reshape(b * s, w_attn),
                  p[i].reshape(b * s, -1), wo, row(g_mlp[i]), wu, wd, row(g_ple[i]), wg, wp,
                  row(g_final), apply_final=(i == depth - 1)).reshape(b, s, d)
    return h
```

```python
import functools

import jax
import jax.numpy as jnp
from jax import lax
from jax.experimental import pallas as pl
from jax.experimental.pallas import tpu as pltpu

HEAD_DIM = 64
CONV_K = 3
EPS = 1e-6
LOG2_E = 1.4426950408889634
UNDERFLOW_BITS = 150.0
MASKED_SCORE = -1e30

F32 = jnp.float32
BF16 = jnp.bfloat16

V7X_VMEM_BYTES = 64 * 1024 * 1024
VMEM_LIMIT_BYTES = V7X_VMEM_BYTES - 8 * 1024 * 1024
SUBLANES = 8
BF16_SUBLANES = 16

TM_IN = 1024
IN_SUB_ROWS = 256
TQ = 256
TK = 256
GROUP_HEADS = 4


def _rms_scale(x):
    return lax.rsqrt(jnp.mean(x * x, axis=-1, keepdims=True) + EPS)


def _group_mean(sq, gmat):
    hi = sq.astype(BF16)
    lo = (sq - hi.astype(F32)).astype(BF16)
    return (jnp.dot(hi, gmat, preferred_element_type=F32)
            + jnp.dot(lo, gmat, preferred_element_type=F32))


def _mixer_in_kernel(x_ref, g_ref, w32_ref, cw_ref, gco_ref, gmat_ref, *rest, w_conv, w_attn,
                     n_side):
    side_in, rest = rest[:n_side], rest[n_side:]
    conv_ref, q_ref, k_ref, v_ref = rest[:4]
    side_out = rest[4:4 + n_side]
    vbuf, w_ref = rest[4 + n_side:]
    tm = x_ref.shape[1]

    @pl.when((pl.program_id(0) == 0) & (pl.program_id(1) == 0))
    def _():
        for r0 in range(0, w32_ref.shape[0], IN_SUB_ROWS):
            w_ref[r0:r0 + IN_SUB_ROWS, :] = w32_ref[r0:r0 + IN_SUB_ROWS, :].astype(BF16)

    for src, dst in zip(side_in, side_out):
        dst[...] = src[...].astype(BF16)

    @pl.when(pl.program_id(1) == 0)
    def _():
        vbuf[0:SUBLANES, :] = jnp.zeros((SUBLANES, w_conv), F32)

    starts = list(range(0, tm, IN_SUB_ROWS))
    a = []
    for r0 in starts:
        x = x_ref[0, r0:r0 + IN_SUB_ROWS, :]
        a.append((x * _rms_scale(x) * g_ref[...]).astype(BF16))
    proj = [jnp.dot(as_, w_ref[...], preferred_element_type=F32) for as_ in a]

    o = 3 * w_conv
    for r0, pr in zip(starts, proj):
        rs = slice(r0, r0 + IN_SUB_ROWS)
        vbuf[SUBLANES + r0:SUBLANES + r0 + IN_SUB_ROWS, :] = (
            pr[:, w_conv:2 * w_conv] * pr[:, 2 * w_conv:3 * w_conv])
        q_ref[0, rs, :] = (pr[:, o:o + w_attn] * (HEAD_DIM ** -0.5 * LOG2_E)).astype(BF16)
        k_ref[0, rs, :] = pr[:, o + w_attn:o + 2 * w_attn].astype(BF16)
        v_ref[0, rs, :] = pr[:, o + 2 * w_attn:o + 3 * w_attn].astype(BF16)

    cw = cw_ref[...]
    for r0, pr in zip(starts, proj):
        v0 = vbuf[SUBLANES + r0:SUBLANES + r0 + IN_SUB_ROWS, :]
        v1 = vbuf[SUBLANES - 1 + r0:SUBLANES - 1 + r0 + IN_SUB_ROWS, :]
        v2 = vbuf[SUBLANES - 2 + r0:SUBLANES - 2 + r0 + IN_SUB_ROWS, :]
        c = pr[:, 0:w_conv] * (cw[0:1, :] * v2 + cw[1:2, :] * v1 + cw[2:3, :] * v0)
        ms = _group_mean(c * c, gmat_ref[...])
        conv_ref[0, r0:r0 + IN_SUB_ROWS, :] = (
            c * lax.rsqrt(ms + EPS) * gco_ref[...]).astype(BF16)
    vbuf[0:SUBLANES, :] = vbuf[tm:tm + SUBLANES, :]


def _mixer_in(x, g_mix, w_in, conv_w, g_conv_out, gmat, w_conv, w_attn, side_weights):
    b, s, d = x.shape
    tm = TM_IN
    n_i = s // tm
    steps = b * n_i
    const = lambda *_: (0, 0)
    tile = lambda bi, i: (bi, i, 0)
    out_sd = lambda w: jax.ShapeDtypeStruct((b, s, w), BF16)
    side_rows = [w.shape[0] // steps for w in side_weights]
    assert all(w.shape[0] % steps == 0 and r % BF16_SUBLANES == 0
               for w, r in zip(side_weights, side_rows))
    side_specs = [pl.BlockSpec((r, w.shape[1]), lambda bi, i: (bi * n_i + i, 0))
                  for w, r in zip(side_weights, side_rows)]
    outs = pl.pallas_call(
        functools.partial(_mixer_in_kernel, w_conv=w_conv, w_attn=w_attn,
                          n_side=len(side_weights)),
        grid=(b, n_i),
        in_specs=[
            pl.BlockSpec((1, tm, d), tile),
            pl.BlockSpec((1, d), const),
            pl.BlockSpec(w_in.shape, const, pipeline_mode=pl.Buffered(1)),
            pl.BlockSpec((CONV_K, w_conv), const),
            pl.BlockSpec((1, w_conv), const),
            pl.BlockSpec((w_conv, w_conv), const),
        ] + side_specs,
        out_specs=[
            pl.BlockSpec((1, tm, w_conv), tile),
            pl.BlockSpec((1, tm, w_attn), tile),
            pl.BlockSpec((1, tm, w_attn), tile),
            pl.BlockSpec((1, tm, w_attn), tile),
        ] + side_specs,
        out_shape=[out_sd(w_conv), out_sd(w_attn), out_sd(w_attn), out_sd(w_attn)]
        + [jax.ShapeDtypeStruct(w.shape, BF16) for w in side_weights],
        scratch_shapes=[pltpu.VMEM((tm + SUBLANES, w_conv), F32),
                        pltpu.VMEM(w_in.shape, BF16)],
        compiler_params=pltpu.CompilerParams(
            dimension_semantics=("arbitrary", "arbitrary"),
            vmem_limit_bytes=VMEM_LIMIT_BYTES),
        name="mixer_in",
    )(x, g_mix, w_in, conv_w, g_conv_out, gmat, *side_weights)
    return outs[:4], outs[4:]


def _attn_stages(q_ref, k_ref, v_ref, tri_ref, bias_a_ref, bias_bc_ref, g_ref, gmat_ref,
                 qst_ref, acc_ref, carry_ref, i, store):
    tq = q_ref.shape[1]
    tk = tri_ref.shape[0]
    chains, rows, gw = qst_ref.shape
    pair = 2 * HEAD_DIM
    lane0 = [c * gw for c in range(chains)]
    segs = []
    for q0, qn in ((0, tq // 2), (tq // 2, tq // 4), (3 * tq // 4, tq // 4)):
        segs.append((q0, qn, slice(GROUP_HEADS * q0, GROUP_HEADS * (q0 + qn))))
    (_, _, ra), (_, _, rb), (_, _, rc) = segs
    rab, rbc, full = slice(0, rb.stop), slice(rb.start, rows), slice(0, rows)
    nb_rows = rb.stop - rb.start
    d0 = pl.multiple_of(i * tk, tk)
    p0 = pl.multiple_of(jnp.maximum(i - 1, 0) * tk, tk)
    live = {}

    def scores(c, rs, start, nk):
        kj = k_ref[0, pl.ds(start, nk), lane0[c]:lane0[c] + gw]
        return lax.dot_general(qst_ref[c, rs, :], kj, (((1,), (1,)), ((), ())),
                               preferred_element_type=F32)

    def weights(z, carry):
        nk = z.shape[1]
        sp = jnp.maximum(z, 0.0) + jnp.log(1.0 + jnp.exp2(-jnp.abs(z))) * LOG2_E
        cs = jnp.dot(sp.astype(BF16), tri_ref[0:nk, 0:nk],
                     preferred_element_type=F32)
        total = cs[:, 0:1] + sp[:, 0:1]
        if carry is None:
            return jnp.exp2(z - (sp + cs)).astype(BF16), total
        return jnp.exp2(z - (sp + cs + carry)).astype(BF16), carry + total

    def accumulate(c, rs, ab, start, assign=False):
        nk = ab.shape[1]
        for _, qn, seg in segs:
            if seg.start < rs.start or seg.stop > rs.stop:
                continue
            half = (seg.stop - seg.start) // 2
            for p in range(GROUP_HEADS // 2):
                vj = v_ref[0, pl.ds(start, nk), lane0[c] + p * pair:lane0[c] + (p + 1) * pair]
                src = seg.start - rs.start + p * half
                av = jnp.dot(ab[src:src + half, :], vj, preferred_element_type=F32)
                dst = slice(seg.start + p * half, seg.start + (p + 1) * half)
                if assign:
                    acc_ref[c, dst, :] = av
                else:
                    acc_ref[c, dst, :] += av

    def stack_q():
        for c in range(chains):
            for q0, qn, rs in segs:
                q = q_ref[0, q0:q0 + qn, lane0[c]:lane0[c] + gw]
                lane = lax.broadcasted_iota(jnp.int32, (qn, gw), 1)
                for hh in range(GROUP_HEADS):
                    in_head = (lane >= hh * HEAD_DIM) & (lane < (hh + 1) * HEAD_DIM)
                    r0 = rs.start + hh * qn
                    qst_ref[c, r0:r0 + qn, :] = jnp.where(in_head, q, jnp.zeros_like(q))

    def first_scores():
        live["z_da"] = [scores(c, ra, d0, tk // 2) + bias_a_ref[...] for c in range(chains)]
        live["z_dbc"] = [scores(c, rbc, d0, tk) + bias_bc_ref[...] for c in range(chains)]
        live["z_pab"] = [scores(c, rab, p0, tk) for c in range(chains)]

    def diagonal():
        no_prev = jnp.where(i == 0, -MASKED_SCORE, 0.0).astype(F32)
        live["carry_ab"], live["low_c"] = [], []
        for c in range(chains):
            ab, cr_a = weights(live["z_da"][c], None)
            accumulate(c, ra, ab, d0, assign=True)
            ab, cr_bc = weights(live["z_dbc"][c], None)
            accumulate(c, rbc, ab, d0, assign=True)
            cr_bc = cr_bc + no_prev
            live["carry_ab"].append(
                jnp.concatenate([cr_a + no_prev, cr_bc[0:nb_rows]], axis=0))
            carry_ref[c, rc, :] = cr_bc[nb_rows:]
            live["low_c"].append(jnp.min(cr_bc[nb_rows:]))

    def previous():
        for c in range(chains):
            ab, cr = weights(live["z_pab"][c], live["carry_ab"][c])
            accumulate(c, rab, ab, p0)
            carry_ref[c, rab, :] = cr

    def tail():
        @pl.when(functools.reduce(jnp.minimum, live["low_c"]) < UNDERFLOW_BITS)
        def _():
            for c in range(chains):
                ab, cr = weights(scores(c, rc, p0, tk), carry_ref[c, rc, :])
                accumulate(c, rc, ab, p0)
                carry_ref[c, rc, :] = cr

        def lowest_carry():
            return functools.reduce(jnp.minimum,
                                    [jnp.min(carry_ref[c]) for c in range(chains)])

        def body(state):
            n, _ = state
            start = pl.multiple_of((i - 1 - n) * tk, tk)
            for c in range(chains):
                ab, cr = weights(scores(c, full, start, tk), carry_ref[c])
                accumulate(c, full, ab, start)
                carry_ref[c] = cr
            return n + 1, lowest_carry()

        def more(state):
            n, low = state
            return (n < i) & (low < UNDERFLOW_BITS)

        lax.while_loop(more, body, (jnp.int32(1), lowest_carry()))

        for q0, qn, rs in segs:
            lane_p = lax.broadcasted_iota(jnp.int32, (qn, pair), 1)
            def head_rows(c, hh):
                r0 = rs.start + hh * qn
                return acc_ref[c, r0:r0 + qn, :]
            out = jnp.concatenate(
                [jnp.where(lane_p < HEAD_DIM, head_rows(c, 2 * p), head_rows(c, 2 * p + 1))
                 for c in range(chains) for p in range(GROUP_HEADS // 2)], axis=1)
            ms = _group_mean(out * out, gmat_ref[...])
            store(q0, qn, (out * lax.rsqrt(ms + EPS) * g_ref[...]).astype(BF16))

    return stack_q, first_scores, diagonal, previous, tail


def _post_stages(x_ref, conv_ref, attn_ref, p_ref, wo_ref, gmlp_ref, wup_ref, wdn_ref,
                 gple_ref, wg_ref, wp_ref, gfin_ref, o_ref, apply_final):
    w_conv = conv_ref.shape[1]
    dot = functools.partial(jnp.dot, preferred_element_type=F32)
    live = {}

    def out_proj():
        h = x_ref[...] + dot(conv_ref[...], wo_ref[0:w_conv, :])
        live["h"] = h + dot(attn_ref[...], wo_ref[w_conv:, :])

    def mlp_up():
        h = live["h"]
        m = (h * _rms_scale(h) * gmlp_ref[...]).astype(BF16)
        live["u"] = dot(m, wup_ref[...])

    def mlp_down():
        r = jnp.square(jnp.maximum(live["u"], 0.0)).astype(BF16)
        live["h"] = live["h"] + dot(r, wdn_ref[...])

    def embed_and_norm():
        h = live["h"]
        n = (h * _rms_scale(h) * gple_ref[...]).astype(BF16)
        pp = dot(p_ref[...].astype(BF16), wp_ref[...])
        h = h + jax.nn.sigmoid(dot(n, wg_ref[...])) * pp
        if apply_final:
            h = h * _rms_scale(h) * gfin_ref[...]
        o_ref[...] = h

    return out_proj, mlp_up, mlp_down, embed_and_norm


def _attn_post_kernel(q_ref, k_ref, v_ref, tri_ref, bias_a_ref, bias_bc_ref, gattn_ref, gmat_ref,
                      x_ref, conv_ref, p_ref, wo_ref, gmlp_ref, wup_ref, wdn_ref, gple_ref,
                      wg_ref, wp_ref, gfin_ref, o_ref,
                      qst_ref, acc_ref, carry_ref, attn_ref, *, n_q, tiles, apply_final):
    s = pl.program_id(0)
    i = lax.rem(jnp.minimum(s, tiles - 1), n_q)

    @pl.when(s == 0)
    def _():
        attn_ref[...] = jnp.zeros(attn_ref.shape, attn_ref.dtype)

    def store(q0, qn, rows):
        attn_ref[q0:q0 + qn, :] = rows

    stack_q, first_scores, diagonal, previous, tail = _attn_stages(
        q_ref, k_ref, v_ref, tri_ref, bias_a_ref, bias_bc_ref, gattn_ref, gmat_ref,
        qst_ref, acc_ref, carry_ref, i, store)
    out_proj, mlp_up, mlp_down, embed_and_norm = _post_stages(
        x_ref, conv_ref, attn_ref, p_ref, wo_ref, gmlp_ref, wup_ref, wdn_ref,
        gple_ref, wg_ref, wp_ref, gfin_ref, o_ref, apply_final)

    stack_q()
    out_proj()
    first_scores()
    mlp_up()
    diagonal()
    mlp_down()
    previous()
    embed_and_norm()
    tail()


def _attention_and_post(q, k, v, tri, bias_a, bias_bc, g_attn_out, gmat, x2, conv2, p2,
                        w_out, g_mlp, w_up, w_down, g_ple, w_gate, w_proj, g_final, apply_final):
    b, s, w = q.shape
    t, d = x2.shape
    n_q = s // TQ
    tiles = b * n_q
    gw = GROUP_HEADS * HEAD_DIM
    chains = w // gw
    rows = GROUP_HEADS * TQ
    att = lambda st: jnp.minimum(st, tiles - 1)
    q_map = lambda st: (att(st) // n_q, att(st) % n_q, 0)
    kv_map = lambda st: (att(st) // n_q, 0, 0)
    tok_map = lambda st: (jnp.maximum(st - 1, 0), 0)
    const = lambda st: (0, 0)
    wspec = lambda arr: pl.BlockSpec(arr.shape, const, pipeline_mode=pl.Buffered(1))
    gspec = pl.BlockSpec((1, d), const)
    return pl.pallas_call(
        functools.partial(_attn_post_kernel, n_q=n_q, tiles=tiles, apply_final=apply_final),
        grid=(tiles + 1,),
        in_specs=[
            pl.BlockSpec((1, TQ, w), q_map),
            pl.BlockSpec((1, s, w), kv_map, pipeline_mode=pl.Buffered(1)),
            pl.BlockSpec((1, s, w), kv_map, pipeline_mode=pl.Buffered(1)),
            pl.BlockSpec(tri.shape, const),
            pl.BlockSpec(bias_a.shape, const),
            pl.BlockSpec(bias_bc.shape, const),
            pl.BlockSpec((1, w), const),
            pl.BlockSpec((w, w), const),
            pl.BlockSpec((TQ, d), tok_map),
            pl.BlockSpec((TQ, conv2.shape[1]), tok_map),
            pl.BlockSpec((TQ, p2.shape[1]), tok_map),
            wspec(w_out), gspec, wspec(w_up), wspec(w_down), gspec, wspec(w_gate), wspec(w_proj),
            gspec,
        ],
        out_specs=pl.BlockSpec((TQ, d), tok_map),
        out_shape=jax.ShapeDtypeStruct((t, d), F32),
        scratch_shapes=[pltpu.VMEM((chains, rows, gw), BF16),
                        pltpu.VMEM((chains, rows, 2 * HEAD_DIM), F32),
                        pltpu.VMEM((chains, rows, 1), F32),
                        pltpu.VMEM((TQ, w), BF16)],
        compiler_params=pltpu.CompilerParams(
            dimension_semantics=("arbitrary",),
            vmem_limit_bytes=VMEM_LIMIT_BYTES),
        name="attn_post",
    )(q, k, v, tri, bias_a, bias_bc, g_attn_out, gmat, x2, conv2, p2,
      w_out, g_mlp, w_up, w_down, g_ple, w_gate, w_proj, g_final)


def _group_matrix(width):
    g = jnp.arange(width) // HEAD_DIM
    return jnp.where(g[:, None] == g[None, :], 1.0 / HEAD_DIM, 0.0).astype(BF16)


def kernel(x, p, g_mix, w_in, conv_w, g_conv_out, g_attn_out, w_out, g_mlp, w_up, w_down,
           g_ple, w_ple_gate, w_ple_proj, g_final):
    b, s, d = x.shape
    depth = p.shape[0]
    w_conv = conv_w.shape[-1]
    w_attn = g_attn_out.shape[-1]
    assert s % TM_IN == 0 and s % TQ == 0 and TQ == TK
    assert w_attn % (GROUP_HEADS * HEAD_DIM) == 0 and w_conv % HEAD_DIM == 0

    idx = jnp.arange(TK)
    tri = (idx[:, None] > idx[None, :]).astype(BF16)
    causal = jnp.where(idx[None, :] < idx[:, None], 0.0, MASKED_SCORE).astype(F32)
    per_head = lambda m: jnp.tile(m, (GROUP_HEADS, 1))
    bias_a = per_head(causal[:TQ // 2, :TK // 2])
    bias_bc = jnp.concatenate([per_head(causal[TQ // 2:3 * TQ // 4]),
                               per_head(causal[3 * TQ // 4:])], axis=0)
    gmat_conv = _group_matrix(w_conv)
    gmat_attn = _group_matrix(w_attn)
    row = lambda g: g.reshape(1, -1)

    h = x
    for i in range(depth):
        (conv, q, k, v), (wo, wu, wd, wg, wp) = _mixer_in(
            h, row(g_mix[i]), w_in[i], conv_w[i], row(g_conv_out[i]), gmat_conv, w_conv, w_attn,
            [w_out[i], w_up[i], w_down[i], w_ple_gate[i], w_ple_proj[i]])
        h = _attention_and_post(
            q, k, v, tri, bias_a, bias_bc, row(g_attn_out[i]), gmat_attn,
            h.reshape(b * s, d), conv.reshape(b * s, w_conv), p[i].reshape(b * s, -1),
            wo, row(g_mlp[i]), wu, wd, row(g_ple[i]), wg, wp, row(g_final),
            apply_final=(i == depth - 1)).reshape(b, s, d)
    return h
```

```python
import functools

import jax
import jax.numpy as jnp
from jax import lax
from jax.experimental import pallas as pl
from jax.experimental.pallas import tpu as pltpu

HEAD_DIM = 64
CONV_K = 3
EPS = 1e-6
LOG2_E = 1.4426950408889634
UNDERFLOW_BITS = 150.0
MASKED_SCORE = -1e30

F32 = jnp.float32
BF16 = jnp.bfloat16

V7X_VMEM_BYTES = 64 * 1024 * 1024
VMEM_LIMIT_BYTES = V7X_VMEM_BYTES - 8 * 1024 * 1024
SUBLANES = 8
BF16_SUBLANES = 16
MXU_TILE = 256

TM_IN = 1024
IN_SUB_ROWS = 256
TQ = 256
TK = 256
GROUP_HEADS = 4
ATTN_BATCHES = 2
TM_POST = 512
POST_SUB_ROWS = 256


def _rms_scale(x):
    return lax.rsqrt(jnp.mean(x * x, axis=-1, keepdims=True) + EPS)


def _group_mean(sq, gmat):
    sqb = sq.astype(BF16)
    return jnp.concatenate(
        [jnp.dot(sqb[:, l0:l0 + MXU_TILE], gmat, preferred_element_type=F32)
         for l0 in range(0, sq.shape[1], MXU_TILE)], axis=1)


def _mixer_in_kernel(x_ref, g_ref, w32_ref, cw_ref, gco_ref, gmat_ref, *rest, w_conv, w_attn,
                     n_side):
    side_in, rest = rest[:n_side], rest[n_side:]
    conv_ref, q_ref, k_ref, v_ref = rest[:4]
    side_out = rest[4:4 + n_side]
    vbuf, w_ref = rest[4 + n_side:]
    tm = x_ref.shape[1]

    @pl.when((pl.program_id(0) == 0) & (pl.program_id(1) == 0))
    def _():
        for r0 in range(0, w32_ref.shape[0], IN_SUB_ROWS):
            w_ref[r0:r0 + IN_SUB_ROWS, :] = w32_ref[r0:r0 + IN_SUB_ROWS, :].astype(BF16)

    for src, dst in zip(side_in, side_out):
        dst[...] = src[...].astype(BF16)

    @pl.when(pl.program_id(1) == 0)
    def _():
        vbuf[0:SUBLANES, :] = jnp.zeros((SUBLANES, w_conv), F32)

    starts = list(range(0, tm, IN_SUB_ROWS))
    a = []
    for r0 in starts:
        x = x_ref[0, r0:r0 + IN_SUB_ROWS, :]
        a.append((x * _rms_scale(x) * g_ref[...]).astype(BF16))
    proj = [jnp.dot(as_, w_ref[...], preferred_element_type=F32) for as_ in a]

    o = 3 * w_conv
    for r0, pr in zip(starts, proj):
        rs = slice(r0, r0 + IN_SUB_ROWS)
        vbuf[SUBLANES + r0:SUBLANES + r0 + IN_SUB_ROWS, :] = (
            pr[:, w_conv:2 * w_conv] * pr[:, 2 * w_conv:3 * w_conv])
        q_ref[0, rs, :] = (pr[:, o:o + w_attn] * (HEAD_DIM ** -0.5 * LOG2_E)).astype(BF16)
        k_ref[0, rs, :] = pr[:, o + w_attn:o + 2 * w_attn].astype(BF16)
        v_ref[0, rs, :] = pr[:, o + 2 * w_attn:o + 3 * w_attn].astype(BF16)

    cw = cw_ref[...]
    for r0, pr in zip(starts, proj):
        v0 = vbuf[SUBLANES + r0:SUBLANES + r0 + IN_SUB_ROWS, :]
        v1 = vbuf[SUBLANES - 1 + r0:SUBLANES - 1 + r0 + IN_SUB_ROWS, :]
        v2 = vbuf[SUBLANES - 2 + r0:SUBLANES - 2 + r0 + IN_SUB_ROWS, :]
        c = pr[:, 0:w_conv] * (cw[0:1, :] * v2 + cw[1:2, :] * v1 + cw[2:3, :] * v0)
        ms = _group_mean(c * c, gmat_ref[...])
        conv_ref[0, r0:r0 + IN_SUB_ROWS, :] = (
            c * lax.rsqrt(ms + EPS) * gco_ref[...]).astype(BF16)
    vbuf[0:SUBLANES, :] = vbuf[tm:tm + SUBLANES, :]


def _mixer_in(x, g_mix, w_in, conv_w, g_conv_out, gmat, w_conv, w_attn, side_weights):
    b, s, d = x.shape
    tm = TM_IN
    n_i = s // tm
    steps = b * n_i
    const = lambda *_: (0, 0)
    tile = lambda bi, i: (bi, i, 0)
    out_sd = lambda w: jax.ShapeDtypeStruct((b, s, w), BF16)
    side_rows = [w.shape[0] // steps for w in side_weights]
    assert all(w.shape[0] % steps == 0 and r % BF16_SUBLANES == 0
               for w, r in zip(side_weights, side_rows))
    side_specs = [pl.BlockSpec((r, w.shape[1]), lambda bi, i: (bi * n_i + i, 0))
                  for w, r in zip(side_weights, side_rows)]
    outs = pl.pallas_call(
        functools.partial(_mixer_in_kernel, w_conv=w_conv, w_attn=w_attn,
                          n_side=len(side_weights)),
        grid=(b, n_i),
        in_specs=[
            pl.BlockSpec((1, tm, d), tile),
            pl.BlockSpec((1, d), const),
            pl.BlockSpec(w_in.shape, const, pipeline_mode=pl.Buffered(1)),
            pl.BlockSpec((CONV_K, w_conv), const),
            pl.BlockSpec((1, w_conv), const),
            pl.BlockSpec(gmat.shape, const),
        ] + side_specs,
        out_specs=[
            pl.BlockSpec((1, tm, w_conv), tile),
            pl.BlockSpec((1, tm, w_attn), tile),
            pl.BlockSpec((1, tm, w_attn), tile),
            pl.BlockSpec((1, tm, w_attn), tile),
        ] + side_specs,
        out_shape=[out_sd(w_conv), out_sd(w_attn), out_sd(w_attn), out_sd(w_attn)]
        + [jax.ShapeDtypeStruct(w.shape, BF16) for w in side_weights],
        scratch_shapes=[pltpu.VMEM((tm + SUBLANES, w_conv), F32),
                        pltpu.VMEM(w_in.shape, BF16)],
        compiler_params=pltpu.CompilerParams(
            dimension_semantics=("arbitrary", "arbitrary"),
            vmem_limit_bytes=VMEM_LIMIT_BYTES),
        name="mixer_in",
    )(x, g_mix, w_in, conv_w, g_conv_out, gmat, *side_weights)
    return outs[:4], outs[4:]


def _attn_kernel(q_ref, k_ref, v_ref, tri_ref, bias_a_ref, bias_bc_ref, g_ref, gmat_ref,
                 o_ref, qst_ref, acc_ref, carry_ref):
    nb, tq = q_ref.shape[0], q_ref.shape[1]
    tk = tri_ref.shape[0]
    chains, rows, gw = qst_ref.shape
    groups = chains // nb
    pair = 2 * HEAD_DIM
    i = pl.program_id(1)
    where = [(c // groups, (c % groups) * gw) for c in range(chains)]
    segs = []
    for q0, qn in ((0, tq // 2), (tq // 2, tq // 4), (3 * tq // 4, tq // 4)):
        segs.append((q0, qn, slice(GROUP_HEADS * q0, GROUP_HEADS * (q0 + qn))))
    (_, _, ra), (_, _, rb), (_, _, rc) = segs
    rab, rbc, full = slice(0, rb.stop), slice(rb.start, rows), slice(0, rows)

    for c, (bl, l0) in enumerate(where):
        for q0, qn, rs in segs:
            q = q_ref[bl, q0:q0 + qn, l0:l0 + gw]
            lane = lax.broadcasted_iota(jnp.int32, (qn, gw), 1)
            for hh in range(GROUP_HEADS):
                in_head = (lane >= hh * HEAD_DIM) & (lane < (hh + 1) * HEAD_DIM)
                r0 = rs.start + hh * qn
                qst_ref[c, r0:r0 + qn, :] = jnp.where(in_head, q, jnp.zeros_like(q))

    def scores(c, rs, start, nk):
        bl, l0 = where[c]
        kj = k_ref[bl, pl.ds(start, nk), l0:l0 + gw]
        return lax.dot_general(qst_ref[c, rs, :], kj, (((1,), (1,)), ((), ())),
                               preferred_element_type=F32)

    def weights(z, carry):
        nk = z.shape[1]
        sp = jnp.maximum(z, 0.0) + jnp.log(1.0 + jnp.exp2(-jnp.abs(z))) * LOG2_E
        cs = jnp.dot(sp.astype(BF16), tri_ref[0:nk, 0:nk],
                     preferred_element_type=F32)
        total = cs[:, 0:1] + sp[:, 0:1]
        if carry is None:
            return jnp.exp2(z - (sp + cs)).astype(BF16), total
        return jnp.exp2(z - (sp + cs + carry)).astype(BF16), carry + total

    def accumulate(c, rs, ab, start, assign=False):
        bl, l0 = where[c]
        nk = ab.shape[1]
        for _, qn, seg in segs:
            if seg.start < rs.start or seg.stop > rs.stop:
                continue
            half = (seg.stop - seg.start) // 2
            for p in range(GROUP_HEADS // 2):
                vj = v_ref[bl, pl.ds(start, nk), l0 + p * pair:l0 + (p + 1) * pair]
                src = seg.start - rs.start + p * half
                av = jnp.dot(ab[src:src + half, :], vj, preferred_element_type=F32)
                dst = slice(seg.start + p * half, seg.start + (p + 1) * half)
                if assign:
                    acc_ref[c, dst, :] = av
                else:
                    acc_ref[c, dst, :] += av

    d0 = pl.multiple_of(i * tk, tk)
    p0 = pl.multiple_of(jnp.maximum(i - 1, 0) * tk, tk)
    no_prev = jnp.where(i == 0, -MASKED_SCORE, 0.0).astype(F32)
    z_da = [scores(c, ra, d0, tk // 2) + bias_a_ref[...] for c in range(chains)]
    z_dbc = [scores(c, rbc, d0, tk) + bias_bc_ref[...] for c in range(chains)]
    z_pab = [scores(c, rab, p0, tk) for c in range(chains)]
    carry_ab = []
    low_c = []
    nb_rows = rb.stop - rb.start
    for c in range(chains):
        ab, cr_a = weights(z_da[c], None)
        accumulate(c, ra, ab, d0, assign=True)
        ab, cr_bc = weights(z_dbc[c], None)
        accumulate(c, rbc, ab, d0, assign=True)
        cr_bc = cr_bc + no_prev
        carry_ab.append(jnp.concatenate([cr_a + no_prev, cr_bc[0:nb_rows]], axis=0))
        carry_ref[c, rc, :] = cr_bc[nb_rows:]
        low_c.append(jnp.min(cr_bc[nb_rows:]))
    for c in range(chains):
        ab, cr = weights(z_pab[c], carry_ab[c])
        accumulate(c, rab, ab, p0)
        carry_ref[c, rab, :] = cr

    @pl.when(functools.reduce(jnp.minimum, low_c) < UNDERFLOW_BITS)
    def _():
        for c in range(chains):
            ab, cr = weights(scores(c, rc, p0, tk), carry_ref[c, rc, :])
            accumulate(c, rc, ab, p0)
            carry_ref[c, rc, :] = cr

    def lowest_carry():
        return functools.reduce(jnp.minimum, [jnp.min(carry_ref[c]) for c in range(chains)])

    def body(state):
        n, _ = state
        start = pl.multiple_of((i - 1 - n) * tk, tk)
        for c in range(chains):
            ab, cr = weights(scores(c, full, start, tk), carry_ref[c])
            accumulate(c, full, ab, start)
            carry_ref[c] = cr
        return n + 1, lowest_carry()

    def more(state):
        n, low = state
        return (n < i) & (low < UNDERFLOW_BITS)

    lax.while_loop(more, body, (jnp.int32(1), lowest_carry()))

    for bl in range(nb):
        for q0, qn, rs in segs:
            lane_p = lax.broadcasted_iota(jnp.int32, (qn, pair), 1)
            def head_rows(c, hh):
                r0 = rs.start + hh * qn
                return acc_ref[c, r0:r0 + qn, :]
            out = jnp.concatenate(
                [jnp.where(lane_p < HEAD_DIM, head_rows(c, 2 * p), head_rows(c, 2 * p + 1))
                 for c in range(bl * groups, (bl + 1) * groups) for p in range(GROUP_HEADS // 2)],
                axis=1)
            ms = _group_mean(out * out, gmat_ref[...])
            o_ref[bl, q0:q0 + qn, :] = (out * lax.rsqrt(ms + EPS) * g_ref[...]).astype(BF16)


def _attention(q, k, v, tri, bias_a, bias_bc, g_attn_out, gmat):
    b, s, w = q.shape
    nb = ATTN_BATCHES
    gw = GROUP_HEADS * HEAD_DIM
    chains = nb * (w // gw)
    rows = GROUP_HEADS * TQ
    const = lambda bi, i: (0, 0)
    return pl.pallas_call(
        _attn_kernel,
        grid=(b // nb, s // TQ),
        in_specs=[
            pl.BlockSpec((nb, TQ, w), lambda bi, i: (bi, i, 0)),
            pl.BlockSpec((nb, s, w), lambda bi, i: (bi, 0, 0), pipeline_mode=pl.Buffered(1)),
            pl.BlockSpec((nb, s, w), lambda bi, i: (bi, 0, 0), pipeline_mode=pl.Buffered(1)),
            pl.BlockSpec(tri.shape, const),
            pl.BlockSpec(bias_a.shape, const),
            pl.BlockSpec(bias_bc.shape, const),
            pl.BlockSpec((1, w), const),
            pl.BlockSpec(gmat.shape, const),
        ],
        out_specs=pl.BlockSpec((nb, TQ, w), lambda bi, i: (bi, i, 0)),
        out_shape=jax.ShapeDtypeStruct((b, s, w), BF16),
        scratch_shapes=[pltpu.VMEM((chains, rows, gw), BF16),
                        pltpu.VMEM((chains, rows, 2 * HEAD_DIM), F32),
                        pltpu.VMEM((chains, rows, 1), F32)],
        compiler_params=pltpu.CompilerParams(
            dimension_semantics=("arbitrary", "arbitrary"),
            vmem_limit_bytes=VMEM_LIMIT_BYTES),
        name="sb_attention",
    )(q, k, v, tri, bias_a, bias_bc, g_attn_out, gmat)


def _post_kernel(x_ref, conv_ref, attn_ref, p_ref, wo_ref, gmlp_ref, wup_ref, wdn_ref,
                 gple_ref, wg_ref, wp_ref, gfin_ref, o_ref, *, apply_final):
    w_conv = conv_ref.shape[1]
    subs = [slice(r0, r0 + POST_SUB_ROWS) for r0 in range(0, x_ref.shape[0], POST_SUB_ROWS)]
    dot = functools.partial(jnp.dot, preferred_element_type=F32)
    h = [x_ref[rs, :] + dot(conv_ref[rs, :], wo_ref[0:w_conv, :]) for rs in subs]
    h = [hs + dot(attn_ref[rs, :], wo_ref[w_conv:, :]) for hs, rs in zip(h, subs)]
    m = [(hs * _rms_scale(hs) * gmlp_ref[...]).astype(BF16) for hs in h]
    r = [jnp.square(jnp.maximum(dot(ms, wup_ref[...]), 0.0)).astype(BF16) for ms in m]
    h = [hs + dot(rr, wdn_ref[...]) for hs, rr in zip(h, r)]
    n = [(hs * _rms_scale(hs) * gple_ref[...]).astype(BF16) for hs in h]
    pp = [dot(p_ref[rs, :].astype(BF16), wp_ref[...]) for rs in subs]
    gate = [jax.nn.sigmoid(dot(ns, wg_ref[...])) for ns in n]
    h = [hs + gs * ps for hs, gs, ps in zip(h, gate, pp)]
    for hs, rs in zip(h, subs):
        if apply_final:
            hs = hs * _rms_scale(hs) * gfin_ref[...]
        o_ref[rs, :] = hs


def _post(x2, conv2, attn2, p2, w_out, g_mlp, w_up, w_down, g_ple, w_gate, w_proj, g_final,
          apply_final):
    t, d = x2.shape
    tm = TM_POST
    tile = lambda i: (i, 0)
    const = lambda i: (0, 0)
    wspec = lambda w: pl.BlockSpec(w.shape, const, pipeline_mode=pl.Buffered(1))
    gspec = pl.BlockSpec((1, d), const)
    return pl.pallas_call(
        functools.partial(_post_kernel, apply_final=apply_final),
        grid=(t // tm,),
        in_specs=[
            pl.BlockSpec((tm, d), tile),
            pl.BlockSpec((tm, conv2.shape[1]), tile),
            pl.BlockSpec((tm, attn2.shape[1]), tile),
            pl.BlockSpec((tm, p2.shape[1]), tile),
            wspec(w_out), gspec, wspec(w_up), wspec(w_down), gspec, wspec(w_gate), wspec(w_proj),
            gspec,
        ],
        out_specs=pl.BlockSpec((tm, d), tile),
        out_shape=jax.ShapeDtypeStruct((t, d), F32),
        compiler_params=pltpu.CompilerParams(
            dimension_semantics=("arbitrary",),
            vmem_limit_bytes=VMEM_LIMIT_BYTES),
        name="post",
    )(x2, conv2, attn2, p2, w_out, g_mlp, w_up, w_down, g_ple, w_gate, w_proj, g_final)


def _group_matrix():
    g = jnp.arange(MXU_TILE) // HEAD_DIM
    return jnp.where(g[:, None] == g[None, :], 1.0 / HEAD_DIM, 0.0).astype(BF16)


def kernel(x, p, g_mix, w_in, conv_w, g_conv_out, g_attn_out, w_out, g_mlp, w_up, w_down,
           g_ple, w_ple_gate, w_ple_proj, g_final):
    b, s, d = x.shape
    depth = p.shape[0]
    w_conv = conv_w.shape[-1]
    w_attn = g_attn_out.shape[-1]
    assert s % TM_IN == 0 and s % TQ == 0 and TQ == TK and (b * s) % TM_POST == 0
    assert w_attn % (GROUP_HEADS * HEAD_DIM) == 0 and w_conv % MXU_TILE == 0
    assert w_attn % MXU_TILE == 0 and MXU_TILE % HEAD_DIM == 0
    assert b % ATTN_BATCHES == 0

    idx = jnp.arange(TK)
    tri = (idx[:, None] > idx[None, :]).astype(BF16)
    causal = jnp.where(idx[None, :] < idx[:, None], 0.0, MASKED_SCORE).astype(F32)
    per_head = lambda m: jnp.tile(m, (GROUP_HEADS, 1))
    bias_a = per_head(causal[:TQ // 2, :TK // 2])
    bias_bc = jnp.concatenate([per_head(causal[TQ // 2:3 * TQ // 4]),
                               per_head(causal[3 * TQ // 4:])], axis=0)
    gmat = _group_matrix()
    row = lambda g: g.reshape(1, -1)

    h = x
    for i in range(depth):
        (conv, q, k, v), (wo, wu, wd, wg, wp) = _mixer_in(
            h, row(g_mix[i]), w_in[i], conv_w[i], row(g_conv_out[i]), gmat, w_conv, w_attn,
            [w_out[i], w_up[i], w_down[i], w_ple_gate[i], w_ple_proj[i]])
        attn = _attention(q, k, v, tri, bias_a, bias_bc, row(g_attn_out[i]), gmat)
        h = _post(h.reshape(b * s, d), conv.reshape(b * s, w_conv), attn.reshape(b * s, w_attn),
                  p[i].reshape(b * s, -1), wo, row(g_mlp[i]), wu, wd, row(g_ple[i]), wg, wp,
                  row(g_final), apply_final=(i == depth - 1)).reshape(b, s, d)
    return h
```

```python
import functools

import jax
import jax.numpy as jnp
from jax import lax
from jax.experimental import pallas as pl
from jax.experimental.pallas import tpu as pltpu

HEAD_DIM = 64
CONV_K = 3
EPS = 1e-6
LOG2_E = 1.4426950408889634
UNDERFLOW_BITS = 150.0
MASKED_SCORE = -1e30

F32 = jnp.float32
BF16 = jnp.bfloat16

V7X_VMEM_BYTES = 64 * 1024 * 1024
VMEM_LIMIT_BYTES = V7X_VMEM_BYTES - 8 * 1024 * 1024
ATTN_VMEM_LIMIT_BYTES = V7X_VMEM_BYTES - 2 * 1024 * 1024
SUBLANES = 8
BF16_SUBLANES = 16
MXU_TILE = 256

TM_IN = 1024
IN_SUB_ROWS = 256
TQ = 256
TK = 256
GROUP_HEADS = 4
ATTN_BATCHES = 2
ATTN_TILES = 2
TM_POST = 512
POST_SUB_ROWS = 256


def _rms_scale(x):
    return lax.rsqrt(jnp.mean(x * x, axis=-1, keepdims=True) + EPS)


def _group_mean(sq, gmat):
    sqb = sq.astype(BF16)
    return jnp.concatenate(
        [jnp.dot(sqb[:, l0:l0 + MXU_TILE], gmat, preferred_element_type=F32)
         for l0 in range(0, sq.shape[1], MXU_TILE)], axis=1)


def _mixer_in_kernel(x_ref, g_ref, w32_ref, cw_ref, gco_ref, gmat_ref, *rest, w_conv, w_attn,
                     n_side):
    side_in, rest = rest[:n_side], rest[n_side:]
    conv_ref, q_ref, k_ref, v_ref = rest[:4]
    side_out = rest[4:4 + n_side]
    vbuf, w_ref = rest[4 + n_side:]
    tm = x_ref.shape[1]

    @pl.when((pl.program_id(0) == 0) & (pl.program_id(1) == 0))
    def _():
        for r0 in range(0, w32_ref.shape[0], IN_SUB_ROWS):
            w_ref[r0:r0 + IN_SUB_ROWS, :] = w32_ref[r0:r0 + IN_SUB_ROWS, :].astype(BF16)

    for src, dst in zip(side_in, side_out):
        dst[...] = src[...].astype(BF16)

    @pl.when(pl.program_id(1) == 0)
    def _():
        vbuf[0:SUBLANES, :] = jnp.zeros((SUBLANES, w_conv), F32)

    starts = list(range(0, tm, IN_SUB_ROWS))
    a = []
    for r0 in starts:
        x = x_ref[0, r0:r0 + IN_SUB_ROWS, :]
        a.append((x * _rms_scale(x) * g_ref[...]).astype(BF16))
    proj = [jnp.dot(as_, w_ref[...], preferred_element_type=F32) for as_ in a]

    o = 3 * w_conv
    for r0, pr in zip(starts, proj):
        rs = slice(r0, r0 + IN_SUB_ROWS)
        vbuf[SUBLANES + r0:SUBLANES + r0 + IN_SUB_ROWS, :] = (
            pr[:, w_conv:2 * w_conv] * pr[:, 2 * w_conv:3 * w_conv])
        q_ref[0, rs, :] = (pr[:, o:o + w_attn] * (HEAD_DIM ** -0.5 * LOG2_E)).astype(BF16)
        k_ref[0, rs, :] = pr[:, o + w_attn:o + 2 * w_attn].astype(BF16)
        v_ref[0, rs, :] = pr[:, o + 2 * w_attn:o + 3 * w_attn].astype(BF16)

    cw = cw_ref[...]
    for r0, pr in zip(starts, proj):
        v0 = vbuf[SUBLANES + r0:SUBLANES + r0 + IN_SUB_ROWS, :]
        v1 = vbuf[SUBLANES - 1 + r0:SUBLANES - 1 + r0 + IN_SUB_ROWS, :]
        v2 = vbuf[SUBLANES - 2 + r0:SUBLANES - 2 + r0 + IN_SUB_ROWS, :]
        c = pr[:, 0:w_conv] * (cw[0:1, :] * v2 + cw[1:2, :] * v1 + cw[2:3, :] * v0)
        ms = _group_mean(c * c, gmat_ref[...])
        conv_ref[0, r0:r0 + IN_SUB_ROWS, :] = (
            c * lax.rsqrt(ms + EPS) * gco_ref[...]).astype(BF16)
    vbuf[0:SUBLANES, :] = vbuf[tm:tm + SUBLANES, :]


def _mixer_in(x, g_mix, w_in, conv_w, g_conv_out, gmat, w_conv, w_attn, side_weights):
    b, s, d = x.shape
    tm = TM_IN
    n_i = s // tm
    steps = b * n_i
    const = lambda *_: (0, 0)
    tile = lambda bi, i: (bi, i, 0)
    out_sd = lambda w: jax.ShapeDtypeStruct((b, s, w), BF16)
    side_rows = [w.shape[0] // steps for w in side_weights]
    assert all(w.shape[0] % steps == 0 and r % BF16_SUBLANES == 0
               for w, r in zip(side_weights, side_rows))
    side_specs = [pl.BlockSpec((r, w.shape[1]), lambda bi, i: (bi * n_i + i, 0))
                  for w, r in zip(side_weights, side_rows)]
    outs = pl.pallas_call(
        functools.partial(_mixer_in_kernel, w_conv=w_conv, w_attn=w_attn,
                          n_side=len(side_weights)),
        grid=(b, n_i),
        in_specs=[
            pl.BlockSpec((1, tm, d), tile),
            pl.BlockSpec((1, d), const),
            pl.BlockSpec(w_in.shape, const, pipeline_mode=pl.Buffered(1)),
            pl.BlockSpec((CONV_K, w_conv), const),
            pl.BlockSpec((1, w_conv), const),
            pl.BlockSpec(gmat.shape, const),
        ] + side_specs,
        out_specs=[
            pl.BlockSpec((1, tm, w_conv), tile),
            pl.BlockSpec((1, tm, w_attn), tile),
            pl.BlockSpec((1, tm, w_attn), tile),
            pl.BlockSpec((1, tm, w_attn), tile),
        ] + side_specs,
        out_shape=[out_sd(w_conv), out_sd(w_attn), out_sd(w_attn), out_sd(w_attn)]
        + [jax.ShapeDtypeStruct(w.shape, BF16) for w in side_weights],
        scratch_shapes=[pltpu.VMEM((tm + SUBLANES, w_conv), F32),
                        pltpu.VMEM(w_in.shape, BF16)],
        compiler_params=pltpu.CompilerParams(
            dimension_semantics=("arbitrary", "arbitrary"),
            vmem_limit_bytes=VMEM_LIMIT_BYTES),
        name="mixer_in",
    )(x, g_mix, w_in, conv_w, g_conv_out, gmat, *side_weights)
    return outs[:4], outs[4:]


def _attn_kernel(q_ref, k_ref, v_ref, tri_ref, bias_a_ref, bias_bc_ref, g_ref, gmat_ref,
                 o_ref, qst_ref, acc_ref, carry_ref):
    nb, tq = q_ref.shape[0], TQ
    tiles = q_ref.shape[1] // tq
    tk = tri_ref.shape[0]
    chains, rows, gw = qst_ref.shape
    groups = chains // (tiles * nb)
    pair = 2 * HEAD_DIM
    where = [(c // (nb * groups), (c // groups) % nb, (c % groups) * gw) for c in range(chains)]
    tile_chains = [[c for c in range(chains) if where[c][0] == t] for t in range(tiles)]
    tile_i = [tiles * pl.program_id(1) + t for t in range(tiles)]
    segs = []
    for q0, qn in ((0, tq // 2), (tq // 2, tq // 4), (3 * tq // 4, tq // 4)):
        segs.append((q0, qn, slice(GROUP_HEADS * q0, GROUP_HEADS * (q0 + qn))))
    (_, _, ra), (_, _, rb), (_, _, rc) = segs
    rab, rbc, full = slice(0, rb.stop), slice(rb.start, rows), slice(0, rows)

    for c, (t, bl, l0) in enumerate(where):
        for q0, qn, rs in segs:
            q = q_ref[bl, t * tq + q0:t * tq + q0 + qn, l0:l0 + gw]
            lane = lax.broadcasted_iota(jnp.int32, (qn, gw), 1)
            for hh in range(GROUP_HEADS):
                in_head = (lane >= hh * HEAD_DIM) & (lane < (hh + 1) * HEAD_DIM)
                r0 = rs.start + hh * qn
                qst_ref[c, r0:r0 + qn, :] = jnp.where(in_head, q, jnp.zeros_like(q))

    def scores(c, rs, start, nk):
        _, bl, l0 = where[c]
        kj = k_ref[bl, pl.ds(start, nk), l0:l0 + gw]
        return lax.dot_general(qst_ref[c, rs, :], kj, (((1,), (1,)), ((), ())),
                               preferred_element_type=F32)

    def weights(z, carry):
        nk = z.shape[1]
        sp = jnp.maximum(z, 0.0) + jnp.log(1.0 + jnp.exp2(-jnp.abs(z))) * LOG2_E
        cs = jnp.dot(sp.astype(BF16), tri_ref[0:nk, 0:nk],
                     preferred_element_type=F32)
        total = cs[:, 0:1] + sp[:, 0:1]
        if carry is None:
            return jnp.exp2(z - (sp + cs)).astype(BF16), total
        return jnp.exp2(z - (sp + cs + carry)).astype(BF16), carry + total

    def accumulate(c, rs, ab, start, assign=False):
        _, bl, l0 = where[c]
        nk = ab.shape[1]
        for _, qn, seg in segs:
            if seg.start < rs.start or seg.stop > rs.stop:
                continue
            half = (seg.stop - seg.start) // 2
            for p in range(GROUP_HEADS // 2):
                vj = v_ref[bl, pl.ds(start, nk), l0 + p * pair:l0 + (p + 1) * pair]
                src = seg.start - rs.start + p * half
                av = jnp.dot(ab[src:src + half, :], vj, preferred_element_type=F32)
                dst = slice(seg.start + p * half, seg.start + (p + 1) * half)
                if assign:
                    acc_ref[c, dst, :] = av
                else:
                    acc_ref[c, dst, :] += av

    d0 = [pl.multiple_of(i * tk, tk) for i in tile_i]
    p0 = [pl.multiple_of(jnp.maximum(i - 1, 0) * tk, tk) for i in tile_i]
    no_prev = [jnp.where(i == 0, -MASKED_SCORE, 0.0).astype(F32) for i in tile_i]
    tile_of = [where[c][0] for c in range(chains)]
    z_da = [scores(c, ra, d0[tile_of[c]], tk // 2) + bias_a_ref[...] for c in range(chains)]
    z_dbc = [scores(c, rbc, d0[tile_of[c]], tk) + bias_bc_ref[...] for c in range(chains)]
    z_pab = [scores(c, rab, p0[tile_of[c]], tk) for c in range(chains)]
    carry_ab = []
    low_c = []
    nb_rows = rb.stop - rb.start
    for c in range(chains):
        t = tile_of[c]
        ab, cr_a = weights(z_da[c], None)
        accumulate(c, ra, ab, d0[t], assign=True)
        ab, cr_bc = weights(z_dbc[c], None)
        accumulate(c, rbc, ab, d0[t], assign=True)
        cr_bc = cr_bc + no_prev[t]
        carry_ab.append(jnp.concatenate([cr_a + no_prev[t], cr_bc[0:nb_rows]], axis=0))
        carry_ref[c, rc, :] = cr_bc[nb_rows:]
        low_c.append(jnp.min(cr_bc[nb_rows:]))
    for c in range(chains):
        ab, cr = weights(z_pab[c], carry_ab[c])
        accumulate(c, rab, ab, p0[tile_of[c]])
        carry_ref[c, rab, :] = cr

    @pl.when(functools.reduce(jnp.minimum, low_c) < UNDERFLOW_BITS)
    def _():
        for c in range(chains):
            start = p0[tile_of[c]]
            ab, cr = weights(scores(c, rc, start, tk), carry_ref[c, rc, :])
            accumulate(c, rc, ab, start)
            carry_ref[c, rc, :] = cr

    for t in range(tiles):
        i = tile_i[t]

        def lowest_carry():
            return functools.reduce(jnp.minimum,
                                    [jnp.min(carry_ref[c]) for c in tile_chains[t]])

        def body(state):
            n, _ = state
            start = pl.multiple_of((i - 1 - n) * tk, tk)
            for c in tile_chains[t]:
                ab, cr = weights(scores(c, full, start, tk), carry_ref[c])
                accumulate(c, full, ab, start)
                carry_ref[c] = cr
            return n + 1, lowest_carry()

        def more(state):
            n, low = state
            return (n < i) & (low < UNDERFLOW_BITS)

        lax.while_loop(more, body, (jnp.int32(1), lowest_carry()))

    for t in range(tiles):
        for bl in range(nb):
            for q0, qn, rs in segs:
                lane_p = lax.broadcasted_iota(jnp.int32, (qn, pair), 1)
                def head_rows(c, hh):
                    r0 = rs.start + hh * qn
                    return acc_ref[c, r0:r0 + qn, :]
                group_chains = [c for c in tile_chains[t] if where[c][1] == bl]
                out = jnp.concatenate(
                    [jnp.where(lane_p < HEAD_DIM, head_rows(c, 2 * p), head_rows(c, 2 * p + 1))
                     for c in group_chains for p in range(GROUP_HEADS // 2)], axis=1)
                ms = _group_mean(out * out, gmat_ref[...])
                o_ref[bl, t * tq + q0:t * tq + q0 + qn, :] = (
                    out * lax.rsqrt(ms + EPS) * g_ref[...]).astype(BF16)


def _attention(q, k, v, tri, bias_a, bias_bc, g_attn_out, gmat):
    b, s, w = q.shape
    nb = ATTN_BATCHES
    gw = GROUP_HEADS * HEAD_DIM
    tq = ATTN_TILES * TQ
    chains = ATTN_TILES * nb * (w // gw)
    rows = GROUP_HEADS * TQ
    const = lambda bi, i: (0, 0)
    return pl.pallas_call(
        _attn_kernel,
        grid=(b // nb, s // tq),
        in_specs=[
            pl.BlockSpec((nb, tq, w), lambda bi, i: (bi, i, 0)),
            pl.BlockSpec((nb, s, w), lambda bi, i: (bi, 0, 0), pipeline_mode=pl.Buffered(1)),
            pl.BlockSpec((nb, s, w), lambda bi, i: (bi, 0, 0), pipeline_mode=pl.Buffered(1)),
            pl.BlockSpec(tri.shape, const),
            pl.BlockSpec(bias_a.shape, const),
            pl.BlockSpec(bias_bc.shape, const),
            pl.BlockSpec((1, w), const),
            pl.BlockSpec(gmat.shape, const),
        ],
        out_specs=pl.BlockSpec((nb, tq, w), lambda bi, i: (bi, i, 0)),
        out_shape=jax.ShapeDtypeStruct((b, s, w), BF16),
        scratch_shapes=[pltpu.VMEM((chains, rows, gw), BF16),
                        pltpu.VMEM((chains, rows, 2 * HEAD_DIM), F32),
                        pltpu.VMEM((chains, rows, 1), F32)],
        compiler_params=pltpu.CompilerParams(
            dimension_semantics=("arbitrary", "arbitrary"),
            vmem_limit_bytes=ATTN_VMEM_LIMIT_BYTES),
        name="sb_attention",
    )(q, k, v, tri, bias_a, bias_bc, g_attn_out, gmat)


def _post_kernel(x_ref, conv_ref, attn_ref, p_ref, wo_ref, gmlp_ref, wup_ref, wdn_ref,
                 gple_ref, wg_ref, wp_ref, gfin_ref, o_ref, *, apply_final):
    w_conv = conv_ref.shape[1]
    subs = [slice(r0, r0 + POST_SUB_ROWS) for r0 in range(0, x_ref.shape[0], POST_SUB_ROWS)]
    dot = functools.partial(jnp.dot, preferred_element_type=F32)
    h = [x_ref[rs, :] + dot(conv_ref[rs, :], wo_ref[0:w_conv, :]) for rs in subs]
    h = [hs + dot(attn_ref[rs, :], wo_ref[w_conv:, :]) for hs, rs in zip(h, subs)]
    m = [(hs * _rms_scale(hs) * gmlp_ref[...]).astype(BF16) for hs in h]
    r = [jnp.square(jnp.maximum(dot(ms, wup_ref[...]), 0.0)).astype(BF16) for ms in m]
    h = [hs + dot(rr, wdn_ref[...]) for hs, rr in zip(h, r)]
    n = [(hs * _rms_scale(hs) * gple_ref[...]).astype(BF16) for hs in h]
    pp = [dot(p_ref[rs, :].astype(BF16), wp_ref[...]) for rs in subs]
    gate = [jax.nn.sigmoid(dot(ns, wg_ref[...])) for ns in n]
    h = [hs + gs * ps for hs, gs, ps in zip(h, gate, pp)]
    for hs, rs in zip(h, subs):
        if apply_final:
            hs = hs * _rms_scale(hs) * gfin_ref[...]
        o_ref[rs, :] = hs


def _post(x2, conv2, attn2, p2, w_out, g_mlp, w_up, w_down, g_ple, w_gate, w_proj, g_final,
          apply_final):
    t, d = x2.shape
    tm = TM_POST
    tile = lambda i: (i, 0)
    const = lambda i: (0, 0)
    wspec = lambda w: pl.BlockSpec(w.shape, const, pipeline_mode=pl.Buffered(1))
    gspec = pl.BlockSpec((1, d), const)
    return pl.pallas_call(
        functools.partial(_post_kernel, apply_final=apply_final),
        grid=(t // tm,),
        in_specs=[
            pl.BlockSpec((tm, d), tile),
            pl.BlockSpec((tm, conv2.shape[1]), tile),
            pl.BlockSpec((tm, attn2.shape[1]), tile),
            pl.BlockSpec((tm, p2.shape[1]), tile),
            wspec(w_out), gspec, wspec(w_up), wspec(w_down), gspec, wspec(w_gate), wspec(w_proj),
            gspec,
        ],
        out_specs=pl.BlockSpec((tm, d), tile),
        out_shape=jax.ShapeDtypeStruct((t, d), F32),
        compiler_params=pltpu.CompilerParams(
            dimension_semantics=("arbitrary",),
            vmem_limit_bytes=VMEM_LIMIT_BYTES),
        name="post",
    )(x2, conv2, attn2, p2, w_out, g_mlp, w_up, w_down, g_ple, w_gate, w_proj, g_final)


def _group_matrix():
    g = jnp.arange(MXU_TILE) // HEAD_DIM
    return jnp.where(g[:, None] == g[None, :], 1.0 / HEAD_DIM, 0.0).astype(BF16)


def kernel(x, p, g_mix, w_in, conv_w, g_conv_out, g_attn_out, w_out, g_mlp, w_up, w_down,
           g_ple, w_ple_gate, w_ple_proj, g_final):
    b, s, d = x.shape
    depth = p.shape[0]
    w_conv = conv_w.shape[-1]
    w_attn = g_attn_out.shape[-1]
    assert s % TM_IN == 0 and s % TQ == 0 and TQ == TK and (b * s) % TM_POST == 0
    assert w_attn % (GROUP_HEADS * HEAD_DIM) == 0 and w_conv % MXU_TILE == 0
    assert w_attn % MXU_TILE == 0 and MXU_TILE % HEAD_DIM == 0
    assert b % ATTN_BATCHES == 0 and s % (ATTN_TILES * TQ) == 0

    idx = jnp.arange(TK)
    tri = (idx[:, None] > idx[None, :]).astype(BF16)
    causal = jnp.where(idx[None, :] < idx[:, None], 0.0, MASKED_SCORE).astype(F32)
    per_head = lambda m: jnp.tile(m, (GROUP_HEADS, 1))
    bias_a = per_head(causal[:TQ // 2, :TK // 2])
    bias_bc = jnp.concatenate([per_head(causal[TQ // 2:3 * TQ // 4]),
                               per_head(causal[3 * TQ // 4:])], axis=0)
    gmat = _group_matrix()
    row = lambda g: g.reshape(1, -1)

    h = x
    for i in range(depth):
        (conv, q, k, v), (wo, wu, wd, wg, wp) = _mixer_in(
            h, row(g_mix[i]), w_in[i], conv_w[i], row(g_conv_out[i]), gmat, w_conv, w_attn,
            [w_out[i], w_up[i], w_down[i], w_ple_gate[i], w_ple_proj[i]])
        attn = _attention(q, k, v, tri, bias_a, bias_bc, row(g_attn_out[i]), gmat)
        h = _post(h.reshape(b * s, d), conv.reshape(b * s, w_conv), attn.reshape(b * s, w_attn),
                  p[i].reshape(b * s, -1), wo, row(g_mlp[i]), wu, wd, row(g_ple[i]), wg, wp,
                  row(g_final), apply_final=(i == depth - 1)).reshape(b, s, d)
    return h
```

```python
import functools

import jax
import jax.numpy as jnp
import numpy as np
from jax import lax
from jax.experimental import pallas as pl
from jax.experimental.pallas import tpu as pltpu

HEAD_DIM = 64
CONV_K = 3
EPS = 1e-6
LOG2_E = 1.4426950408889634
UNDERFLOW_BITS = 150.0
MASKED_SCORE = -1e30

F32 = jnp.float32
BF16 = jnp.bfloat16

V7X_VMEM_BYTES = 64 * 1024 * 1024
VMEM_LIMIT_BYTES = V7X_VMEM_BYTES - 8 * 1024 * 1024
ATTN_VMEM_LIMIT_BYTES = V7X_VMEM_BYTES - 2 * 1024 * 1024
SUBLANES = 8
BF16_SUBLANES = 16
MXU_TILE = 256

TM_IN = 1024
IN_SUB_ROWS = 256
TQ = 256
TK = 256
GROUP_HEADS = 4
ATTN_BATCHES = 2
ATTN_TILES = 2
TM_POST = 512
POST_SUB_ROWS = 256


def _rms_scale(x):
    return lax.rsqrt(jnp.mean(x * x, axis=-1, keepdims=True) + EPS)


def _group_mean(sq, gmat):
    sqb = sq.astype(BF16)
    return jnp.concatenate(
        [jnp.dot(sqb[:, l0:l0 + MXU_TILE], gmat, preferred_element_type=F32)
         for l0 in range(0, sq.shape[1], MXU_TILE)], axis=1)


def _mixer_in_kernel(x_ref, g_ref, w32_ref, cw_ref, gco_ref, gmat_ref, *rest, w_conv, w_attn,
                     n_side):
    side_in, rest = rest[:n_side], rest[n_side:]
    conv_ref, q_ref, k_ref, v_ref = rest[:4]
    side_out = rest[4:4 + n_side]
    vbuf, w_ref = rest[4 + n_side:]
    tm = x_ref.shape[1]

    @pl.when((pl.program_id(0) == 0) & (pl.program_id(1) == 0))
    def _():
        for r0 in range(0, w32_ref.shape[0], IN_SUB_ROWS):
            w_ref[r0:r0 + IN_SUB_ROWS, :] = w32_ref[r0:r0 + IN_SUB_ROWS, :].astype(BF16)

    for src, dst in zip(side_in, side_out):
        dst[...] = src[...].astype(BF16)

    @pl.when(pl.program_id(1) == 0)
    def _():
        vbuf[0:SUBLANES, :] = jnp.zeros((SUBLANES, w_conv), F32)

    starts = list(range(0, tm, IN_SUB_ROWS))
    a = []
    for r0 in starts:
        x = x_ref[0, r0:r0 + IN_SUB_ROWS, :]
        a.append((x * _rms_scale(x) * g_ref[...]).astype(BF16))
    proj = [jnp.dot(as_, w_ref[...], preferred_element_type=F32) for as_ in a]

    o = 3 * w_conv
    for r0, pr in zip(starts, proj):
        rs = slice(r0, r0 + IN_SUB_ROWS)
        vbuf[SUBLANES + r0:SUBLANES + r0 + IN_SUB_ROWS, :] = (
            pr[:, w_conv:2 * w_conv] * pr[:, 2 * w_conv:3 * w_conv])
        q_ref[0, rs, :] = (pr[:, o:o + w_attn] * (HEAD_DIM ** -0.5 * LOG2_E)).astype(BF16)
        k_ref[0, rs, :] = pr[:, o + w_attn:o + 2 * w_attn].astype(BF16)
        v_ref[0, rs, :] = pr[:, o + 2 * w_attn:o + 3 * w_attn].astype(BF16)

    cw = cw_ref[...]
    for r0, pr in zip(starts, proj):
        v0 = vbuf[SUBLANES + r0:SUBLANES + r0 + IN_SUB_ROWS, :]
        v1 = vbuf[SUBLANES - 1 + r0:SUBLANES - 1 + r0 + IN_SUB_ROWS, :]
        v2 = vbuf[SUBLANES - 2 + r0:SUBLANES - 2 + r0 + IN_SUB_ROWS, :]
        c = pr[:, 0:w_conv] * (cw[0:1, :] * v2 + cw[1:2, :] * v1 + cw[2:3, :] * v0)
        ms = _group_mean(c * c, gmat_ref[...])
        conv_ref[0, r0:r0 + IN_SUB_ROWS, :] = (
            c * lax.rsqrt(ms + EPS) * gco_ref[...]).astype(BF16)
    vbuf[0:SUBLANES, :] = vbuf[tm:tm + SUBLANES, :]


def _mixer_in(x, g_mix, w_in, conv_w, g_conv_out, gmat, w_conv, w_attn, side_weights):
    b, s, d = x.shape
    tm = TM_IN
    n_i = s // tm
    steps = b * n_i
    const = lambda *_: (0, 0)
    tile = lambda bi, i: (bi, i, 0)
    out_sd = lambda w: jax.ShapeDtypeStruct((b, s, w), BF16)
    side_rows = [w.shape[0] // steps for w in side_weights]
    assert all(w.shape[0] % steps == 0 and r % BF16_SUBLANES == 0
               for w, r in zip(side_weights, side_rows))
    side_specs = [pl.BlockSpec((r, w.shape[1]), lambda bi, i: (bi * n_i + i, 0))
                  for w, r in zip(side_weights, side_rows)]
    outs = pl.pallas_call(
        functools.partial(_mixer_in_kernel, w_conv=w_conv, w_attn=w_attn,
                          n_side=len(side_weights)),
        grid=(b, n_i),
        in_specs=[
            pl.BlockSpec((1, tm, d), tile),
            pl.BlockSpec((1, d), const),
            pl.BlockSpec(w_in.shape, const, pipeline_mode=pl.Buffered(1)),
            pl.BlockSpec((CONV_K, w_conv), const),
            pl.BlockSpec((1, w_conv), const),
            pl.BlockSpec(gmat.shape, const),
        ] + side_specs,
        out_specs=[
            pl.BlockSpec((1, tm, w_conv), tile),
            pl.BlockSpec((1, tm, w_attn), tile),
            pl.BlockSpec((1, tm, w_attn), tile),
            pl.BlockSpec((1, tm, w_attn), tile),
        ] + side_specs,
        out_shape=[out_sd(w_conv), out_sd(w_attn), out_sd(w_attn), out_sd(w_attn)]
        + [jax.ShapeDtypeStruct(w.shape, BF16) for w in side_weights],
        scratch_shapes=[pltpu.VMEM((tm + SUBLANES, w_conv), F32),
                        pltpu.VMEM(w_in.shape, BF16)],
        compiler_params=pltpu.CompilerParams(
            dimension_semantics=("arbitrary", "arbitrary"),
            vmem_limit_bytes=VMEM_LIMIT_BYTES),
        name="mixer_in",
    )(x, g_mix, w_in, conv_w, g_conv_out, gmat, *side_weights)
    return outs[:4], outs[4:]


def _attn_kernel(q_ref, k_ref, v_ref, tri_ref, bias_a_ref, bias_bc_ref, g_ref, gmat_ref,
                 o_ref, qst_ref, acc_ref, carry_ref):
    nb, tq = q_ref.shape[0], TQ
    tiles = q_ref.shape[1] // tq
    tk = tri_ref.shape[0]
    chains, rows, gw = qst_ref.shape
    groups = chains // (tiles * nb)
    pair = 2 * HEAD_DIM
    where = [(c // (nb * groups), (c // groups) % nb, (c % groups) * gw) for c in range(chains)]
    tile_chains = [[c for c in range(chains) if where[c][0] == t] for t in range(tiles)]
    tile_i = [tiles * pl.program_id(1) + t for t in range(tiles)]
    segs = []
    for q0, qn in ((0, tq // 2), (tq // 2, tq // 4), (3 * tq // 4, tq // 4)):
        segs.append((q0, qn, slice(GROUP_HEADS * q0, GROUP_HEADS * (q0 + qn))))
    (_, _, ra), (_, _, rb), (_, _, rc) = segs
    rab, rbc, full = slice(0, rb.stop), slice(rb.start, rows), slice(0, rows)

    for c, (t, bl, l0) in enumerate(where):
        for q0, qn, rs in segs:
            q = q_ref[bl, t * tq + q0:t * tq + q0 + qn, l0:l0 + gw]
            lane = lax.broadcasted_iota(jnp.int32, (qn, gw), 1)
            for hh in range(GROUP_HEADS):
                in_head = (lane >= hh * HEAD_DIM) & (lane < (hh + 1) * HEAD_DIM)
                r0 = rs.start + hh * qn
                qst_ref[c, r0:r0 + qn, :] = jnp.where(in_head, q, jnp.zeros_like(q))

    def scores(c, rs, start, nk):
        _, bl, l0 = where[c]
        kj = k_ref[bl, pl.ds(start, nk), l0:l0 + gw]
        return lax.dot_general(qst_ref[c, rs, :], kj, (((1,), (1,)), ((), ())),
                               preferred_element_type=F32)

    def weights(z, carry):
        nk = z.shape[1]
        sp = jnp.maximum(z, 0.0) + jnp.log(1.0 + jnp.exp2(-jnp.abs(z))) * LOG2_E
        cs = jnp.dot(sp.astype(BF16), tri_ref[0:nk, 0:nk],
                     preferred_element_type=F32)
        total = cs[:, 0:1] + sp[:, 0:1]
        if carry is None:
            return jnp.exp2(z - (sp + cs)).astype(BF16), total
        return jnp.exp2(z - (sp + cs + carry)).astype(BF16), carry + total

    def accumulate(c, rs, ab, start, assign=False):
        _, bl, l0 = where[c]
        nk = ab.shape[1]
        for _, qn, seg in segs:
            if seg.start < rs.start or seg.stop > rs.stop:
                continue
            half = (seg.stop - seg.start) // 2
            for p in range(GROUP_HEADS // 2):
                vj = v_ref[bl, pl.ds(start, nk), l0 + p * pair:l0 + (p + 1) * pair]
                src = seg.start - rs.start + p * half
                av = jnp.dot(ab[src:src + half, :], vj, preferred_element_type=F32)
                dst = slice(seg.start + p * half, seg.start + (p + 1) * half)
                if assign:
                    acc_ref[c, dst, :] = av
                else:
                    acc_ref[c, dst, :] += av

    d0 = [pl.multiple_of(i * tk, tk) for i in tile_i]
    p0 = [pl.multiple_of(jnp.maximum(i - 1, 0) * tk, tk) for i in tile_i]
    no_prev = [jnp.where(i == 0, -MASKED_SCORE, 0.0).astype(F32) for i in tile_i]
    tile_of = [where[c][0] for c in range(chains)]
    z_da = [scores(c, ra, d0[tile_of[c]], tk // 2) + bias_a_ref[...] for c in range(chains)]
    z_dbc = [scores(c, rbc, d0[tile_of[c]], tk) + bias_bc_ref[...] for c in range(chains)]
    z_pab = [scores(c, rab, p0[tile_of[c]], tk) for c in range(chains)]
    carry_ab = []
    low_c = []
    nb_rows = rb.stop - rb.start
    for c in range(chains):
        t = tile_of[c]
        ab, cr_a = weights(z_da[c], None)
        accumulate(c, ra, ab, d0[t], assign=True)
        ab, cr_bc = weights(z_dbc[c], None)
        accumulate(c, rbc, ab, d0[t], assign=True)
        cr_bc = cr_bc + no_prev[t]
        carry_ab.append(jnp.concatenate([cr_a + no_prev[t], cr_bc[0:nb_rows]], axis=0))
        carry_ref[c, rc, :] = cr_bc[nb_rows:]
        low_c.append(jnp.min(cr_bc[nb_rows:]))
    for c in range(chains):
        ab, cr = weights(z_pab[c], carry_ab[c])
        accumulate(c, rab, ab, p0[tile_of[c]])
        carry_ref[c, rab, :] = cr

    @pl.when(functools.reduce(jnp.minimum, low_c) < UNDERFLOW_BITS)
    def _():
        for c in range(chains):
            start = p0[tile_of[c]]
            ab, cr = weights(scores(c, rc, start, tk), carry_ref[c, rc, :])
            accumulate(c, rc, ab, start)
            carry_ref[c, rc, :] = cr

    for t in range(tiles):
        i = tile_i[t]

        def lowest_carry():
            return functools.reduce(jnp.minimum,
                                    [jnp.min(carry_ref[c]) for c in tile_chains[t]])

        def body(state):
            n, _ = state
            start = pl.multiple_of((i - 1 - n) * tk, tk)
            for c in tile_chains[t]:
                ab, cr = weights(scores(c, full, start, tk), carry_ref[c])
                accumulate(c, full, ab, start)
                carry_ref[c] = cr
            return n + 1, lowest_carry()

        def more(state):
            n, low = state
            return (n < i) & (low < UNDERFLOW_BITS)

        lax.while_loop(more, body, (jnp.int32(1), lowest_carry()))

    for t in range(tiles):
        for bl in range(nb):
            for q0, qn, rs in segs:
                lane_p = lax.broadcasted_iota(jnp.int32, (qn, pair), 1)
                def head_rows(c, hh):
                    r0 = rs.start + hh * qn
                    return acc_ref[c, r0:r0 + qn, :]
                group_chains = [c for c in tile_chains[t] if where[c][1] == bl]
                out = jnp.concatenate(
                    [jnp.where(lane_p < HEAD_DIM, head_rows(c, 2 * p), head_rows(c, 2 * p + 1))
                     for c in group_chains for p in range(GROUP_HEADS // 2)], axis=1)
                ms = _group_mean(out * out, gmat_ref[...])
                o_ref[bl, t * tq + q0:t * tq + q0 + qn, :] = (
                    out * lax.rsqrt(ms + EPS) * g_ref[...]).astype(BF16)


def _attention(q, k, v, tri, bias_a, bias_bc, g_attn_out, gmat):
    b, s, w = q.shape
    nb = ATTN_BATCHES
    gw = GROUP_HEADS * HEAD_DIM
    tq = ATTN_TILES * TQ
    chains = ATTN_TILES * nb * (w // gw)
    rows = GROUP_HEADS * TQ
    const = lambda bi, i: (0, 0)
    return pl.pallas_call(
        _attn_kernel,
        grid=(b // nb, s // tq),
        in_specs=[
            pl.BlockSpec((nb, tq, w), lambda bi, i: (bi, i, 0)),
            pl.BlockSpec((nb, s, w), lambda bi, i: (bi, 0, 0), pipeline_mode=pl.Buffered(1)),
            pl.BlockSpec((nb, s, w), lambda bi, i: (bi, 0, 0), pipeline_mode=pl.Buffered(1)),
            pl.BlockSpec(tri.shape, const),
            pl.BlockSpec(bias_a.shape, const),
            pl.BlockSpec(bias_bc.shape, const),
            pl.BlockSpec((1, w), const),
            pl.BlockSpec(gmat.shape, const),
        ],
        out_specs=pl.BlockSpec((nb, tq, w), lambda bi, i: (bi, i, 0)),
        out_shape=jax.ShapeDtypeStruct((b, s, w), BF16),
        scratch_shapes=[pltpu.VMEM((chains, rows, gw), BF16),
                        pltpu.VMEM((chains, rows, 2 * HEAD_DIM), F32),
                        pltpu.VMEM((chains, rows, 1), F32)],
        compiler_params=pltpu.CompilerParams(
            dimension_semantics=("arbitrary", "arbitrary"),
            vmem_limit_bytes=ATTN_VMEM_LIMIT_BYTES),
        name="sb_attention",
    )(q, k, v, tri, bias_a, bias_bc, g_attn_out, gmat)


def _post_kernel(x_ref, conv_ref, attn_ref, p_ref, wo_ref, gmlp_ref, wup_ref, wdn_ref,
                 gple_ref, wg_ref, wp_ref, gfin_ref, o_ref, *, apply_final):
    w_conv = conv_ref.shape[1]
    subs = [slice(r0, r0 + POST_SUB_ROWS) for r0 in range(0, x_ref.shape[0], POST_SUB_ROWS)]
    dot = functools.partial(jnp.dot, preferred_element_type=F32)
    h = [x_ref[rs, :] + dot(conv_ref[rs, :], wo_ref[0:w_conv, :]) for rs in subs]
    h = [hs + dot(attn_ref[rs, :], wo_ref[w_conv:, :]) for hs, rs in zip(h, subs)]
    m = [(hs * _rms_scale(hs) * gmlp_ref[...]).astype(BF16) for hs in h]
    r = [jnp.square(jnp.maximum(dot(ms, wup_ref[...]), 0.0)).astype(BF16) for ms in m]
    h = [hs + dot(rr, wdn_ref[...]) for hs, rr in zip(h, r)]
    n = [(hs * _rms_scale(hs) * gple_ref[...]).astype(BF16) for hs in h]
    pp = [dot(p_ref[rs, :].astype(BF16), wp_ref[...]) for rs in subs]
    gate = [jax.nn.sigmoid(dot(ns, wg_ref[...])) for ns in n]
    h = [hs + gs * ps for hs, gs, ps in zip(h, gate, pp)]
    for hs, rs in zip(h, subs):
        if apply_final:
            hs = hs * _rms_scale(hs) * gfin_ref[...]
        o_ref[rs, :] = hs


def _post(x2, conv2, attn2, p2, w_out, g_mlp, w_up, w_down, g_ple, w_gate, w_proj, g_final,
          apply_final):
    t, d = x2.shape
    tm = TM_POST
    tile = lambda i: (i, 0)
    const = lambda i: (0, 0)
    wspec = lambda w: pl.BlockSpec(w.shape, const, pipeline_mode=pl.Buffered(1))
    gspec = pl.BlockSpec((1, d), const)
    return pl.pallas_call(
        functools.partial(_post_kernel, apply_final=apply_final),
        grid=(t // tm,),
        in_specs=[
            pl.BlockSpec((tm, d), tile),
            pl.BlockSpec((tm, conv2.shape[1]), tile),
            pl.BlockSpec((tm, attn2.shape[1]), tile),
            pl.BlockSpec((tm, p2.shape[1]), tile),
            wspec(w_out), gspec, wspec(w_up), wspec(w_down), gspec, wspec(w_gate), wspec(w_proj),
            gspec,
        ],
        out_specs=pl.BlockSpec((tm, d), tile),
        out_shape=jax.ShapeDtypeStruct((t, d), F32),
        compiler_params=pltpu.CompilerParams(
            dimension_semantics=("arbitrary",),
            vmem_limit_bytes=VMEM_LIMIT_BYTES),
        name="post",
    )(x2, conv2, attn2, p2, w_out, g_mlp, w_up, w_down, g_ple, w_gate, w_proj, g_final)


def _group_matrix():
    g = np.arange(MXU_TILE) // HEAD_DIM
    return jnp.asarray(np.where(g[:, None] == g[None, :], 1.0 / HEAD_DIM, 0.0), BF16)


def kernel(x, p, g_mix, w_in, conv_w, g_conv_out, g_attn_out, w_out, g_mlp, w_up, w_down,
           g_ple, w_ple_gate, w_ple_proj, g_final):
    b, s, d = x.shape
    depth = p.shape[0]
    w_conv = conv_w.shape[-1]
    w_attn = g_attn_out.shape[-1]
    assert s % TM_IN == 0 and s % TQ == 0 and TQ == TK and (b * s) % TM_POST == 0
    assert w_attn % (GROUP_HEADS * HEAD_DIM) == 0 and w_conv % MXU_TILE == 0
    assert w_attn % MXU_TILE == 0 and MXU_TILE % HEAD_DIM == 0
    assert b % ATTN_BATCHES == 0 and s % (ATTN_TILES * TQ) == 0

    idx = np.arange(TK)
    tri = jnp.asarray(idx[:, None] > idx[None, :], BF16)
    causal = np.where(idx[None, :] < idx[:, None], 0.0, MASKED_SCORE).astype(np.float32)
    per_head = lambda m: np.tile(m, (GROUP_HEADS, 1))
    bias_a = jnp.asarray(per_head(causal[:TQ // 2, :TK // 2]))
    bias_bc = jnp.asarray(np.concatenate([per_head(causal[TQ // 2:3 * TQ // 4]),
                                          per_head(causal[3 * TQ // 4:])], axis=0))
    gmat = _group_matrix()
    row = lambda g: g.reshape(1, -1)

    h = x
    for i in range(depth):
        (conv, q, k, v), (wo, wu, wd, wg, wp) = _mixer_in(
            h, row(g_mix[i]), w_in[i], conv_w[i], row(g_conv_out[i]), gmat, w_conv, w_attn,
            [w_out[i], w_up[i], w_down[i], w_ple_gate[i], w_ple_proj[i]])
        attn = _attention(q, k, v, tri, bias_a, bias_bc, row(g_attn_out[i]), gmat)
        h = _post(h.reshape(b * s, d), conv.reshape(b * s, w_conv), attn.reshape(b * s, w_attn),
                  p[i].reshape(b * s, -1), wo, row(g_mlp[i]), wu, wd, row(g_ple[i]), wg, wp,
                  row(g_final), apply_final=(i == depth - 1)).reshape(b, s, d)
    return h
```

```python
import functools

import jax
import jax.numpy as jnp
import numpy as np
from jax import lax
from jax.experimental import pallas as pl
from jax.experimental.pallas import tpu as pltpu

HEAD_DIM = 64
CONV_K = 3
EPS = 1e-6
LOG2_E = 1.4426950408889634
UNDERFLOW_BITS = 150.0
MASKED_SCORE = -1e30

F32 = jnp.float32
BF16 = jnp.bfloat16

V7X_VMEM_BYTES = 64 * 1024 * 1024
VMEM_LIMIT_BYTES = V7X_VMEM_BYTES - 8 * 1024 * 1024
ATTN_VMEM_LIMIT_BYTES = V7X_VMEM_BYTES - 2 * 1024 * 1024
SUBLANES = 8
BF16_SUBLANES = 16
MXU_TILE = 256

TM_IN = 1024
IN_SUB_ROWS = 256
TQ = 256
TK = 256
GROUP_HEADS = 4
ATTN_BATCHES = 2
ATTN_TILES = 2
TM_POST = 512
POST_SUB_ROWS = 256


def _rms_scale(x):
    return lax.rsqrt(jnp.mean(x * x, axis=-1, keepdims=True) + EPS)


def _group_mean(sq, gmat):
    sqb = sq.astype(BF16)
    return jnp.concatenate(
        [jnp.dot(sqb[:, l0:l0 + MXU_TILE], gmat, preferred_element_type=F32)
         for l0 in range(0, sq.shape[1], MXU_TILE)], axis=1)


def _mixer_in_kernel(x_ref, g_ref, w32_ref, cw_ref, gco_ref, gmat_ref, *rest, w_conv, w_attn,
                     n_side):
    side_in, rest = rest[:n_side], rest[n_side:]
    conv_ref, q_ref, k_ref, v_ref = rest[:4]
    side_out = rest[4:4 + n_side]
    vbuf, w_ref = rest[4 + n_side:]
    tm = x_ref.shape[1]

    @pl.when((pl.program_id(0) == 0) & (pl.program_id(1) == 0))
    def _():
        for r0 in range(0, w32_ref.shape[0], IN_SUB_ROWS):
            w_ref[r0:r0 + IN_SUB_ROWS, :] = w32_ref[r0:r0 + IN_SUB_ROWS, :].astype(BF16)

    for src, dst in zip(side_in, side_out):
        dst[...] = src[...].astype(BF16)

    @pl.when(pl.program_id(1) == 0)
    def _():
        vbuf[0:SUBLANES, :] = jnp.zeros((SUBLANES, w_conv), F32)

    starts = list(range(0, tm, IN_SUB_ROWS))
    a = []
    for r0 in starts:
        x = x_ref[0, r0:r0 + IN_SUB_ROWS, :]
        a.append((x * _rms_scale(x) * g_ref[...]).astype(BF16))
    proj = [jnp.dot(as_, w_ref[...], preferred_element_type=F32) for as_ in a]

    o = 3 * w_conv
    for r0, pr in zip(starts, proj):
        rs = slice(r0, r0 + IN_SUB_ROWS)
        vbuf[SUBLANES + r0:SUBLANES + r0 + IN_SUB_ROWS, :] = (
            pr[:, w_conv:2 * w_conv] * pr[:, 2 * w_conv:3 * w_conv])
        q_ref[0, rs, :] = (pr[:, o:o + w_attn] * (HEAD_DIM ** -0.5 * LOG2_E)).astype(BF16)
        k_ref[0, rs, :] = pr[:, o + w_attn:o + 2 * w_attn].astype(BF16)
        v_ref[0, rs, :] = pr[:, o + 2 * w_attn:o + 3 * w_attn].astype(BF16)

    cw = cw_ref[...]
    for r0, pr in zip(starts, proj):
        v0 = vbuf[SUBLANES + r0:SUBLANES + r0 + IN_SUB_ROWS, :]
        v1 = vbuf[SUBLANES - 1 + r0:SUBLANES - 1 + r0 + IN_SUB_ROWS, :]
        v2 = vbuf[SUBLANES - 2 + r0:SUBLANES - 2 + r0 + IN_SUB_ROWS, :]
        c = pr[:, 0:w_conv] * (cw[0:1, :] * v2 + cw[1:2, :] * v1 + cw[2:3, :] * v0)
        ms = _group_mean(c * c, gmat_ref[...])
        conv_ref[0, r0:r0 + IN_SUB_ROWS, :] = (
            c * lax.rsqrt(ms + EPS) * gco_ref[...]).astype(BF16)
    vbuf[0:SUBLANES, :] = vbuf[tm:tm + SUBLANES, :]


def _mixer_in(x, g_mix, w_in, conv_w, g_conv_out, gmat, w_conv, w_attn, side_weights):
    b, s, d = x.shape
    tm = TM_IN
    n_i = s // tm
    steps = b * n_i
    const = lambda *_: (0, 0)
    tile = lambda bi, i: (bi, i, 0)
    out_sd = lambda w: jax.ShapeDtypeStruct((b, s, w), BF16)
    side_rows = [w.shape[0] // steps for w in side_weights]
    assert all(w.shape[0] % steps == 0 and r % BF16_SUBLANES == 0
               for w, r in zip(side_weights, side_rows))
    side_specs = [pl.BlockSpec((r, w.shape[1]), lambda bi, i: (bi * n_i + i, 0))
                  for w, r in zip(side_weights, side_rows)]
    outs = pl.pallas_call(
        functools.partial(_mixer_in_kernel, w_conv=w_conv, w_attn=w_attn,
                          n_side=len(side_weights)),
        grid=(b, n_i),
        in_specs=[
            pl.BlockSpec((1, tm, d), tile),
            pl.BlockSpec((1, d), const),
            pl.BlockSpec(w_in.shape, const, pipeline_mode=pl.Buffered(1)),
            pl.BlockSpec((CONV_K, w_conv), const),
            pl.BlockSpec((1, w_conv), const),
            pl.BlockSpec(gmat.shape, const),
        ] + side_specs,
        out_specs=[
            pl.BlockSpec((1, tm, w_conv), tile),
            pl.BlockSpec((1, tm, w_attn), tile),
            pl.BlockSpec((1, tm, w_attn), tile),
            pl.BlockSpec((1, tm, w_attn), tile),
        ] + side_specs,
        out_shape=[out_sd(w_conv), out_sd(w_attn), out_sd(w_attn), out_sd(w_attn)]
        + [jax.ShapeDtypeStruct(w.shape, BF16) for w in side_weights],
        scratch_shapes=[pltpu.VMEM((tm + SUBLANES, w_conv), F32),
                        pltpu.VMEM(w_in.shape, BF16)],
        compiler_params=pltpu.CompilerParams(
            dimension_semantics=("arbitrary", "arbitrary"),
            vmem_limit_bytes=VMEM_LIMIT_BYTES),
        name="mixer_in",
    )(x, g_mix, w_in, conv_w, g_conv_out, gmat, *side_weights)
    return outs[:4], outs[4:]


def _attn_kernel(q_ref, k_hbm, v_hbm, tri_ref, bias_a_ref, bias_bc_ref, g_ref, gmat_ref,
                 o_ref, qst_ref, acc_ref, carry_ref, k_ref, v_ref, kv_sem):
    nb, tq = q_ref.shape[0], TQ
    tiles = q_ref.shape[1] // tq
    tk = tri_ref.shape[0]
    chains, rows, gw = qst_ref.shape
    groups = chains // (tiles * nb)
    pair = 2 * HEAD_DIM
    where = [(c // (nb * groups), (c // groups) % nb, (c % groups) * gw) for c in range(chains)]
    tile_chains = [[c for c in range(chains) if where[c][0] == t] for t in range(tiles)]
    tile_i = [tiles * pl.program_id(1) + t for t in range(tiles)]
    segs = []
    for q0, qn in ((0, tq // 2), (tq // 2, tq // 4), (3 * tq // 4, tq // 4)):
        segs.append((q0, qn, slice(GROUP_HEADS * q0, GROUP_HEADS * (q0 + qn))))
    (_, _, ra), (_, _, rb), (_, _, rc) = segs
    rab, rbc, full = slice(0, rb.stop), slice(rb.start, rows), slice(0, rows)

    step, last_step = pl.program_id(1), pl.num_programs(1) - 1
    step_keys = tiles * tq

    def kv_copies(j):
        batches = pl.ds(pl.program_id(0) * nb, nb)
        keys = pl.ds(pl.multiple_of(j * step_keys, step_keys), step_keys)
        return [pltpu.make_async_copy(src.at[batches, keys, :], dst.at[:, keys, :], kv_sem.at[n])
                for n, (src, dst) in enumerate(((k_hbm, k_ref), (v_hbm, v_ref)))]

    @pl.when(step == 0)
    def _():
        for copy in kv_copies(0):
            copy.start()
        for copy in kv_copies(0):
            copy.wait()

    @pl.when(step < last_step)
    def _():
        for copy in kv_copies(step + 1):
            copy.start()

    for c, (t, bl, l0) in enumerate(where):
        for q0, qn, rs in segs:
            q = q_ref[bl, t * tq + q0:t * tq + q0 + qn, l0:l0 + gw]
            lane = lax.broadcasted_iota(jnp.int32, (qn, gw), 1)
            for hh in range(GROUP_HEADS):
                in_head = (lane >= hh * HEAD_DIM) & (lane < (hh + 1) * HEAD_DIM)
                r0 = rs.start + hh * qn
                qst_ref[c, r0:r0 + qn, :] = jnp.where(in_head, q, jnp.zeros_like(q))

    def scores(c, rs, start, nk):
        _, bl, l0 = where[c]
        kj = k_ref[bl, pl.ds(start, nk), l0:l0 + gw]
        return lax.dot_general(qst_ref[c, rs, :], kj, (((1,), (1,)), ((), ())),
                               preferred_element_type=F32)

    def weights(z, carry):
        nk = z.shape[1]
        sp = jnp.maximum(z, 0.0) + jnp.log(1.0 + jnp.exp2(-jnp.abs(z))) * LOG2_E
        cs = jnp.dot(sp.astype(BF16), tri_ref[0:nk, 0:nk],
                     preferred_element_type=F32)
        total = cs[:, 0:1] + sp[:, 0:1]
        if carry is None:
            return jnp.exp2(z - (sp + cs)).astype(BF16), total
        return jnp.exp2(z - (sp + cs + carry)).astype(BF16), carry + total

    def accumulate(c, rs, ab, start, assign=False):
        _, bl, l0 = where[c]
        nk = ab.shape[1]
        for _, qn, seg in segs:
            if seg.start < rs.start or seg.stop > rs.stop:
                continue
            half = (seg.stop - seg.start) // 2
            for p in range(GROUP_HEADS // 2):
                vj = v_ref[bl, pl.ds(start, nk), l0 + p * pair:l0 + (p + 1) * pair]
                src = seg.start - rs.start + p * half
                av = jnp.dot(ab[src:src + half, :], vj, preferred_element_type=F32)
                dst = slice(seg.start + p * half, seg.start + (p + 1) * half)
                if assign:
                    acc_ref[c, dst, :] = av
                else:
                    acc_ref[c, dst, :] += av

    d0 = [pl.multiple_of(i * tk, tk) for i in tile_i]
    p0 = [pl.multiple_of(jnp.maximum(i - 1, 0) * tk, tk) for i in tile_i]
    no_prev = [jnp.where(i == 0, -MASKED_SCORE, 0.0).astype(F32) for i in tile_i]
    tile_of = [where[c][0] for c in range(chains)]
    z_da = [scores(c, ra, d0[tile_of[c]], tk // 2) + bias_a_ref[...] for c in range(chains)]
    z_dbc = [scores(c, rbc, d0[tile_of[c]], tk) + bias_bc_ref[...] for c in range(chains)]
    z_pab = [scores(c, rab, p0[tile_of[c]], tk) for c in range(chains)]
    carry_ab = []
    low_c = []
    nb_rows = rb.stop - rb.start
    for c in range(chains):
        t = tile_of[c]
        ab, cr_a = weights(z_da[c], None)
        accumulate(c, ra, ab, d0[t], assign=True)
        ab, cr_bc = weights(z_dbc[c], None)
        accumulate(c, rbc, ab, d0[t], assign=True)
        cr_bc = cr_bc + no_prev[t]
        carry_ab.append(jnp.concatenate([cr_a + no_prev[t], cr_bc[0:nb_rows]], axis=0))
        carry_ref[c, rc, :] = cr_bc[nb_rows:]
        low_c.append(jnp.min(cr_bc[nb_rows:]))
    for c in range(chains):
        ab, cr = weights(z_pab[c], carry_ab[c])
        accumulate(c, rab, ab, p0[tile_of[c]])
        carry_ref[c, rab, :] = cr

    @pl.when(functools.reduce(jnp.minimum, low_c) < UNDERFLOW_BITS)
    def _():
        for c in range(chains):
            start = p0[tile_of[c]]
            ab, cr = weights(scores(c, rc, start, tk), carry_ref[c, rc, :])
            accumulate(c, rc, ab, start)
            carry_ref[c, rc, :] = cr

    for t in range(tiles):
        i = tile_i[t]

        def lowest_carry():
            return functools.reduce(jnp.minimum,
                                    [jnp.min(carry_ref[c]) for c in tile_chains[t]])

        def body(state):
            n, _ = state
            start = pl.multiple_of((i - 1 - n) * tk, tk)
            for c in tile_chains[t]:
                ab, cr = weights(scores(c, full, start, tk), carry_ref[c])
                accumulate(c, full, ab, start)
                carry_ref[c] = cr
            return n + 1, lowest_carry()

        def more(state):
            n, low = state
            return (n < i) & (low < UNDERFLOW_BITS)

        lax.while_loop(more, body, (jnp.int32(1), lowest_carry()))

    for t in range(tiles):
        for bl in range(nb):
            for q0, qn, rs in segs:
                lane_p = lax.broadcasted_iota(jnp.int32, (qn, pair), 1)
                def head_rows(c, hh):
                    r0 = rs.start + hh * qn
                    return acc_ref[c, r0:r0 + qn, :]
                group_chains = [c for c in tile_chains[t] if where[c][1] == bl]
                out = jnp.concatenate(
                    [jnp.where(lane_p < HEAD_DIM, head_rows(c, 2 * p), head_rows(c, 2 * p + 1))
                     for c in group_chains for p in range(GROUP_HEADS // 2)], axis=1)
                ms = _group_mean(out * out, gmat_ref[...])
                o_ref[bl, t * tq + q0:t * tq + q0 + qn, :] = (
                    out * lax.rsqrt(ms + EPS) * g_ref[...]).astype(BF16)

    @pl.when(step < last_step)
    def _():
        for copy in kv_copies(step + 1):
            copy.wait()


def _attention(q, k, v, tri, bias_a, bias_bc, g_attn_out, gmat):
    b, s, w = q.shape
    nb = ATTN_BATCHES
    gw = GROUP_HEADS * HEAD_DIM
    tq = ATTN_TILES * TQ
    chains = ATTN_TILES * nb * (w // gw)
    rows = GROUP_HEADS * TQ
    const = lambda bi, i: (0, 0)
    return pl.pallas_call(
        _attn_kernel,
        grid=(b // nb, s // tq),
        in_specs=[
            pl.BlockSpec((nb, tq, w), lambda bi, i: (bi, i, 0)),
            pl.BlockSpec(memory_space=pl.ANY),
            pl.BlockSpec(memory_space=pl.ANY),
            pl.BlockSpec(tri.shape, const),
            pl.BlockSpec(bias_a.shape, const),
            pl.BlockSpec(bias_bc.shape, const),
            pl.BlockSpec((1, w), const),
            pl.BlockSpec(gmat.shape, const),
        ],
        out_specs=pl.BlockSpec((nb, tq, w), lambda bi, i: (bi, i, 0)),
        out_shape=jax.ShapeDtypeStruct((b, s, w), BF16),
        scratch_shapes=[pltpu.VMEM((chains, rows, gw), BF16),
                        pltpu.VMEM((chains, rows, 2 * HEAD_DIM), F32),
                        pltpu.VMEM((chains, rows, 1), F32),
                        pltpu.VMEM((nb, s, w), BF16),
                        pltpu.VMEM((nb, s, w), BF16),
                        pltpu.SemaphoreType.DMA((2,))],
        compiler_params=pltpu.CompilerParams(
            dimension_semantics=("arbitrary", "arbitrary"),
            vmem_limit_bytes=ATTN_VMEM_LIMIT_BYTES),
        name="sb_attention",
    )(q, k, v, tri, bias_a, bias_bc, g_attn_out, gmat)


def _post_kernel(x_ref, conv_ref, attn_ref, p_ref, wo_ref, gmlp_ref, wup_ref, wdn_ref,
                 gple_ref, wg_ref, wp_ref, gfin_ref, o_ref, *, apply_final):
    w_conv = conv_ref.shape[1]
    subs = [slice(r0, r0 + POST_SUB_ROWS) for r0 in range(0, x_ref.shape[0], POST_SUB_ROWS)]
    dot = functools.partial(jnp.dot, preferred_element_type=F32)
    h = [x_ref[rs, :] + dot(conv_ref[rs, :], wo_ref[0:w_conv, :]) for rs in subs]
    h = [hs + dot(attn_ref[rs, :], wo_ref[w_conv:, :]) for hs, rs in zip(h, subs)]
    m = [(hs * _rms_scale(hs) * gmlp_ref[...]).astype(BF16) for hs in h]
    r = [jnp.square(jnp.maximum(dot(ms, wup_ref[...]), 0.0)).astype(BF16) for ms in m]
    h = [hs + dot(rr, wdn_ref[...]) for hs, rr in zip(h, r)]
    n = [(hs * _rms_scale(hs) * gple_ref[...]).astype(BF16) for hs in h]
    pp = [dot(p_ref[rs, :].astype(BF16), wp_ref[...]) for rs in subs]
    gate = [jax.nn.sigmoid(dot(ns, wg_ref[...])) for ns in n]
    h = [hs + gs * ps for hs, gs, ps in zip(h, gate, pp)]
    for hs, rs in zip(h, subs):
        if apply_final:
            hs = hs * _rms_scale(hs) * gfin_ref[...]
        o_ref[rs, :] = hs


def _post(x2, conv2, attn2, p2, w_out, g_mlp, w_up, w_down, g_ple, w_gate, w_proj, g_final,
          apply_final):
    t, d = x2.shape
    tm = TM_POST
    tile = lambda i: (i, 0)
    const = lambda i: (0, 0)
    wspec = lambda w: pl.BlockSpec(w.shape, const, pipeline_mode=pl.Buffered(1))
    gspec = pl.BlockSpec((1, d), const)
    return pl.pallas_call(
        functools.partial(_post_kernel, apply_final=apply_final),
        grid=(t // tm,),
        in_specs=[
            pl.BlockSpec((tm, d), tile),
            pl.BlockSpec((tm, conv2.shape[1]), tile),
            pl.BlockSpec((tm, attn2.shape[1]), tile),
            pl.BlockSpec((tm, p2.shape[1]), tile),
            wspec(w_out), gspec, wspec(w_up), wspec(w_down), gspec, wspec(w_gate), wspec(w_proj),
            gspec,
        ],
        out_specs=pl.BlockSpec((tm, d), tile),
        out_shape=jax.ShapeDtypeStruct((t, d), F32),
        compiler_params=pltpu.CompilerParams(
            dimension_semantics=("arbitrary",),
            vmem_limit_bytes=VMEM_LIMIT_BYTES),
        name="post",
    )(x2, conv2, attn2, p2, w_out, g_mlp, w_up, w_down, g_ple, w_gate, w_proj, g_final)


def _group_matrix():
    g = np.arange(MXU_TILE) // HEAD_DIM
    return jnp.asarray(np.where(g[:, None] == g[None, :], 1.0 / HEAD_DIM, 0.0), BF16)


def kernel(x, p, g_mix, w_in, conv_w, g_conv_out, g_attn_out, w_out, g_mlp, w_up, w_down,
           g_ple, w_ple_gate, w_ple_proj, g_final):
    b, s, d = x.shape
    depth = p.shape[0]
    w_conv = conv_w.shape[-1]
    w_attn = g_attn_out.shape[-1]
    assert s % TM_IN == 0 and s % TQ == 0 and TQ == TK and (b * s) % TM_POST == 0
    assert w_attn % (GROUP_HEADS * HEAD_DIM) == 0 and w_conv % MXU_TILE == 0
    assert w_attn % MXU_TILE == 0 and MXU_TILE % HEAD_DIM == 0
    assert b % ATTN_BATCHES == 0 and s % (ATTN_TILES * TQ) == 0

    idx = np.arange(TK)
    tri = jnp.asarray(idx[:, None] > idx[None, :], BF16)
    causal = np.where(idx[None, :] < idx[:, None], 0.0, MASKED_SCORE).astype(np.float32)
    per_head = lambda m: np.tile(m, (GROUP_HEADS, 1))
    bias_a = jnp.asarray(per_head(causal[:TQ // 2, :TK // 2]))
    bias_bc = jnp.asarray(np.concatenate([per_head(causal[TQ // 2:3 * TQ // 4]),
                                          per_head(causal[3 * TQ // 4:])], axis=0))
    gmat = _group_matrix()
    row = lambda g: g.reshape(1, -1)

    h = x
    for i in range(depth):
        (conv, q, k, v), (wo, wu, wd, wg, wp) = _mixer_in(
            h, row(g_mix[i]), w_in[i], conv_w[i], row(g_conv_out[i]), gmat, w_conv, w_attn,
            [w_out[i], w_up[i], w_down[i], w_ple_gate[i], w_ple_proj[i]])
        attn = _attention(q, k, v, tri, bias_a, bias_bc, row(g_attn_out[i]), gmat)
        h = _post(h.reshape(b * s, d), conv.reshape(b * s, w_conv), attn.reshape(b * s, w_attn),
                  p[i].reshape(b * s, -1), wo, row(g_mlp[i]), wu, wd, row(g_ple[i]), wg, wp,
                  row(g_final), apply_final=(i == depth - 1)).reshape(b, s, d)
    return h
```

```python
import functools

import jax
import jax.numpy as jnp
import numpy as np
from jax import lax
from jax.experimental import pallas as pl
from jax.experimental.pallas import tpu as pltpu

HEAD_DIM = 64
CONV_K = 3
EPS = 1e-6
LOG2_E = 1.4426950408889634
UNDERFLOW_BITS = 150.0
MASKED_SCORE = -1e30

F32 = jnp.float32
BF16 = jnp.bfloat16

V7X_VMEM_BYTES = 64 * 1024 * 1024
VMEM_LIMIT_BYTES = V7X_VMEM_BYTES - 8 * 1024 * 1024
ATTN_VMEM_LIMIT_BYTES = V7X_VMEM_BYTES - 2 * 1024 * 1024
SUBLANES = 8
BF16_SUBLANES = 16
MXU_TILE = 256

TM_IN = 1024
IN_SUB_ROWS = 256
TQ = 256
TK = 256
GROUP_HEADS = 4
ATTN_BATCHES = 2
ATTN_TILES = 2
TM_POST = 512
POST_SUB_ROWS = 256


def _rms_scale(x):
    return lax.rsqrt(jnp.mean(x * x, axis=-1, keepdims=True) + EPS)


def _group_mean(sq, gmat):
    sqb = sq.astype(BF16)
    return jnp.concatenate(
        [jnp.dot(sqb[:, l0:l0 + MXU_TILE], gmat, preferred_element_type=F32)
         for l0 in range(0, sq.shape[1], MXU_TILE)], axis=1)


def _mixer_in_kernel(x_ref, g_ref, w32_ref, cw_ref, gco_ref, gmat_ref, *rest, w_conv, w_attn,
                     n_side):
    side_in, rest = rest[:n_side], rest[n_side:]
    conv_ref, q_ref, k_ref, v_ref = rest[:4]
    side_out = rest[4:4 + n_side]
    vbuf, w_ref = rest[4 + n_side:]
    tm = x_ref.shape[1]

    @pl.when((pl.program_id(0) == 0) & (pl.program_id(1) == 0))
    def _():
        for r0 in range(0, w32_ref.shape[0], IN_SUB_ROWS):
            w_ref[r0:r0 + IN_SUB_ROWS, :] = w32_ref[r0:r0 + IN_SUB_ROWS, :].astype(BF16)

    for src, dst in zip(side_in, side_out):
        dst[...] = src[...].astype(BF16)

    @pl.when(pl.program_id(1) == 0)
    def _():
        vbuf[0:SUBLANES, :] = jnp.zeros((SUBLANES, w_conv), F32)

    starts = list(range(0, tm, IN_SUB_ROWS))
    a = []
    for r0 in starts:
        x = x_ref[0, r0:r0 + IN_SUB_ROWS, :]
        a.append((x * _rms_scale(x) * g_ref[...]).astype(BF16))
    proj = [jnp.dot(as_, w_ref[...], preferred_element_type=F32) for as_ in a]

    o = 3 * w_conv
    for r0, pr in zip(starts, proj):
        rs = slice(r0, r0 + IN_SUB_ROWS)
        vbuf[SUBLANES + r0:SUBLANES + r0 + IN_SUB_ROWS, :] = (
            pr[:, w_conv:2 * w_conv] * pr[:, 2 * w_conv:3 * w_conv])
        q_ref[0, rs, :] = (pr[:, o:o + w_attn] * (HEAD_DIM ** -0.5 * LOG2_E)).astype(BF16)
        k_ref[0, rs, :] = pr[:, o + w_attn:o + 2 * w_attn].astype(BF16)
        v_ref[0, rs, :] = pr[:, o + 2 * w_attn:o + 3 * w_attn].astype(BF16)

    cw = [cw_ref[:, j * w_conv:(j + 1) * w_conv] for j in range(CONV_K)]
    for r0, pr in zip(starts, proj):
        v0 = vbuf[SUBLANES + r0:SUBLANES + r0 + IN_SUB_ROWS, :]
        v1 = vbuf[SUBLANES - 1 + r0:SUBLANES - 1 + r0 + IN_SUB_ROWS, :]
        v2 = vbuf[SUBLANES - 2 + r0:SUBLANES - 2 + r0 + IN_SUB_ROWS, :]
        c = pr[:, 0:w_conv] * (cw[0] * v2 + cw[1] * v1 + cw[2] * v0)
        ms = _group_mean(c * c, gmat_ref[...])
        conv_ref[0, r0:r0 + IN_SUB_ROWS, :] = (
            c * lax.rsqrt(ms + EPS) * gco_ref[...]).astype(BF16)
    vbuf[0:SUBLANES, :] = vbuf[tm:tm + SUBLANES, :]


def _mixer_in(x, g_mix, w_in, conv_w, g_conv_out, gmat, w_conv, w_attn, side_weights):
    b, s, d = x.shape
    tm = TM_IN
    n_i = s // tm
    steps = b * n_i
    const = lambda *_: (0, 0)
    tile = lambda bi, i: (bi, i, 0)
    out_sd = lambda w: jax.ShapeDtypeStruct((b, s, w), BF16)
    side_rows = [w.shape[0] // steps for w in side_weights]
    assert all(w.shape[0] % steps == 0 and r % BF16_SUBLANES == 0
               for w, r in zip(side_weights, side_rows))
    side_specs = [pl.BlockSpec((r, w.shape[1]), lambda bi, i: (bi * n_i + i, 0))
                  for w, r in zip(side_weights, side_rows)]
    outs = pl.pallas_call(
        functools.partial(_mixer_in_kernel, w_conv=w_conv, w_attn=w_attn,
                          n_side=len(side_weights)),
        grid=(b, n_i),
        in_specs=[
            pl.BlockSpec((1, tm, d), tile),
            pl.BlockSpec((1, d), const),
            pl.BlockSpec(w_in.shape, const, pipeline_mode=pl.Buffered(1)),
            pl.BlockSpec((1, CONV_K * w_conv), const),
            pl.BlockSpec((1, w_conv), const),
            pl.BlockSpec(gmat.shape, const),
        ] + side_specs,
        out_specs=[
            pl.BlockSpec((1, tm, w_conv), tile),
            pl.BlockSpec((1, tm, w_attn), tile),
            pl.BlockSpec((1, tm, w_attn), tile),
            pl.BlockSpec((1, tm, w_attn), tile),
        ] + side_specs,
        out_shape=[out_sd(w_conv), out_sd(w_attn), out_sd(w_attn), out_sd(w_attn)]
        + [jax.ShapeDtypeStruct(w.shape, BF16) for w in side_weights],
        scratch_shapes=[pltpu.VMEM((tm + SUBLANES, w_conv), F32),
                        pltpu.VMEM(w_in.shape, BF16)],
        compiler_params=pltpu.CompilerParams(
            dimension_semantics=("arbitrary", "arbitrary"),
            vmem_limit_bytes=VMEM_LIMIT_BYTES),
        name="mixer_in",
    )(x, g_mix, w_in, conv_w, g_conv_out, gmat, *side_weights)
    return outs[:4], outs[4:]


def _attn_kernel(q_ref, k_hbm, v_hbm, tri_ref, bias_a_ref, bias_bc_ref, g_ref, gmat_ref,
                 o_ref, qst_ref, acc_ref, carry_ref, k_ref, v_ref, kv_sem):
    nb, tq = q_ref.shape[0], TQ
    tiles = q_ref.shape[1] // tq
    tk = tri_ref.shape[0]
    chains, rows, gw = qst_ref.shape
    groups = chains // (tiles * nb)
    pair = 2 * HEAD_DIM
    where = [(c // (nb * groups), (c // groups) % nb, (c % groups) * gw) for c in range(chains)]
    tile_chains = [[c for c in range(chains) if where[c][0] == t] for t in range(tiles)]
    tile_i = [tiles * pl.program_id(1) + t for t in range(tiles)]
    segs = []
    for q0, qn in ((0, tq // 2), (tq // 2, tq // 4), (3 * tq // 4, tq // 4)):
        segs.append((q0, qn, slice(GROUP_HEADS * q0, GROUP_HEADS * (q0 + qn))))
    (_, _, ra), (_, _, rb), (_, _, rc) = segs
    rab, rbc, full = slice(0, rb.stop), slice(rb.start, rows), slice(0, rows)

    step, last_step = pl.program_id(1), pl.num_programs(1) - 1
    step_keys = tiles * tq
    batches = pl.ds(pl.program_id(0) * nb, nb)

    def kv_copies(j):
        keys = pl.ds(pl.multiple_of(j * step_keys, step_keys), step_keys)
        return [pltpu.make_async_copy(src.at[batches, keys, :], dst.at[:, keys, :], kv_sem.at[n])
                for n, (src, dst) in enumerate(((k_hbm, k_ref), (v_hbm, v_ref)))]

    @pl.when(step == 0)
    def _():
        for copy in kv_copies(0):
            copy.start()
        for copy in kv_copies(0):
            copy.wait()

    @pl.when(step < last_step)
    def _():
        for copy in kv_copies(step + 1):
            copy.start()

    @pl.when((pl.program_id(0) == 0) & (step == 0))
    def _():
        qst_ref[...] = jnp.zeros_like(qst_ref)

    for c, (t, bl, l0) in enumerate(where):
        for q0, qn, rs in segs:
            for hh in range(GROUP_HEADS):
                r0 = rs.start + hh * qn
                lanes = slice(hh * HEAD_DIM, (hh + 1) * HEAD_DIM)
                qst_ref[c, r0:r0 + qn, lanes] = q_ref[
                    bl, t * tq + q0:t * tq + q0 + qn, l0 + lanes.start:l0 + lanes.stop]

    def scores(c, rs, start, nk):
        _, bl, l0 = where[c]
        kj = k_ref[bl, pl.ds(start, nk), l0:l0 + gw]
        return lax.dot_general(qst_ref[c, rs, :], kj, (((1,), (1,)), ((), ())),
                               preferred_element_type=F32)

    def weights(z, carry):
        nk = z.shape[1]
        sp = jnp.maximum(z, 0.0) + jnp.log(1.0 + jnp.exp2(-jnp.abs(z))) * LOG2_E
        cs = jnp.dot(sp.astype(BF16), tri_ref[0:nk, 0:nk],
                     preferred_element_type=F32)
        total = cs[:, 0:1] + sp[:, 0:1]
        if carry is None:
            return jnp.exp2(z - (sp + cs)).astype(BF16), total
        return jnp.exp2(z - (sp + cs + carry)).astype(BF16), carry + total

    def accumulate(c, rs, ab, start, assign=False):
        _, bl, l0 = where[c]
        nk = ab.shape[1]
        for _, qn, seg in segs:
            if seg.start < rs.start or seg.stop > rs.stop:
                continue
            half = (seg.stop - seg.start) // 2
            for p in range(GROUP_HEADS // 2):
                vj = v_ref[bl, pl.ds(start, nk), l0 + p * pair:l0 + (p + 1) * pair]
                src = seg.start - rs.start + p * half
                av = jnp.dot(ab[src:src + half, :], vj, preferred_element_type=F32)
                dst = slice(seg.start + p * half, seg.start + (p + 1) * half)
                if assign:
                    acc_ref[c, dst, :] = av
                else:
                    acc_ref[c, dst, :] += av

    d0 = [pl.multiple_of(i * tk, tk) for i in tile_i]
    p0 = [pl.multiple_of(jnp.maximum(i - 1, 0) * tk, tk) for i in tile_i]
    no_prev = [jnp.where(i == 0, -MASKED_SCORE, 0.0).astype(F32) for i in tile_i]
    tile_of = [where[c][0] for c in range(chains)]
    z_da = [scores(c, ra, d0[tile_of[c]], tk // 2) + bias_a_ref[...] for c in range(chains)]
    z_dbc = [scores(c, rbc, d0[tile_of[c]], tk) + bias_bc_ref[...] for c in range(chains)]
    z_pab = [scores(c, rab, p0[tile_of[c]], tk) for c in range(chains)]
    low_c = []
    nb_rows = rb.stop - rb.start
    for c in range(chains):
        t = tile_of[c]
        ab, cr_a = weights(z_da[c], None)
        accumulate(c, ra, ab, d0[t], assign=True)
        ab, cr_bc = weights(z_dbc[c], None)
        accumulate(c, rbc, ab, d0[t], assign=True)
        cr_bc = cr_bc + no_prev[t]
        carry_ref[c, rc, :] = cr_bc[nb_rows:]
        low_c.append(jnp.min(cr_bc[nb_rows:]))
        carry_ab = jnp.concatenate([cr_a + no_prev[t], cr_bc[0:nb_rows]], axis=0)
        ab, cr = weights(z_pab[c], carry_ab)
        accumulate(c, rab, ab, p0[t])
        carry_ref[c, rab, :] = cr

    @pl.when(functools.reduce(jnp.minimum, low_c) < UNDERFLOW_BITS)
    def _():
        for c in range(chains):
            start = p0[tile_of[c]]
            ab, cr = weights(scores(c, rc, start, tk), carry_ref[c, rc, :])
            accumulate(c, rc, ab, start)
            carry_ref[c, rc, :] = cr

    for t in range(tiles):
        i = tile_i[t]

        def lowest_carry():
            return functools.reduce(jnp.minimum,
                                    [jnp.min(carry_ref[c]) for c in tile_chains[t]])

        def body(state):
            n, _ = state
            start = pl.multiple_of((i - 1 - n) * tk, tk)
            for c in tile_chains[t]:
                ab, cr = weights(scores(c, full, start, tk), carry_ref[c])
                accumulate(c, full, ab, start)
                carry_ref[c] = cr
            return n + 1, lowest_carry()

        def more(state):
            n, low = state
            return (n < i) & (low < UNDERFLOW_BITS)

        lax.while_loop(more, body, (jnp.int32(1), lowest_carry()))

    for t in range(tiles):
        for bl in range(nb):
            for q0, qn, rs in segs:
                lane_p = lax.broadcasted_iota(jnp.int32, (qn, pair), 1)
                def head_rows(c, hh):
                    r0 = rs.start + hh * qn
                    return acc_ref[c, r0:r0 + qn, :]
                group_chains = [c for c in tile_chains[t] if where[c][1] == bl]
                out = jnp.concatenate(
                    [jnp.where(lane_p < HEAD_DIM, head_rows(c, 2 * p), head_rows(c, 2 * p + 1))
                     for c in group_chains for p in range(GROUP_HEADS // 2)], axis=1)
                ms = _group_mean(out * out, gmat_ref[...])
                o_ref[bl, t * tq + q0:t * tq + q0 + qn, :] = (
                    out * lax.rsqrt(ms + EPS) * g_ref[...]).astype(BF16)

    @pl.when(step < last_step)
    def _():
        for copy in kv_copies(step + 1):
            copy.wait()


def _attention(q, k, v, tri, bias_a, bias_bc, g_attn_out, gmat):
    b, s, w = q.shape
    nb = ATTN_BATCHES
    gw = GROUP_HEADS * HEAD_DIM
    tq = ATTN_TILES * TQ
    chains = ATTN_TILES * nb * (w // gw)
    rows = GROUP_HEADS * TQ
    const = lambda bi, i: (0, 0)
    return pl.pallas_call(
        _attn_kernel,
        grid=(b // nb, s // tq),
        in_specs=[
            pl.BlockSpec((nb, tq, w), lambda bi, i: (bi, i, 0)),
            pl.BlockSpec(memory_space=pl.ANY),
            pl.BlockSpec(memory_space=pl.ANY),
            pl.BlockSpec(tri.shape, const),
            pl.BlockSpec(bias_a.shape, const),
            pl.BlockSpec(bias_bc.shape, const),
            pl.BlockSpec((1, w), const),
            pl.BlockSpec(gmat.shape, const),
        ],
        out_specs=pl.BlockSpec((nb, tq, w), lambda bi, i: (bi, i, 0)),
        out_shape=jax.ShapeDtypeStruct((b, s, w), BF16),
        scratch_shapes=[pltpu.VMEM((chains, rows, gw), BF16),
                        pltpu.VMEM((chains, rows, 2 * HEAD_DIM), F32),
                        pltpu.VMEM((chains, rows, 1), F32),
                        pltpu.VMEM((nb, s, w), BF16),
                        pltpu.VMEM((nb, s, w), BF16),
                        pltpu.SemaphoreType.DMA((2,))],
        compiler_params=pltpu.CompilerParams(
            dimension_semantics=("arbitrary", "arbitrary"),
            vmem_limit_bytes=ATTN_VMEM_LIMIT_BYTES),
        name="sb_attention",
    )(q, k, v, tri, bias_a, bias_bc, g_attn_out, gmat)


def _post_kernel(x_ref, conv_ref, attn_ref, p_ref, wo_ref, gmlp_ref, wup_ref, wdn_ref,
                 gple_ref, wg_ref, wp_ref, gfin_ref, o_ref, *, apply_final):
    w_conv = conv_ref.shape[1]
    subs = [slice(r0, r0 + POST_SUB_ROWS) for r0 in range(0, x_ref.shape[0], POST_SUB_ROWS)]
    dot = functools.partial(jnp.dot, preferred_element_type=F32)
    h = [x_ref[rs, :] + dot(conv_ref[rs, :], wo_ref[0:w_conv, :]) for rs in subs]
    h = [hs + dot(attn_ref[rs, :], wo_ref[w_conv:, :]) for hs, rs in zip(h, subs)]
    m = [(hs * _rms_scale(hs) * gmlp_ref[...]).astype(BF16) for hs in h]
    r = [jnp.square(jnp.maximum(dot(ms, wup_ref[...]), 0.0)).astype(BF16) for ms in m]
    h = [hs + dot(rr, wdn_ref[...]) for hs, rr in zip(h, r)]
    n = [(hs * _rms_scale(hs) * gple_ref[...]).astype(BF16) for hs in h]
    pp = [dot(p_ref[rs, :].astype(BF16), wp_ref[...]) for rs in subs]
    gate = [jax.nn.sigmoid(dot(ns, wg_ref[...])) for ns in n]
    h = [hs + gs * ps for hs, gs, ps in zip(h, gate, pp)]
    for hs, rs in zip(h, subs):
        if apply_final:
            hs = hs * _rms_scale(hs) * gfin_ref[...]
        o_ref[rs, :] = hs


def _post(x2, conv2, attn2, p2, w_out, g_mlp, w_up, w_down, g_ple, w_gate, w_proj, g_final,
          apply_final):
    t, d = x2.shape
    tm = TM_POST
    tile = lambda i: (i, 0)
    const = lambda i: (0, 0)
    wspec = lambda w: pl.BlockSpec(w.shape, const, pipeline_mode=pl.Buffered(1))
    gspec = pl.BlockSpec((1, d), const)
    return pl.pallas_call(
        functools.partial(_post_kernel, apply_final=apply_final),
        grid=(t // tm,),
        in_specs=[
            pl.BlockSpec((tm, d), tile),
            pl.BlockSpec((tm, conv2.shape[1]), tile),
            pl.BlockSpec((tm, attn2.shape[1]), tile),
            pl.BlockSpec((tm, p2.shape[1]), tile),
            wspec(w_out), gspec, wspec(w_up), wspec(w_down), gspec, wspec(w_gate), wspec(w_proj),
            gspec,
        ],
        out_specs=pl.BlockSpec((tm, d), tile),
        out_shape=jax.ShapeDtypeStruct((t, d), F32),
        compiler_params=pltpu.CompilerParams(
            dimension_semantics=("arbitrary",),
            vmem_limit_bytes=VMEM_LIMIT_BYTES),
        name="post",
    )(x2, conv2, attn2, p2, w_out, g_mlp, w_up, w_down, g_ple, w_gate, w_proj, g_final)


def _group_matrix():
    g = np.arange(MXU_TILE) // HEAD_DIM
    return jnp.asarray(np.where(g[:, None] == g[None, :], 1.0 / HEAD_DIM, 0.0), BF16)


def kernel(x, p, g_mix, w_in, conv_w, g_conv_out, g_attn_out, w_out, g_mlp, w_up, w_down,
           g_ple, w_ple_gate, w_ple_proj, g_final):
    b, s, d = x.shape
    depth = p.shape[0]
    w_conv = conv_w.shape[-1]
    w_attn = g_attn_out.shape[-1]
    assert s % TM_IN == 0 and s % TQ == 0 and TQ == TK and (b * s) % TM_POST == 0
    assert w_attn % (GROUP_HEADS * HEAD_DIM) == 0 and w_conv % MXU_TILE == 0
    assert w_attn % MXU_TILE == 0 and MXU_TILE % HEAD_DIM == 0
    assert b % ATTN_BATCHES == 0 and s % (ATTN_TILES * TQ) == 0

    idx = np.arange(TK)
    tri = jnp.asarray(idx[:, None] > idx[None, :], BF16)
    causal = np.where(idx[None, :] < idx[:, None], 0.0, MASKED_SCORE).astype(np.float32)
    per_head = lambda m: np.tile(m, (GROUP_HEADS, 1))
    bias_a = jnp.asarray(per_head(causal[:TQ // 2, :TK // 2]))
    bias_bc = jnp.asarray(np.concatenate([per_head(causal[TQ // 2:3 * TQ // 4]),
                                          per_head(causal[3 * TQ // 4:])], axis=0))
    gmat = _group_matrix()
    row = lambda g: g.reshape(1, -1)

    h = x
    for i in range(depth):
        (conv, q, k, v), (wo, wu, wd, wg, wp) = _mixer_in(
            h, row(g_mix[i]), w_in[i], row(conv_w[i]), row(g_conv_out[i]), gmat, w_conv, w_attn,
            [w_out[i], w_up[i], w_down[i], w_ple_gate[i], w_ple_proj[i]])
        attn = _attention(q, k, v, tri, bias_a, bias_bc, row(g_attn_out[i]), gmat)
        h = _post(h.reshape(b * s, d), conv.reshape(b * s, w_conv), attn.reshape(b * s, w_attn),
                  p[i].reshape(b * s, -1), wo, row(g_mlp[i]), wu, wd, row(g_ple[i]), wg, wp,
                  row(g_final), apply_final=(i == depth - 1)).reshape(b, s, d)
    return h
```

```python
import functools

import jax
import jax.numpy as jnp
import numpy as np
from jax import lax
from jax.experimental import pallas as pl
from jax.experimental.pallas import tpu as pltpu

HEAD_DIM = 64
CONV_K = 3
EPS = 1e-6
LOG2_E = 1.4426950408889634
UNDERFLOW_BITS = 150.0
MASKED_SCORE = -1e30

F32 = jnp.float32
BF16 = jnp.bfloat16

V7X_VMEM_BYTES = 64 * 1024 * 1024
VMEM_LIMIT_BYTES = V7X_VMEM_BYTES - 8 * 1024 * 1024
ATTN_VMEM_LIMIT_BYTES = V7X_VMEM_BYTES - 2 * 1024 * 1024
SUBLANES = 8
BF16_SUBLANES = 16
MXU_TILE = 256

TM_IN = 1024
IN_SUB_ROWS = 256
TQ = 256
TK = 256
GROUP_HEADS = 4
ATTN_BATCHES = 2
ATTN_TILES = 2
TM_POST = 512
POST_SUB_ROWS = 256


def _rms_scale(x):
    return lax.rsqrt(jnp.mean(x * x, axis=-1, keepdims=True) + EPS)


def _group_mean(sq, gmat):
    sqb = sq.astype(BF16)
    return jnp.concatenate(
        [jnp.dot(sqb[:, l0:l0 + MXU_TILE], gmat, preferred_element_type=F32)
         for l0 in range(0, sq.shape[1], MXU_TILE)], axis=1)


def _mixer_in_kernel(x_ref, g_ref, w32_ref, cw_ref, gco_ref, gmat_ref, *rest, w_conv, w_attn,
                     n_side):
    side_in, rest = rest[:n_side], rest[n_side:]
    conv_ref, q_ref, k_ref, v_ref = rest[:4]
    side_out = rest[4:4 + n_side]
    vbuf, w_ref = rest[4 + n_side:]
    tm = x_ref.shape[1]

    @pl.when((pl.program_id(0) == 0) & (pl.program_id(1) == 0))
    def _():
        for r0 in range(0, w32_ref.shape[0], IN_SUB_ROWS):
            w_ref[r0:r0 + IN_SUB_ROWS, :] = w32_ref[r0:r0 + IN_SUB_ROWS, :].astype(BF16)

    for src, dst in zip(side_in, side_out):
        dst[...] = src[...].astype(BF16)

    @pl.when(pl.program_id(1) == 0)
    def _():
        vbuf[0:SUBLANES, :] = jnp.zeros((SUBLANES, w_conv), F32)

    starts = list(range(0, tm, IN_SUB_ROWS))
    a = []
    for r0 in starts:
        x = x_ref[0, r0:r0 + IN_SUB_ROWS, :]
        a.append((x * _rms_scale(x) * g_ref[...]).astype(BF16))
    proj = [jnp.dot(as_, w_ref[...], preferred_element_type=F32) for as_ in a]

    o = 3 * w_conv
    for r0, pr in zip(starts, proj):
        rs = slice(r0, r0 + IN_SUB_ROWS)
        vbuf[SUBLANES + r0:SUBLANES + r0 + IN_SUB_ROWS, :] = (
            pr[:, w_conv:2 * w_conv] * pr[:, 2 * w_conv:3 * w_conv])
        q_ref[0, rs, :] = (pr[:, o:o + w_attn] * (HEAD_DIM ** -0.5 * LOG2_E)).astype(BF16)
        k_ref[0, rs, :] = pr[:, o + w_attn:o + 2 * w_attn].astype(BF16)
        v_ref[0, rs, :] = pr[:, o + 2 * w_attn:o + 3 * w_attn].astype(BF16)

    cw = [cw_ref[:, j * w_conv:(j + 1) * w_conv] for j in range(CONV_K)]
    for r0, pr in zip(starts, proj):
        v0 = vbuf[SUBLANES + r0:SUBLANES + r0 + IN_SUB_ROWS, :]
        v1 = vbuf[SUBLANES - 1 + r0:SUBLANES - 1 + r0 + IN_SUB_ROWS, :]
        v2 = vbuf[SUBLANES - 2 + r0:SUBLANES - 2 + r0 + IN_SUB_ROWS, :]
        c = pr[:, 0:w_conv] * (cw[0] * v2 + cw[1] * v1 + cw[2] * v0)
        ms = _group_mean(c * c, gmat_ref[...])
        conv_ref[0, r0:r0 + IN_SUB_ROWS, :] = (
            c * lax.rsqrt(ms + EPS) * gco_ref[...]).astype(BF16)
    vbuf[0:SUBLANES, :] = vbuf[tm:tm + SUBLANES, :]


def _mixer_in(x, g_mix, w_in, conv_w, g_conv_out, gmat, w_conv, w_attn, side_weights):
    b, s, d = x.shape
    tm = TM_IN
    n_i = s // tm
    steps = b * n_i
    const = lambda *_: (0, 0)
    tile = lambda bi, i: (bi, i, 0)
    out_sd = lambda w: jax.ShapeDtypeStruct((b, s, w), BF16)
    side_rows = [w.shape[0] // steps for w in side_weights]
    assert all(w.shape[0] % steps == 0 and r % BF16_SUBLANES == 0
               for w, r in zip(side_weights, side_rows))
    side_specs = [pl.BlockSpec((r, w.shape[1]), lambda bi, i: (bi * n_i + i, 0))
                  for w, r in zip(side_weights, side_rows)]
    outs = pl.pallas_call(
        functools.partial(_mixer_in_kernel, w_conv=w_conv, w_attn=w_attn,
                          n_side=len(side_weights)),
        grid=(b, n_i),
        in_specs=[
            pl.BlockSpec((1, tm, d), tile),
            pl.BlockSpec((1, d), const),
            pl.BlockSpec(w_in.shape, const, pipeline_mode=pl.Buffered(1)),
            pl.BlockSpec((1, CONV_K * w_conv), const),
            pl.BlockSpec((1, w_conv), const),
            pl.BlockSpec(gmat.shape, const),
        ] + side_specs,
        out_specs=[
            pl.BlockSpec((1, tm, w_conv), tile),
            pl.BlockSpec((1, tm, w_attn), tile),
            pl.BlockSpec((1, tm, w_attn), tile),
            pl.BlockSpec((1, tm, w_attn), tile),
        ] + side_specs,
        out_shape=[out_sd(w_conv), out_sd(w_attn), out_sd(w_attn), out_sd(w_attn)]
        + [jax.ShapeDtypeStruct(w.shape, BF16) for w in side_weights],
        scratch_shapes=[pltpu.VMEM((tm + SUBLANES, w_conv), F32),
                        pltpu.VMEM(w_in.shape, BF16)],
        compiler_params=pltpu.CompilerParams(
            dimension_semantics=("arbitrary", "arbitrary"),
            vmem_limit_bytes=VMEM_LIMIT_BYTES),
        name="mixer_in",
    )(x, g_mix, w_in, conv_w, g_conv_out, gmat, *side_weights)
    return outs[:4], outs[4:]


def _attn_kernel(q_ref, k_hbm, v_hbm, tri_ref, bias_a_ref, bias_bc_ref, g_ref, gmat_ref,
                 o_ref, qst_ref, acc_ref, carry_ref, k_ref, v_ref, kv_sem):
    nb, tq = q_ref.shape[0], TQ
    tiles = q_ref.shape[1] // tq
    tk = tri_ref.shape[0]
    chains, rows, gw = qst_ref.shape
    groups = chains // (tiles * nb)
    pair = 2 * HEAD_DIM
    where = [(c // (nb * groups), (c // groups) % nb, (c % groups) * gw) for c in range(chains)]
    tile_chains = [[c for c in range(chains) if where[c][0] == t] for t in range(tiles)]
    tile_i = [tiles * pl.program_id(1) + t for t in range(tiles)]
    segs = []
    for q0, qn in ((0, tq // 2), (tq // 2, tq // 4), (3 * tq // 4, tq // 4)):
        segs.append((q0, qn, slice(GROUP_HEADS * q0, GROUP_HEADS * (q0 + qn))))
    (_, _, ra), (_, _, rb), (_, _, rc) = segs
    rab, rbc, full = slice(0, rb.stop), slice(rb.start, rows), slice(0, rows)

    step, last_step = pl.program_id(1), pl.num_programs(1) - 1
    step_keys = tiles * tq
    batches = pl.ds(pl.program_id(0) * nb, nb)

    def kv_copies(j):
        keys = pl.ds(pl.multiple_of(j * step_keys, step_keys), step_keys)
        return [pltpu.make_async_copy(src.at[batches, keys, :], dst.at[:, keys, :], kv_sem.at[n])
                for n, (src, dst) in enumerate(((k_hbm, k_ref), (v_hbm, v_ref)))]

    @pl.when(step == 0)
    def _():
        for copy in kv_copies(0):
            copy.start()
        for copy in kv_copies(0):
            copy.wait()

    @pl.when(step < last_step)
    def _():
        for copy in kv_copies(step + 1):
            copy.start()

    for c, (t, bl, l0) in enumerate(where):
        for q0, qn, rs in segs:
            q = q_ref[bl, t * tq + q0:t * tq + q0 + qn, l0:l0 + gw]
            lane = lax.broadcasted_iota(jnp.int32, (qn, gw), 1)
            for hh in range(GROUP_HEADS):
                in_head = (lane >= hh * HEAD_DIM) & (lane < (hh + 1) * HEAD_DIM)
                r0 = rs.start + hh * qn
                qst_ref[c, r0:r0 + qn, :] = jnp.where(in_head, q, jnp.zeros_like(q))

    def scores(c, rs, start, nk):
        _, bl, l0 = where[c]
        kj = k_ref[bl, pl.ds(start, nk), l0:l0 + gw]
        return lax.dot_general(qst_ref[c, rs, :], kj, (((1,), (1,)), ((), ())),
                               preferred_element_type=F32)

    def weights(z, carry):
        nk = z.shape[1]
        sp = jnp.maximum(z, 0.0) + jnp.log(1.0 + jnp.exp2(-jnp.abs(z))) * LOG2_E
        cs = jnp.dot(sp.astype(BF16), tri_ref[0:nk, 0:nk],
                     preferred_element_type=F32)
        total = cs[:, 0:1] + sp[:, 0:1]
        if carry is None:
            return jnp.exp2(z - (sp + cs)).astype(BF16), total
        return jnp.exp2(z - (sp + cs + carry)).astype(BF16), carry + total

    def accumulate(c, rs, ab, start, assign=False):
        _, bl, l0 = where[c]
        nk = ab.shape[1]
        for _, qn, seg in segs:
            if seg.start < rs.start or seg.stop > rs.stop:
                continue
            half = (seg.stop - seg.start) // 2
            for p in range(GROUP_HEADS // 2):
                vj = v_ref[bl, pl.ds(start, nk), l0 + p * pair:l0 + (p + 1) * pair]
                src = seg.start - rs.start + p * half
                av = jnp.dot(ab[src:src + half, :], vj, preferred_element_type=F32)
                dst = slice(seg.start + p * half, seg.start + (p + 1) * half)
                if assign:
                    acc_ref[c, dst, :] = av
                else:
                    acc_ref[c, dst, :] += av

    d0 = [pl.multiple_of(i * tk, tk) for i in tile_i]
    p0 = [pl.multiple_of(jnp.maximum(i - 1, 0) * tk, tk) for i in tile_i]
    no_prev = [jnp.where(i == 0, -MASKED_SCORE, 0.0).astype(F32) for i in tile_i]
    tile_of = [where[c][0] for c in range(chains)]
    z_da = [scores(c, ra, d0[tile_of[c]], tk // 2) + bias_a_ref[...] for c in range(chains)]
    z_dbc = [scores(c, rbc, d0[tile_of[c]], tk) + bias_bc_ref[...] for c in range(chains)]
    z_pab = [scores(c, rab, p0[tile_of[c]], tk) for c in range(chains)]
    low_c = []
    nb_rows = rb.stop - rb.start
    for c in range(chains):
        t = tile_of[c]
        ab, cr_a = weights(z_da[c], None)
        accumulate(c, ra, ab, d0[t], assign=True)
        ab, cr_bc = weights(z_dbc[c], None)
        accumulate(c, rbc, ab, d0[t], assign=True)
        cr_bc = cr_bc + no_prev[t]
        carry_ref[c, rc, :] = cr_bc[nb_rows:]
        low_c.append(jnp.min(cr_bc[nb_rows:]))
        carry_ab = jnp.concatenate([cr_a + no_prev[t], cr_bc[0:nb_rows]], axis=0)
        ab, cr = weights(z_pab[c], carry_ab)
        accumulate(c, rab, ab, p0[t])
        carry_ref[c, rab, :] = cr

    @pl.when(functools.reduce(jnp.minimum, low_c) < UNDERFLOW_BITS)
    def _():
        for c in range(chains):
            start = p0[tile_of[c]]
            ab, cr = weights(scores(c, rc, start, tk), carry_ref[c, rc, :])
            accumulate(c, rc, ab, start)
            carry_ref[c, rc, :] = cr

    for t in range(tiles):
        i = tile_i[t]

        def lowest_carry():
            return functools.reduce(jnp.minimum,
                                    [jnp.min(carry_ref[c]) for c in tile_chains[t]])

        def body(state):
            n, _ = state
            start = pl.multiple_of((i - 1 - n) * tk, tk)
            for c in tile_chains[t]:
                ab, cr = weights(scores(c, full, start, tk), carry_ref[c])
                accumulate(c, full, ab, start)
                carry_ref[c] = cr
            return n + 1, lowest_carry()

        def more(state):
            n, low = state
            return (n < i) & (low < UNDERFLOW_BITS)

        lax.while_loop(more, body, (jnp.int32(1), lowest_carry()))

    for t in range(tiles):
        for bl in range(nb):
            for q0, qn, rs in segs:
                lane_p = lax.broadcasted_iota(jnp.int32, (qn, pair), 1)
                def head_rows(c, hh):
                    r0 = rs.start + hh * qn
                    return acc_ref[c, r0:r0 + qn, :]
                group_chains = [c for c in tile_chains[t] if where[c][1] == bl]
                out = jnp.concatenate(
                    [jnp.where(lane_p < HEAD_DIM, head_rows(c, 2 * p), head_rows(c, 2 * p + 1))
                     for c in group_chains for p in range(GROUP_HEADS // 2)], axis=1)
                ms = _group_mean(out * out, gmat_ref[...])
                o_ref[bl, t * tq + q0:t * tq + q0 + qn, :] = (
                    out * lax.rsqrt(ms + EPS) * g_ref[...]).astype(BF16)

    @pl.when(step < last_step)
    def _():
        for copy in kv_copies(step + 1):
            copy.wait()


def _attention(q, k, v, tri, bias_a, bias_bc, g_attn_out, gmat):
    b, s, w = q.shape
    nb = ATTN_BATCHES
    gw = GROUP_HEADS * HEAD_DIM
    tq = ATTN_TILES * TQ
    chains = ATTN_TILES * nb * (w // gw)
    rows = GROUP_HEADS * TQ
    const = lambda bi, i: (0, 0)
    return pl.pallas_call(
        _attn_kernel,
        grid=(b // nb, s // tq),
        in_specs=[
            pl.BlockSpec((nb, tq, w), lambda bi, i: (bi, i, 0)),
            pl.BlockSpec(memory_space=pl.ANY),
            pl.BlockSpec(memory_space=pl.ANY),
            pl.BlockSpec(tri.shape, const),
            pl.BlockSpec(bias_a.shape, const),
            pl.BlockSpec(bias_bc.shape, const),
            pl.BlockSpec((1, w), const),
            pl.BlockSpec(gmat.shape, const),
        ],
        out_specs=pl.BlockSpec((nb, tq, w), lambda bi, i: (bi, i, 0)),
        out_shape=jax.ShapeDtypeStruct((b, s, w), BF16),
        scratch_shapes=[pltpu.VMEM((chains, rows, gw), BF16),
                        pltpu.VMEM((chains, rows, 2 * HEAD_DIM), F32),
                        pltpu.VMEM((chains, rows, 1), F32),
                        pltpu.VMEM((nb, s, w), BF16),
                        pltpu.VMEM((nb, s, w), BF16),
                        pltpu.SemaphoreType.DMA((2,))],
        compiler_params=pltpu.CompilerParams(
            dimension_semantics=("arbitrary", "arbitrary"),
            vmem_limit_bytes=ATTN_VMEM_LIMIT_BYTES),
        name="sb_attention",
    )(q, k, v, tri, bias_a, bias_bc, g_attn_out, gmat)


def _post_kernel(x_ref, conv_ref, attn_ref, p_ref, wo_ref, gmlp_ref, wup_ref, wdn_ref,
                 gple_ref, wg_ref, wp_ref, gfin_ref, o_ref, *, apply_final):
    w_conv = conv_ref.shape[1]
    subs = [slice(r0, r0 + POST_SUB_ROWS) for r0 in range(0, x_ref.shape[0], POST_SUB_ROWS)]
    dot = functools.partial(jnp.dot, preferred_element_type=F32)
    h = [x_ref[rs, :] + dot(conv_ref[rs, :], wo_ref[0:w_conv, :]) for rs in subs]
    h = [hs + dot(attn_ref[rs, :], wo_ref[w_conv:, :]) for hs, rs in zip(h, subs)]
    m = [(hs * _rms_scale(hs) * gmlp_ref[...]).astype(BF16) for hs in h]
    r = [jnp.square(jnp.maximum(dot(ms, wup_ref[...]), 0.0)).astype(BF16) for ms in m]
    h = [hs + dot(rr, wdn_ref[...]) for hs, rr in zip(h, r)]
    n = [(hs * _rms_scale(hs) * gple_ref[...]).astype(BF16) for hs in h]
    pp = [dot(p_ref[rs, :].astype(BF16), wp_ref[...]) for rs in subs]
    gate = [jax.nn.sigmoid(dot(ns, wg_ref[...])) for ns in n]
    h = [hs + gs * ps for hs, gs, ps in zip(h, gate, pp)]
    for hs, rs in zip(h, subs):
        if apply_final:
            hs = hs * _rms_scale(hs) * gfin_ref[...]
        o_ref[rs, :] = hs


def _post(x2, conv2, attn2, p2, w_out, g_mlp, w_up, w_down, g_ple, w_gate, w_proj, g_final,
          apply_final):
    t, d = x2.shape
    tm = TM_POST
    tile = lambda i: (i, 0)
    const = lambda i: (0, 0)
    wspec = lambda w: pl.BlockSpec(w.shape, const, pipeline_mode=pl.Buffered(1))
    gspec = pl.BlockSpec((1, d), const)
    return pl.pallas_call(
        functools.partial(_post_kernel, apply_final=apply_final),
        grid=(t // tm,),
        in_specs=[
            pl.BlockSpec((tm, d), tile),
            pl.BlockSpec((tm, conv2.shape[1]), tile),
            pl.BlockSpec((tm, attn2.shape[1]), tile),
            pl.BlockSpec((tm, p2.shape[1]), tile),
            wspec(w_out), gspec, wspec(w_up), wspec(w_down), gspec, wspec(w_gate), wspec(w_proj),
            gspec,
        ],
        out_specs=pl.BlockSpec((tm, d), tile),
        out_shape=jax.ShapeDtypeStruct((t, d), F32),
        compiler_params=pltpu.CompilerParams(
            dimension_semantics=("arbitrary",),
            vmem_limit_bytes=VMEM_LIMIT_BYTES),
        name="post",
    )(x2, conv2, attn2, p2, w_out, g_mlp, w_up, w_down, g_ple, w_gate, w_proj, g_final)


def _group_matrix():
    g = np.arange(MXU_TILE) // HEAD_DIM
    return jnp.asarray(np.where(g[:, None] == g[None, :], 1.0 / HEAD_DIM, 0.0), BF16)


def kernel(x, p, g_mix, w_in, conv_w, g_conv_out, g_attn_out, w_out, g_mlp, w_up, w_down,
           g_ple, w_ple_gate, w_ple_proj, g_final):
    b, s, d = x.shape
    depth = p.shape[0]
    w_conv = conv_w.shape[-1]
    w_attn = g_attn_out.shape[-1]
    assert s % TM_IN == 0 and s % TQ == 0 and TQ == TK and (b * s) % TM_POST == 0
    assert w_attn % (GROUP_HEADS * HEAD_DIM) == 0 and w_conv % MXU_TILE == 0
    assert w_attn % MXU_TILE == 0 and MXU_TILE % HEAD_DIM == 0
    assert b % ATTN_BATCHES == 0 and s % (ATTN_TILES * TQ) == 0

    idx = np.arange(TK)
    tri = jnp.asarray(idx[:, None] > idx[None, :], BF16)
    causal = np.where(idx[None, :] < idx[:, None], 0.0, MASKED_SCORE).astype(np.float32)
    per_head = lambda m: np.tile(m, (GROUP_HEADS, 1))
    bias_a = jnp.asarray(per_head(causal[:TQ // 2, :TK // 2]))
    bias_bc = jnp.asarray(np.concatenate([per_head(causal[TQ // 2:3 * TQ // 4]),
                                          per_head(causal[3 * TQ // 4:])], axis=0))
    gmat = _group_matrix()
    row = lambda g: g.reshape(1, -1)

    h = x
    for i in range(depth):
        (conv, q, k, v), (wo, wu, wd, wg, wp) = _mixer_in(
            h, row(g_mix[i]), w_in[i], row(conv_w[i]), row(g_conv_out[i]), gmat, w_conv, w_attn,
            [w_out[i], w_up[i], w_down[i], w_ple_gate[i], w_ple_proj[i]])
        attn = _attention(q, k, v, tri, bias_a, bias_bc, row(g_attn_out[i]), gmat)
        h = _post(h.reshape(b * s, d), conv.reshape(b * s, w_conv), attn.reshape(b * s, w_attn),
                  p[i].reshape(b * s, -1), wo, row(g_mlp[i]), wu, wd, row(g_ple[i]), wg, wp,
                  row(g_final), apply_final=(i == depth - 1)).reshape(b, s, d)
    return h
```

```python
import functools

import jax
import jax.numpy as jnp
import numpy as np
from jax import lax
from jax.experimental import pallas as pl
from jax.experimental.pallas import tpu as pltpu

HEAD_DIM = 64
CONV_K = 3
EPS = 1e-6
LOG2_E = 1.4426950408889634
UNDERFLOW_BITS = 150.0
MASKED_SCORE = -1e30

F32 = jnp.float32
BF16 = jnp.bfloat16

V7X_VMEM_BYTES = 64 * 1024 * 1024
VMEM_LIMIT_BYTES = V7X_VMEM_BYTES - 8 * 1024 * 1024
ATTN_VMEM_LIMIT_BYTES = V7X_VMEM_BYTES - 2 * 1024 * 1024
SUBLANES = 8
BF16_SUBLANES = 16
MXU_TILE = 256

TM_IN = 1024
IN_SUB_ROWS = 256
TQ = 256
TK = 256
GROUP_HEADS = 4
ATTN_BATCHES = 2
ATTN_TILES = 2
TM_POST = 1024
POST_SUB_ROWS = 256
POST_GROUP = 2


def _rms_scale(x):
    return lax.rsqrt(jnp.mean(x * x, axis=-1, keepdims=True) + EPS)


def _group_mean(sq, gmat):
    sqb = sq.astype(BF16)
    return jnp.concatenate(
        [jnp.dot(sqb[:, l0:l0 + MXU_TILE], gmat, preferred_element_type=F32)
         for l0 in range(0, sq.shape[1], MXU_TILE)], axis=1)


def _mixer_in_kernel(x_ref, g_ref, w32_ref, cw_ref, gco_ref, gmat_ref, *rest, w_conv, w_attn,
                     n_side):
    side_in, rest = rest[:n_side], rest[n_side:]
    conv_ref, q_ref, k_ref, v_ref = rest[:4]
    side_out = rest[4:4 + n_side]
    vbuf, w_ref = rest[4 + n_side:]
    tm = x_ref.shape[1]

    @pl.when((pl.program_id(0) == 0) & (pl.program_id(1) == 0))
    def _():
        for r0 in range(0, w32_ref.shape[0], IN_SUB_ROWS):
            w_ref[r0:r0 + IN_SUB_ROWS, :] = w32_ref[r0:r0 + IN_SUB_ROWS, :].astype(BF16)

    for src, dst in zip(side_in, side_out):
        dst[...] = src[...].astype(BF16)

    @pl.when(pl.program_id(1) == 0)
    def _():
        vbuf[0:SUBLANES, :] = jnp.zeros((SUBLANES, w_conv), F32)

    starts = list(range(0, tm, IN_SUB_ROWS))
    a = []
    for r0 in starts:
        x = x_ref[0, r0:r0 + IN_SUB_ROWS, :]
        a.append((x * _rms_scale(x) * g_ref[...]).astype(BF16))
    proj = [jnp.dot(as_, w_ref[...], preferred_element_type=F32) for as_ in a]

    o = 3 * w_conv
    for r0, pr in zip(starts, proj):
        rs = slice(r0, r0 + IN_SUB_ROWS)
        vbuf[SUBLANES + r0:SUBLANES + r0 + IN_SUB_ROWS, :] = (
            pr[:, w_conv:2 * w_conv] * pr[:, 2 * w_conv:3 * w_conv])
        q_ref[0, rs, :] = (pr[:, o:o + w_attn] * (HEAD_DIM ** -0.5 * LOG2_E)).astype(BF16)
        k_ref[0, rs, :] = pr[:, o + w_attn:o + 2 * w_attn].astype(BF16)
        v_ref[0, rs, :] = pr[:, o + 2 * w_attn:o + 3 * w_attn].astype(BF16)

    cw = [cw_ref[:, j * w_conv:(j + 1) * w_conv] for j in range(CONV_K)]
    for r0, pr in zip(starts, proj):
        v0 = vbuf[SUBLANES + r0:SUBLANES + r0 + IN_SUB_ROWS, :]
        v1 = vbuf[SUBLANES - 1 + r0:SUBLANES - 1 + r0 + IN_SUB_ROWS, :]
        v2 = vbuf[SUBLANES - 2 + r0:SUBLANES - 2 + r0 + IN_SUB_ROWS, :]
        c = pr[:, 0:w_conv] * (cw[0] * v2 + cw[1] * v1 + cw[2] * v0)
        ms = _group_mean(c * c, gmat_ref[...])
        conv_ref[0, r0:r0 + IN_SUB_ROWS, :] = (
            c * lax.rsqrt(ms + EPS) * gco_ref[...]).astype(BF16)
    vbuf[0:SUBLANES, :] = vbuf[tm:tm + SUBLANES, :]


def _mixer_in(x, g_mix, w_in, conv_w, g_conv_out, gmat, w_conv, w_attn, side_weights):
    b, s, d = x.shape
    tm = TM_IN
    n_i = s // tm
    steps = b * n_i
    const = lambda *_: (0, 0)
    tile = lambda bi, i: (bi, i, 0)
    out_sd = lambda w: jax.ShapeDtypeStruct((b, s, w), BF16)
    side_rows = [w.shape[0] // steps for w in side_weights]
    assert all(w.shape[0] % steps == 0 and r % BF16_SUBLANES == 0
               for w, r in zip(side_weights, side_rows))
    side_specs = [pl.BlockSpec((r, w.shape[1]), lambda bi, i: (bi * n_i + i, 0))
                  for w, r in zip(side_weights, side_rows)]
    outs = pl.pallas_call(
        functools.partial(_mixer_in_kernel, w_conv=w_conv, w_attn=w_attn,
                          n_side=len(side_weights)),
        grid=(b, n_i),
        in_specs=[
            pl.BlockSpec((1, tm, d), tile),
            pl.BlockSpec((1, d), const),
            pl.BlockSpec(w_in.shape, const, pipeline_mode=pl.Buffered(1)),
            pl.BlockSpec((1, CONV_K * w_conv), const),
            pl.BlockSpec((1, w_conv), const),
            pl.BlockSpec(gmat.shape, const),
        ] + side_specs,
        out_specs=[
            pl.BlockSpec((1, tm, w_conv), tile),
            pl.BlockSpec((1, tm, w_attn), tile),
            pl.BlockSpec((1, tm, w_attn), tile),
            pl.BlockSpec((1, tm, w_attn), tile),
        ] + side_specs,
        out_shape=[out_sd(w_conv), out_sd(w_attn), out_sd(w_attn), out_sd(w_attn)]
        + [jax.ShapeDtypeStruct(w.shape, BF16) for w in side_weights],
        scratch_shapes=[pltpu.VMEM((tm + SUBLANES, w_conv), F32),
                        pltpu.VMEM(w_in.shape, BF16)],
        compiler_params=pltpu.CompilerParams(
            dimension_semantics=("arbitrary", "arbitrary"),
            vmem_limit_bytes=VMEM_LIMIT_BYTES),
        name="mixer_in",
    )(x, g_mix, w_in, conv_w, g_conv_out, gmat, *side_weights)
    return outs[:4], outs[4:]


def _attn_kernel(q_ref, k_hbm, v_hbm, tri_ref, bias_a_ref, bias_bc_ref, g_ref, gmat_ref,
                 o_ref, qst_ref, acc_ref, carry_ref, k_ref, v_ref, kv_sem):
    nb, tq = q_ref.shape[0], TQ
    tiles = q_ref.shape[1] // tq
    tk = tri_ref.shape[0]
    chains, rows, gw = qst_ref.shape
    groups = chains // (tiles * nb)
    pair = 2 * HEAD_DIM
    where = [(c // (nb * groups), (c // groups) % nb, (c % groups) * gw) for c in range(chains)]
    tile_chains = [[c for c in range(chains) if where[c][0] == t] for t in range(tiles)]
    tile_i = [tiles * pl.program_id(1) + t for t in range(tiles)]
    segs = []
    for q0, qn in ((0, tq // 2), (tq // 2, tq // 4), (3 * tq // 4, tq // 4)):
        segs.append((q0, qn, slice(GROUP_HEADS * q0, GROUP_HEADS * (q0 + qn))))
    (_, _, ra), (_, _, rb), (_, _, rc) = segs
    rab, rbc, full = slice(0, rb.stop), slice(rb.start, rows), slice(0, rows)

    step, last_step = pl.program_id(1), pl.num_programs(1) - 1
    step_keys = tiles * tq
    batches = pl.ds(pl.program_id(0) * nb, nb)

    def kv_copies(j):
        keys = pl.ds(pl.multiple_of(j * step_keys, step_keys), step_keys)
        return [pltpu.make_async_copy(src.at[batches, keys, :], dst.at[:, keys, :], kv_sem.at[n])
                for n, (src, dst) in enumerate(((k_hbm, k_ref), (v_hbm, v_ref)))]

    @pl.when(step == 0)
    def _():
        for copy in kv_copies(0):
            copy.start()
        for copy in kv_copies(0):
            copy.wait()

    @pl.when(step < last_step)
    def _():
        for copy in kv_copies(step + 1):
            copy.start()

    for c, (t, bl, l0) in enumerate(where):
        for q0, qn, rs in segs:
            q = q_ref[bl, t * tq + q0:t * tq + q0 + qn, l0:l0 + gw]
            lane = lax.broadcasted_iota(jnp.int32, (qn, gw), 1)
            for hh in range(GROUP_HEADS):
                in_head = (lane >= hh * HEAD_DIM) & (lane < (hh + 1) * HEAD_DIM)
                r0 = rs.start + hh * qn
                qst_ref[c, r0:r0 + qn, :] = jnp.where(in_head, q, jnp.zeros_like(q))

    def scores(c, rs, start, nk):
        _, bl, l0 = where[c]
        kj = k_ref[bl, pl.ds(start, nk), l0:l0 + gw]
        return lax.dot_general(qst_ref[c, rs, :], kj, (((1,), (1,)), ((), ())),
                               preferred_element_type=F32)

    def weights(z, carry):
        nk = z.shape[1]
        sp = jnp.maximum(z, 0.0) + jnp.log(1.0 + jnp.exp2(-jnp.abs(z))) * LOG2_E
        cs = jnp.dot(sp.astype(BF16), tri_ref[0:nk, 0:nk],
                     preferred_element_type=F32)
        total = cs[:, 0:1] + sp[:, 0:1]
        if carry is None:
            return jnp.exp2(z - (sp + cs)).astype(BF16), total
        return jnp.exp2(z - (sp + cs + carry)).astype(BF16), carry + total

    def accumulate(c, rs, ab, start, assign=False):
        _, bl, l0 = where[c]
        nk = ab.shape[1]
        for _, qn, seg in segs:
            if seg.start < rs.start or seg.stop > rs.stop:
                continue
            half = (seg.stop - seg.start) // 2
            for p in range(GROUP_HEADS // 2):
                vj = v_ref[bl, pl.ds(start, nk), l0 + p * pair:l0 + (p + 1) * pair]
                src = seg.start - rs.start + p * half
                av = jnp.dot(ab[src:src + half, :], vj, preferred_element_type=F32)
                dst = slice(seg.start + p * half, seg.start + (p + 1) * half)
                if assign:
                    acc_ref[c, dst, :] = av
                else:
                    acc_ref[c, dst, :] += av

    d0 = [pl.multiple_of(i * tk, tk) for i in tile_i]
    p0 = [pl.multiple_of(jnp.maximum(i - 1, 0) * tk, tk) for i in tile_i]
    no_prev = [jnp.where(i == 0, -MASKED_SCORE, 0.0).astype(F32) for i in tile_i]
    tile_of = [where[c][0] for c in range(chains)]
    z_da = [scores(c, ra, d0[tile_of[c]], tk // 2) + bias_a_ref[...] for c in range(chains)]
    z_dbc = [scores(c, rbc, d0[tile_of[c]], tk) + bias_bc_ref[...] for c in range(chains)]
    z_pab = [scores(c, rab, p0[tile_of[c]], tk) for c in range(chains)]
    low_c = []
    nb_rows = rb.stop - rb.start
    for c in range(chains):
        t = tile_of[c]
        ab, cr_a = weights(z_da[c], None)
        accumulate(c, ra, ab, d0[t], assign=True)
        ab, cr_bc = weights(z_dbc[c], None)
        accumulate(c, rbc, ab, d0[t], assign=True)
        cr_bc = cr_bc + no_prev[t]
        carry_ref[c, rc, :] = cr_bc[nb_rows:]
        low_c.append(jnp.min(cr_bc[nb_rows:]))
        carry_ab = jnp.concatenate([cr_a + no_prev[t], cr_bc[0:nb_rows]], axis=0)
        ab, cr = weights(z_pab[c], carry_ab)
        accumulate(c, rab, ab, p0[t])
        carry_ref[c, rab, :] = cr

    @pl.when(functools.reduce(jnp.minimum, low_c) < UNDERFLOW_BITS)
    def _():
        for c in range(chains):
            start = p0[tile_of[c]]
            ab, cr = weights(scores(c, rc, start, tk), carry_ref[c, rc, :])
            accumulate(c, rc, ab, start)
            carry_ref[c, rc, :] = cr

    for t in range(tiles):
        i = tile_i[t]

        def lowest_carry():
            return functools.reduce(jnp.minimum,
                                    [jnp.min(carry_ref[c]) for c in tile_chains[t]])

        def body(state):
            n, _ = state
            start = pl.multiple_of((i - 1 - n) * tk, tk)
            for c in tile_chains[t]:
                ab, cr = weights(scores(c, full, start, tk), carry_ref[c])
                accumulate(c, full, ab, start)
                carry_ref[c] = cr
            return n + 1, lowest_carry()

        def more(state):
            n, low = state
            return (n < i) & (low < UNDERFLOW_BITS)

        lax.while_loop(more, body, (jnp.int32(1), lowest_carry()))

    for t in range(tiles):
        for bl in range(nb):
            for q0, qn, rs in segs:
                lane_p = lax.broadcasted_iota(jnp.int32, (qn, pair), 1)
                def head_rows(c, hh):
                    r0 = rs.start + hh * qn
                    return acc_ref[c, r0:r0 + qn, :]
                group_chains = [c for c in tile_chains[t] if where[c][1] == bl]
                out = jnp.concatenate(
                    [jnp.where(lane_p < HEAD_DIM, head_rows(c, 2 * p), head_rows(c, 2 * p + 1))
                     for c in group_chains for p in range(GROUP_HEADS // 2)], axis=1)
                ms = _group_mean(out * out, gmat_ref[...])
                o_ref[bl, t * tq + q0:t * tq + q0 + qn, :] = (
                    out * lax.rsqrt(ms + EPS) * g_ref[...]).astype(BF16)

    @pl.when(step < last_step)
    def _():
        for copy in kv_copies(step + 1):
            copy.wait()


def _attention(q, k, v, tri, bias_a, bias_bc, g_attn_out, gmat):
    b, s, w = q.shape
    nb = ATTN_BATCHES
    gw = GROUP_HEADS * HEAD_DIM
    tq = ATTN_TILES * TQ
    chains = ATTN_TILES * nb * (w // gw)
    rows = GROUP_HEADS * TQ
    const = lambda bi, i: (0, 0)
    return pl.pallas_call(
        _attn_kernel,
        grid=(b // nb, s // tq),
        in_specs=[
            pl.BlockSpec((nb, tq, w), lambda bi, i: (bi, i, 0)),
            pl.BlockSpec(memory_space=pl.ANY),
            pl.BlockSpec(memory_space=pl.ANY),
            pl.BlockSpec(tri.shape, const),
            pl.BlockSpec(bias_a.shape, const),
            pl.BlockSpec(bias_bc.shape, const),
            pl.BlockSpec((1, w), const),
            pl.BlockSpec(gmat.shape, const),
        ],
        out_specs=pl.BlockSpec((nb, tq, w), lambda bi, i: (bi, i, 0)),
        out_shape=jax.ShapeDtypeStruct((b, s, w), BF16),
        scratch_shapes=[pltpu.VMEM((chains, rows, gw), BF16),
                        pltpu.VMEM((chains, rows, 2 * HEAD_DIM), F32),
                        pltpu.VMEM((chains, rows, 1), F32),
                        pltpu.VMEM((nb, s, w), BF16),
                        pltpu.VMEM((nb, s, w), BF16),
                        pltpu.SemaphoreType.DMA((2,))],
        compiler_params=pltpu.CompilerParams(
            dimension_semantics=("arbitrary", "arbitrary"),
            vmem_limit_bytes=ATTN_VMEM_LIMIT_BYTES),
        name="sb_attention",
    )(q, k, v, tri, bias_a, bias_bc, g_attn_out, gmat)


def _post_kernel(x_ref, conv_ref, attn_ref, p_ref, wo_ref, gmlp_ref, wup_ref, wdn_ref,
                 gple_ref, wg_ref, wp_ref, gfin_ref, o_ref, *, apply_final):
    w_conv = conv_ref.shape[1]
    dot = functools.partial(jnp.dot, preferred_element_type=F32)
    group_rows = POST_GROUP * POST_SUB_ROWS
    for g0 in range(0, x_ref.shape[0], group_rows):
        subs = [slice(r0, r0 + POST_SUB_ROWS) for r0 in range(g0, g0 + group_rows, POST_SUB_ROWS)]
        h = [x_ref[rs, :] + dot(conv_ref[rs, :], wo_ref[0:w_conv, :]) for rs in subs]
        h = [hs + dot(attn_ref[rs, :], wo_ref[w_conv:, :]) for hs, rs in zip(h, subs)]
        m = [(hs * _rms_scale(hs) * gmlp_ref[...]).astype(BF16) for hs in h]
        r = [jnp.square(jnp.maximum(dot(ms, wup_ref[...]), 0.0)).astype(BF16) for ms in m]
        h = [hs + dot(rr, wdn_ref[...]) for hs, rr in zip(h, r)]
        n = [(hs * _rms_scale(hs) * gple_ref[...]).astype(BF16) for hs in h]
        pp = [dot(p_ref[rs, :].astype(BF16), wp_ref[...]) for rs in subs]
        gate = [jax.nn.sigmoid(dot(ns, wg_ref[...])) for ns in n]
        h = [hs + gs * ps for hs, gs, ps in zip(h, gate, pp)]
        for hs, rs in zip(h, subs):
            if apply_final:
                hs = hs * _rms_scale(hs) * gfin_ref[...]
            o_ref[rs, :] = hs


def _post(x2, conv2, attn2, p2, w_out, g_mlp, w_up, w_down, g_ple, w_gate, w_proj, g_final,
          apply_final):
    t, d = x2.shape
    tm = TM_POST
    tile = lambda i: (i, 0)
    const = lambda i: (0, 0)
    wspec = lambda w: pl.BlockSpec(w.shape, const, pipeline_mode=pl.Buffered(1))
    gspec = pl.BlockSpec((1, d), const)
    return pl.pallas_call(
        functools.partial(_post_kernel, apply_final=apply_final),
        grid=(t // tm,),
        in_specs=[
            pl.BlockSpec((tm, d), tile),
            pl.BlockSpec((tm, conv2.shape[1]), tile),
            pl.BlockSpec((tm, attn2.shape[1]), tile),
            pl.BlockSpec((tm, p2.shape[1]), tile),
            wspec(w_out), gspec, wspec(w_up), wspec(w_down), gspec, wspec(w_gate), wspec(w_proj),
            gspec,
        ],
        out_specs=pl.BlockSpec((tm, d), tile),
        out_shape=jax.ShapeDtypeStruct((t, d), F32),
        compiler_params=pltpu.CompilerParams(
            dimension_semantics=("arbitrary",),
            vmem_limit_bytes=ATTN_VMEM_LIMIT_BYTES),
        name="post",
    )(x2, conv2, attn2, p2, w_out, g_mlp, w_up, w_down, g_ple, w_gate, w_proj, g_final)


def _group_matrix():
    g = np.arange(MXU_TILE) // HEAD_DIM
    return jnp.asarray(np.where(g[:, None] == g[None, :], 1.0 / HEAD_DIM, 0.0), BF16)


def kernel(x, p, g_mix, w_in, conv_w, g_conv_out, g_attn_out, w_out, g_mlp, w_up, w_down,
           g_ple, w_ple_gate, w_ple_proj, g_final):
    b, s, d = x.shape
    depth = p.shape[0]
    w_conv = conv_w.shape[-1]
    w_attn = g_attn_out.shape[-1]
    assert s % TM_IN == 0 and s % TQ == 0 and TQ == TK and (b * s) % TM_POST == 0
    assert w_attn % (GROUP_HEADS * HEAD_DIM) == 0 and w_conv % MXU_TILE == 0
    assert w_attn % MXU_TILE == 0 and MXU_TILE % HEAD_DIM == 0
    assert b % ATTN_BATCHES == 0 and s % (ATTN_TILES * TQ) == 0

    idx = np.arange(TK)
    tri = jnp.asarray(idx[:, None] > idx[None, :], BF16)
    causal = np.where(idx[None, :] < idx[:, None], 0.0, MASKED_SCORE).astype(np.float32)
    per_head = lambda m: np.tile(m, (GROUP_HEADS, 1))
    bias_a = jnp.asarray(per_head(causal[:TQ // 2, :TK // 2]))
    bias_bc = jnp.asarray(np.concatenate([per_head(causal[TQ // 2:3 * TQ // 4]),
                                          per_head(causal[3 * TQ // 4:])], axis=0))
    gmat = _group_matrix()
    row = lambda g: g.reshape(1, -1)

    h = x
    for i in range(depth):
        (conv, q, k, v), (wo, wu, wd, wg, wp) = _mixer_in(
            h, row(g_mix[i]), w_in[i], row(conv_w[i]), row(g_conv_out[i]), gmat, w_conv, w_attn,
            [w_out[i], w_up[i], w_down[i], w_ple_gate[i], w_ple_proj[i]])
        attn = _attention(q, k, v, tri, bias_a, bias_bc, row(g_attn_out[i]), gmat)
        h = _post(h.reshape(b * s, d), conv.reshape(b * s, w_conv), attn.reshape(b * s, w_attn),
                  p[i].reshape(b * s, -1), wo, row(g_mlp[i]), wu, wd, row(g_ple[i]), wg, wp,
                  row(g_final), apply_final=(i == depth - 1)).reshape(b, s, d)
    return h
```

```python
import functools

import jax
import jax.numpy as jnp
import numpy as np
from jax import lax
from jax.experimental import pallas as pl
from jax.experimental.pallas import tpu as pltpu

HEAD_DIM = 64
CONV_K = 3
EPS = 1e-6
LOG2_E = 1.4426950408889634
UNDERFLOW_BITS = 150.0
MASKED_SCORE = -1e30

F32 = jnp.float32
BF16 = jnp.bfloat16

V7X_VMEM_BYTES = 64 * 1024 * 1024
VMEM_LIMIT_BYTES = V7X_VMEM_BYTES - 8 * 1024 * 1024
ATTN_VMEM_LIMIT_BYTES = V7X_VMEM_BYTES - 2 * 1024 * 1024
SUBLANES = 8
BF16_SUBLANES = 16
MXU_TILE = 256

TM_IN = 1024
IN_SUB_ROWS = 256
TQ = 256
TK = 256
GROUP_HEADS = 4
ATTN_BATCHES = 2
ATTN_TILES = 2
TM_POST = 512
POST_SUB_ROWS = 256


def _rms_scale(x):
    return lax.rsqrt(jnp.mean(x * x, axis=-1, keepdims=True) + EPS)


def _group_mean(sq, gmat):
    sqb = sq.astype(BF16)
    return jnp.concatenate(
        [jnp.dot(sqb[:, l0:l0 + MXU_TILE], gmat, preferred_element_type=F32)
         for l0 in range(0, sq.shape[1], MXU_TILE)], axis=1)


def _mixer_in_kernel(x_ref, g_ref, w32_ref, cw_ref, gco_ref, gmat_ref, *rest, w_conv, w_attn,
                     n_side):
    side_in, rest = rest[:n_side], rest[n_side:]
    conv_ref, q_ref, k_ref, v_ref = rest[:4]
    side_out = rest[4:4 + n_side]
    vbuf, w_ref = rest[4 + n_side:]
    tm = x_ref.shape[1]

    @pl.when((pl.program_id(0) == 0) & (pl.program_id(1) == 0))
    def _():
        for r0 in range(0, w32_ref.shape[0], IN_SUB_ROWS):
            w_ref[r0:r0 + IN_SUB_ROWS, :] = w32_ref[r0:r0 + IN_SUB_ROWS, :].astype(BF16)

    for src, dst in zip(side_in, side_out):
        dst[...] = src[...].astype(BF16)

    @pl.when(pl.program_id(1) == 0)
    def _():
        vbuf[0:SUBLANES, :] = jnp.zeros((SUBLANES, w_conv), F32)

    starts = list(range(0, tm, IN_SUB_ROWS))
    a = []
    for r0 in starts:
        x = x_ref[0, r0:r0 + IN_SUB_ROWS, :]
        a.append((x * _rms_scale(x) * g_ref[...]).astype(BF16))
    proj = [jnp.dot(as_, w_ref[...], preferred_element_type=F32) for as_ in a]

    o = 3 * w_conv
    for r0, pr in zip(starts, proj):
        rs = slice(r0, r0 + IN_SUB_ROWS)
        vbuf[SUBLANES + r0:SUBLANES + r0 + IN_SUB_ROWS, :] = (
            pr[:, w_conv:2 * w_conv] * pr[:, 2 * w_conv:3 * w_conv])
        q_ref[0, rs, :] = (pr[:, o:o + w_attn] * (HEAD_DIM ** -0.5 * LOG2_E)).astype(BF16)
        k_ref[0, rs, :] = pr[:, o + w_attn:o + 2 * w_attn].astype(BF16)
        v_ref[0, rs, :] = pr[:, o + 2 * w_attn:o + 3 * w_attn].astype(BF16)

    cw = [cw_ref[:, j * w_conv:(j + 1) * w_conv] for j in range(CONV_K)]
    for r0, pr in zip(starts, proj):
        v0 = vbuf[SUBLANES + r0:SUBLANES + r0 + IN_SUB_ROWS, :]
        v1 = vbuf[SUBLANES - 1 + r0:SUBLANES - 1 + r0 + IN_SUB_ROWS, :]
        v2 = vbuf[SUBLANES - 2 + r0:SUBLANES - 2 + r0 + IN_SUB_ROWS, :]
        c = pr[:, 0:w_conv] * (cw[0] * v2 + cw[1] * v1 + cw[2] * v0)
        ms = _group_mean(c * c, gmat_ref[...])
        conv_ref[0, r0:r0 + IN_SUB_ROWS, :] = (
            c * lax.rsqrt(ms + EPS) * gco_ref[...]).astype(BF16)
    vbuf[0:SUBLANES, :] = vbuf[tm:tm + SUBLANES, :]


def _mixer_in(x, g_mix, w_in, conv_w, g_conv_out, gmat, w_conv, w_attn, side_weights):
    b, s, d = x.shape
    tm = TM_IN
    n_i = s // tm
    steps = b * n_i
    const = lambda *_: (0, 0)
    tile = lambda bi, i: (bi, i, 0)
    out_sd = lambda w: jax.ShapeDtypeStruct((b, s, w), BF16)
    side_rows = [w.shape[0] // steps for w in side_weights]
    assert all(w.shape[0] % steps == 0 and r % BF16_SUBLANES == 0
               for w, r in zip(side_weights, side_rows))
    side_specs = [pl.BlockSpec((r, w.shape[1]), lambda bi, i: (bi * n_i + i, 0))
                  for w, r in zip(side_weights, side_rows)]
    outs = pl.pallas_call(
        functools.partial(_mixer_in_kernel, w_conv=w_conv, w_attn=w_attn,
                          n_side=len(side_weights)),
        grid=(b, n_i),
        in_specs=[
            pl.BlockSpec((1, tm, d), tile),
            pl.BlockSpec((1, d), const),
            pl.BlockSpec(w_in.shape, const, pipeline_mode=pl.Buffered(1)),
            pl.BlockSpec((1, CONV_K * w_conv), const),
            pl.BlockSpec((1, w_conv), const),
            pl.BlockSpec(gmat.shape, const),
        ] + side_specs,
        out_specs=[
            pl.BlockSpec((1, tm, w_conv), tile),
            pl.BlockSpec((1, tm, w_attn), tile),
            pl.BlockSpec((1, tm, w_attn), tile),
            pl.BlockSpec((1, tm, w_attn), tile),
        ] + side_specs,
        out_shape=[out_sd(w_conv), out_sd(w_attn), out_sd(w_attn), out_sd(w_attn)]
        + [jax.ShapeDtypeStruct(w.shape, BF16) for w in side_weights],
        scratch_shapes=[pltpu.VMEM((tm + SUBLANES, w_conv), F32),
                        pltpu.VMEM(w_in.shape, BF16)],
        compiler_params=pltpu.CompilerParams(
            dimension_semantics=("arbitrary", "arbitrary"),
            vmem_limit_bytes=VMEM_LIMIT_BYTES),
        name="mixer_in",
    )(x, g_mix, w_in, conv_w, g_conv_out, gmat, *side_weights)
    return outs[:4], outs[4:]


def _attn_kernel(q_ref, k_hbm, v_hbm, tri_ref, bias_a_ref, bias_bc_ref, g_ref, gmat_ref,
                 o_ref, qst_ref, acc_ref, carry_ref, k_ref, v_ref, kv_sem):
    nb, tq = q_ref.shape[0], TQ
    tiles = q_ref.shape[1] // tq
    tk = tri_ref.shape[0]
    chains, rows, gw = qst_ref.shape
    groups = chains // (tiles * nb)
    pair = 2 * HEAD_DIM
    where = [(c // (nb * groups), (c // groups) % nb, (c % groups) * gw) for c in range(chains)]
    tile_chains = [[c for c in range(chains) if where[c][0] == t] for t in range(tiles)]
    tile_i = [tiles * pl.program_id(1) + t for t in range(tiles)]
    segs = []
    for q0, qn in ((0, tq // 2), (tq // 2, tq // 4), (3 * tq // 4, tq // 4)):
        segs.append((q0, qn, slice(GROUP_HEADS * q0, GROUP_HEADS * (q0 + qn))))
    (_, _, ra), (_, _, rb), (_, _, rc) = segs
    rab, rbc, full = slice(0, rb.stop), slice(rb.start, rows), slice(0, rows)

    step, last_step = pl.program_id(1), pl.num_programs(1) - 1
    step_keys = tiles * tq
    batches = pl.ds(pl.program_id(0) * nb, nb)

    def kv_copies(j):
        keys = pl.ds(pl.multiple_of(j * step_keys, step_keys), step_keys)
        return [pltpu.make_async_copy(src.at[batches, keys, :], dst.at[:, keys, :], kv_sem.at[n])
                for n, (src, dst) in enumerate(((k_hbm, k_ref), (v_hbm, v_ref)))]

    @pl.when(step == 0)
    def _():
        for copy in kv_copies(0):
            copy.start()
        for copy in kv_copies(0):
            copy.wait()

    @pl.when(step < last_step)
    def _():
        for copy in kv_copies(step + 1):
            copy.start()

    for c, (t, bl, l0) in enumerate(where):
        for q0, qn, rs in segs:
            q = q_ref[bl, t * tq + q0:t * tq + q0 + qn, l0:l0 + gw]
            lane = lax.broadcasted_iota(jnp.int32, (qn, gw), 1)
            for hh in range(GROUP_HEADS):
                in_head = (lane >= hh * HEAD_DIM) & (lane < (hh + 1) * HEAD_DIM)
                r0 = rs.start + hh * qn
                qst_ref[c, r0:r0 + qn, :] = jnp.where(in_head, q, jnp.zeros_like(q))

    def scores(c, rs, start, nk):
        _, bl, l0 = where[c]
        kj = k_ref[bl, pl.ds(start, nk), l0:l0 + gw]
        return lax.dot_general(qst_ref[c, rs, :], kj, (((1,), (1,)), ((), ())),
                               preferred_element_type=F32)

    def weights(z, carry):
        nk = z.shape[1]
        sp = jnp.maximum(z, 0.0) + jnp.log(1.0 + jnp.exp2(-jnp.abs(z))) * LOG2_E
        cs = jnp.dot(sp.astype(BF16), tri_ref[0:nk, 0:nk],
                     preferred_element_type=F32)
        total = cs[:, 0:1] + sp[:, 0:1]
        if carry is None:
            return jnp.exp2(z - (sp + cs)).astype(BF16), total
        return jnp.exp2(z - (sp + cs + carry)).astype(BF16), carry + total

    def accumulate(c, rs, ab, start, assign=False):
        _, bl, l0 = where[c]
        nk = ab.shape[1]
        for _, qn, seg in segs:
            if seg.start < rs.start or seg.stop > rs.stop:
                continue
            half = (seg.stop - seg.start) // 2
            for p in range(GROUP_HEADS // 2):
                vj = v_ref[bl, pl.ds(start, nk), l0 + p * pair:l0 + (p + 1) * pair]
                src = seg.start - rs.start + p * half
                av = jnp.dot(ab[src:src + half, :], vj, preferred_element_type=F32)
                dst = slice(seg.start + p * half, seg.start + (p + 1) * half)
                if assign:
                    acc_ref[c, dst, :] = av
                else:
                    acc_ref[c, dst, :] += av

    d0 = [pl.multiple_of(i * tk, tk) for i in tile_i]
    p0 = [pl.multiple_of(jnp.maximum(i - 1, 0) * tk, tk) for i in tile_i]
    no_prev = [jnp.where(i == 0, -MASKED_SCORE, 0.0).astype(F32) for i in tile_i]
    tile_of = [where[c][0] for c in range(chains)]
    z_da, z_dbc, z_pab = {}, {}, {}
    low_c = []
    nb_rows = rb.stop - rb.start
    for c in range(chains):
        t = tile_of[c]
        if c == tile_chains[t][0]:
            for cc in tile_chains[t]:
                z_da[cc] = scores(cc, ra, d0[t], tk // 2) + bias_a_ref[...]
            for cc in tile_chains[t]:
                z_dbc[cc] = scores(cc, rbc, d0[t], tk) + bias_bc_ref[...]
            for cc in tile_chains[t]:
                z_pab[cc] = scores(cc, rab, p0[t], tk)
        ab, cr_a = weights(z_da[c], None)
        accumulate(c, ra, ab, d0[t], assign=True)
        ab, cr_bc = weights(z_dbc[c], None)
        accumulate(c, rbc, ab, d0[t], assign=True)
        cr_bc = cr_bc + no_prev[t]
        carry_ref[c, rc, :] = cr_bc[nb_rows:]
        low_c.append(jnp.min(cr_bc[nb_rows:]))
        carry_ab = jnp.concatenate([cr_a + no_prev[t], cr_bc[0:nb_rows]], axis=0)
        ab, cr = weights(z_pab[c], carry_ab)
        accumulate(c, rab, ab, p0[t])
        carry_ref[c, rab, :] = cr

    @pl.when(functools.reduce(jnp.minimum, low_c) < UNDERFLOW_BITS)
    def _():
        for c in range(chains):
            start = p0[tile_of[c]]
            ab, cr = weights(scores(c, rc, start, tk), carry_ref[c, rc, :])
            accumulate(c, rc, ab, start)
            carry_ref[c, rc, :] = cr

    for t in range(tiles):
        i = tile_i[t]

        def lowest_carry():
            return functools.reduce(jnp.minimum,
                                    [jnp.min(carry_ref[c]) for c in tile_chains[t]])

        def body(state):
            n, _ = state
            start = pl.multiple_of((i - 1 - n) * tk, tk)
            for c in tile_chains[t]:
                ab, cr = weights(scores(c, full, start, tk), carry_ref[c])
                accumulate(c, full, ab, start)
                carry_ref[c] = cr
            return n + 1, lowest_carry()

        def more(state):
            n, low = state
            return (n < i) & (low < UNDERFLOW_BITS)

        lax.while_loop(more, body, (jnp.int32(1), lowest_carry()))

    for t in range(tiles):
        for bl in range(nb):
            for q0, qn, rs in segs:
                lane_p = lax.broadcasted_iota(jnp.int32, (qn, pair), 1)
                def head_rows(c, hh):
                    r0 = rs.start + hh * qn
                    return acc_ref[c, r0:r0 + qn, :]
                group_chains = [c for c in tile_chains[t] if where[c][1] == bl]
                out = jnp.concatenate(
                    [jnp.where(lane_p < HEAD_DIM, head_rows(c, 2 * p), head_rows(c, 2 * p + 1))
                     for c in group_chains for p in range(GROUP_HEADS // 2)], axis=1)
                ms = _group_mean(out * out, gmat_ref[...])
                o_ref[bl, t * tq + q0:t * tq + q0 + qn, :] = (
                    out * lax.rsqrt(ms + EPS) * g_ref[...]).astype(BF16)

    @pl.when(step < last_step)
    def _():
        for copy in kv_copies(step + 1):
            copy.wait()


def _attention(q, k, v, tri, bias_a, bias_bc, g_attn_out, gmat):
    b, s, w = q.shape
    nb = ATTN_BATCHES
    gw = GROUP_HEADS * HEAD_DIM
    tq = ATTN_TILES * TQ
    chains = ATTN_TILES * nb * (w // gw)
    rows = GROUP_HEADS * TQ
    const = lambda bi, i: (0, 0)
    return pl.pallas_call(
        _attn_kernel,
        grid=(b // nb, s // tq),
        in_specs=[
            pl.BlockSpec((nb, tq, w), lambda bi, i: (bi, i, 0)),
            pl.BlockSpec(memory_space=pl.ANY),
            pl.BlockSpec(memory_space=pl.ANY),
            pl.BlockSpec(tri.shape, const),
            pl.BlockSpec(bias_a.shape, const),
            pl.BlockSpec(bias_bc.shape, const),
            pl.BlockSpec((1, w), const),
            pl.BlockSpec(gmat.shape, const),
        ],
        out_specs=pl.BlockSpec((nb, tq, w), lambda bi, i: (bi, i, 0)),
        out_shape=jax.ShapeDtypeStruct((b, s, w), BF16),
        scratch_shapes=[pltpu.VMEM((chains, rows, gw), BF16),
                        pltpu.VMEM((chains, rows, 2 * HEAD_DIM), F32),
                        pltpu.VMEM((chains, rows, 1), F32),
                        pltpu.VMEM((nb, s, w), BF16),
                        pltpu.VMEM((nb, s, w), BF16),
                        pltpu.SemaphoreType.DMA((2,))],
        compiler_params=pltpu.CompilerParams(
            dimension_semantics=("arbitrary", "arbitrary"),
            vmem_limit_bytes=ATTN_VMEM_LIMIT_BYTES),
        name="sb_attention",
    )(q, k, v, tri, bias_a, bias_bc, g_attn_out, gmat)


def _post_kernel(x_ref, conv_ref, attn_ref, p_ref, wo_ref, gmlp_ref, wup_ref, wdn_ref,
                 gple_ref, wg_ref, wp_ref, gfin_ref, o_ref, *, apply_final):
    w_conv = conv_ref.shape[1]
    subs = [slice(r0, r0 + POST_SUB_ROWS) for r0 in range(0, x_ref.shape[0], POST_SUB_ROWS)]
    dot = functools.partial(jnp.dot, preferred_element_type=F32)
    h = [x_ref[rs, :] + dot(conv_ref[rs, :], wo_ref[0:w_conv, :]) for rs in subs]
    h = [hs + dot(attn_ref[rs, :], wo_ref[w_conv:, :]) for hs, rs in zip(h, subs)]
    m = [(hs * _rms_scale(hs) * gmlp_ref[...]).astype(BF16) for hs in h]
    r = [jnp.square(jnp.maximum(dot(ms, wup_ref[...]), 0.0)).astype(BF16) for ms in m]
    h = [hs + dot(rr, wdn_ref[...]) for hs, rr in zip(h, r)]
    n = [(hs * _rms_scale(hs) * gple_ref[...]).astype(BF16) for hs in h]
    pp = [dot(p_ref[rs, :].astype(BF16), wp_ref[...]) for rs in subs]
    gate = [jax.nn.sigmoid(dot(ns, wg_ref[...])) for ns in n]
    h = [hs + gs * ps for hs, gs, ps in zip(h, gate, pp)]
    for hs, rs in zip(h, subs):
        if apply_final:
            hs = hs * _rms_scale(hs) * gfin_ref[...]
        o_ref[rs, :] = hs


def _post(x2, conv2, attn2, p2, w_out, g_mlp, w_up, w_down, g_ple, w_gate, w_proj, g_final,
          apply_final):
    t, d = x2.shape
    tm = TM_POST
    tile = lambda i: (i, 0)
    const = lambda i: (0, 0)
    wspec = lambda w: pl.BlockSpec(w.shape, const, pipeline_mode=pl.Buffered(1))
    gspec = pl.BlockSpec((1, d), const)
    return pl.pallas_call(
        functools.partial(_post_kernel, apply_final=apply_final),
        grid=(t // tm,),
        in_specs=[
            pl.BlockSpec((tm, d), tile),
            pl.BlockSpec((tm, conv2.shape[1]), tile),
            pl.BlockSpec((tm, attn2.shape[1]), tile),
            pl.BlockSpec((tm, p2.shape[1]), tile),
            wspec(w_out), gspec, wspec(w_up), wspec(w_down), gspec, wspec(w_gate), wspec(w_proj),
            gspec,
        ],
        out_specs=pl.BlockSpec((tm, d), tile),
        out_shape=jax.ShapeDtypeStruct((t, d), F32),
        compiler_params=pltpu.CompilerParams(
            dimension_semantics=("arbitrary",),
            vmem_limit_bytes=VMEM_LIMIT_BYTES),
        name="post",
    )(x2, conv2, attn2, p2, w_out, g_mlp, w_up, w_down, g_ple, w_gate, w_proj, g_final)


def _group_matrix():
    g = np.arange(MXU_TILE) // HEAD_DIM
    return jnp.asarray(np.where(g[:, None] == g[None, :], 1.0 / HEAD_DIM, 0.0), BF16)


def kernel(x, p, g_mix, w_in, conv_w, g_conv_out, g_attn_out, w_out, g_mlp, w_up, w_down,
           g_ple, w_ple_gate, w_ple_proj, g_final):
    b, s, d = x.shape
    depth = p.shape[0]
    w_conv = conv_w.shape[-1]
    w_attn = g_attn_out.shape[-1]
    assert s % TM_IN == 0 and s % TQ == 0 and TQ == TK and (b * s) % TM_POST == 0
    assert w_attn % (GROUP_HEADS * HEAD_DIM) == 0 and w_conv % MXU_TILE == 0
    assert w_attn % MXU_TILE == 0 and MXU_TILE % HEAD_DIM == 0
    assert b % ATTN_BATCHES == 0 and s % (ATTN_TILES * TQ) == 0

    idx = np.arange(TK)
    tri = jnp.asarray(idx[:, None] > idx[None, :], BF16)
    causal = np.where(idx[None, :] < idx[:, None], 0.0, MASKED_SCORE).astype(np.float32)
    per_head = lambda m: np.tile(m, (GROUP_HEADS, 1))
    bias_a = jnp.asarray(per_head(causal[:TQ // 2, :TK // 2]))
    bias_bc = jnp.asarray(np.concatenate([per_head(causal[TQ // 2:3 * TQ // 4]),
                                          per_head(causal[3 * TQ // 4:])], axis=0))
    gmat = _group_matrix()
    row = lambda g: g.reshape(1, -1)

    h = x
    for i in range(depth):
        (conv, q, k, v), (wo, wu, wd, wg, wp) = _mixer_in(
            h, row(g_mix[i]), w_in[i], row(conv_w[i]), row(g_conv_out[i]), gmat, w_conv, w_attn,
            [w_out[i], w_up[i], w_down[i], w_ple_gate[i], w_ple_proj[i]])
        attn = _attention(q, k, v, tri, bias_a, bias_bc, row(g_attn_out[i]), gmat)
        h = _post(h.reshape(b * s, d), conv.reshape(b * s, w_conv), attn.reshape(b * s, w_attn),
                  p[i].reshape(b * s, -1), wo, row(g_mlp[i]), wu, wd, row(g_ple[i]), wg, wp,
                  row(g_final), apply_final=(i == depth - 1)).reshape(b, s, d)
    return h
```

```python
import functools

import jax
import jax.numpy as jnp
import numpy as np
from jax import lax
from jax.experimental import pallas as pl
from jax.experimental.pallas import tpu as pltpu

HEAD_DIM = 64
CONV_K = 3
EPS = 1e-6
LOG2_E = 1.4426950408889634
UNDERFLOW_BITS = 150.0
MASKED_SCORE = -1e30

F32 = jnp.float32
BF16 = jnp.bfloat16

V7X_VMEM_BYTES = 64 * 1024 * 1024
VMEM_LIMIT_BYTES = V7X_VMEM_BYTES - 8 * 1024 * 1024
ATTN_VMEM_LIMIT_BYTES = V7X_VMEM_BYTES - 2 * 1024 * 1024
SUBLANES = 8
BF16_SUBLANES = 16
MXU_TILE = 256

TM_IN = 1024
IN_SUB_ROWS = 256
TQ = 256
TK = 256
GROUP_HEADS = 4
ATTN_BATCHES = 2
ATTN_TILES = 2
TM_POST = 512
POST_SUB_ROWS = 256


def _rms_scale(x):
    return lax.rsqrt(jnp.mean(x * x, axis=-1, keepdims=True) + EPS)


def _group_mean(sq, gmat):
    sqb = sq.astype(BF16)
    return jnp.concatenate(
        [jnp.dot(sqb[:, l0:l0 + MXU_TILE], gmat, preferred_element_type=F32)
         for l0 in range(0, sq.shape[1], MXU_TILE)], axis=1)


def _mixer_in_kernel(x_ref, g_ref, w32_ref, cw_ref, gco_ref, gmat_ref, *rest, w_conv, w_attn,
                     n_side):
    side_in, rest = rest[:n_side], rest[n_side:]
    conv_ref, q_ref, k_ref, v_ref = rest[:4]
    side_out = rest[4:4 + n_side]
    vbuf, w_ref = rest[4 + n_side:]
    tm = x_ref.shape[1]

    @pl.when((pl.program_id(0) == 0) & (pl.program_id(1) == 0))
    def _():
        for r0 in range(0, w32_ref.shape[0], IN_SUB_ROWS):
            w_ref[r0:r0 + IN_SUB_ROWS, :] = w32_ref[r0:r0 + IN_SUB_ROWS, :].astype(BF16)

    for src, dst in zip(side_in, side_out):
        dst[...] = src[...].astype(BF16)

    @pl.when(pl.program_id(1) == 0)
    def _():
        vbuf[0:SUBLANES, :] = jnp.zeros((SUBLANES, w_conv), F32)

    starts = list(range(0, tm, IN_SUB_ROWS))
    a = []
    for r0 in starts:
        x = x_ref[0, r0:r0 + IN_SUB_ROWS, :]
        a.append((x * _rms_scale(x) * g_ref[...]).astype(BF16))
    proj = [jnp.dot(as_, w_ref[...], preferred_element_type=F32) for as_ in a]

    o = 3 * w_conv
    for r0, pr in zip(starts, proj):
        rs = slice(r0, r0 + IN_SUB_ROWS)
        vbuf[SUBLANES + r0:SUBLANES + r0 + IN_SUB_ROWS, :] = (
            pr[:, w_conv:2 * w_conv] * pr[:, 2 * w_conv:3 * w_conv])
        q_ref[0, rs, :] = (pr[:, o:o + w_attn] * (HEAD_DIM ** -0.5 * LOG2_E)).astype(BF16)
        k_ref[0, rs, :] = pr[:, o + w_attn:o + 2 * w_attn].astype(BF16)
        v_ref[0, rs, :] = pr[:, o + 2 * w_attn:o + 3 * w_attn].astype(BF16)

    cw = [cw_ref[:, j * w_conv:(j + 1) * w_conv] for j in range(CONV_K)]
    for r0, pr in zip(starts, proj):
        v0 = vbuf[SUBLANES + r0:SUBLANES + r0 + IN_SUB_ROWS, :]
        v1 = vbuf[SUBLANES - 1 + r0:SUBLANES - 1 + r0 + IN_SUB_ROWS, :]
        v2 = vbuf[SUBLANES - 2 + r0:SUBLANES - 2 + r0 + IN_SUB_ROWS, :]
        c = pr[:, 0:w_conv] * (cw[0] * v2 + cw[1] * v1 + cw[2] * v0)
        ms = _group_mean(c * c, gmat_ref[...])
        conv_ref[0, r0:r0 + IN_SUB_ROWS, :] = (
            c * lax.rsqrt(ms + EPS) * gco_ref[...]).astype(BF16)
    vbuf[0:SUBLANES, :] = vbuf[tm:tm + SUBLANES, :]


def _mixer_in(x, g_mix, w_in, conv_w, g_conv_out, gmat, w_conv, w_attn, side_weights):
    b, s, d = x.shape
    tm = TM_IN
    n_i = s // tm
    steps = b * n_i
    const = lambda *_: (0, 0)
    tile = lambda bi, i: (bi, i, 0)
    out_sd = lambda w: jax.ShapeDtypeStruct((b, s, w), BF16)
    side_rows = [w.shape[0] // steps for w in side_weights]
    assert all(w.shape[0] % steps == 0 and r % BF16_SUBLANES == 0
               for w, r in zip(side_weights, side_rows))
    side_specs = [pl.BlockSpec((r, w.shape[1]), lambda bi, i: (bi * n_i + i, 0))
                  for w, r in zip(side_weights, side_rows)]
    outs = pl.pallas_call(
        functools.partial(_mixer_in_kernel, w_conv=w_conv, w_attn=w_attn,
                          n_side=len(side_weights)),
        grid=(b, n_i),
        in_specs=[
            pl.BlockSpec((1, tm, d), tile),
            pl.BlockSpec((1, d), const),
            pl.BlockSpec(w_in.shape, const, pipeline_mode=pl.Buffered(1)),
            pl.BlockSpec((1, CONV_K * w_conv), const),
            pl.BlockSpec((1, w_conv), const),
            pl.BlockSpec(gmat.shape, const),
        ] + side_specs,
        out_specs=[
            pl.BlockSpec((1, tm, w_conv), tile),
            pl.BlockSpec((1, tm, w_attn), tile),
            pl.BlockSpec((1, tm, w_attn), tile),
            pl.BlockSpec((1, tm, w_attn), tile),
        ] + side_specs,
        out_shape=[out_sd(w_conv), out_sd(w_attn), out_sd(w_attn), out_sd(w_attn)]
        + [jax.ShapeDtypeStruct(w.shape, BF16) for w in side_weights],
        scratch_shapes=[pltpu.VMEM((tm + SUBLANES, w_conv), F32),
                        pltpu.VMEM(w_in.shape, BF16)],
        compiler_params=pltpu.CompilerParams(
            dimension_semantics=("arbitrary", "arbitrary"),
            vmem_limit_bytes=VMEM_LIMIT_BYTES),
        name="mixer_in",
    )(x, g_mix, w_in, conv_w, g_conv_out, gmat, *side_weights)
    return outs[:4], outs[4:]


def _attn_kernel(q_ref, k_hbm, v_hbm, tri_ref, bias_a_ref, bias_bc_ref, g_ref, gmat_ref,
                 o_ref, qst_ref, acc_ref, carry_ref, k_ref, v_ref, kv_sem):
    nb, tq = q_ref.shape[0], TQ
    tiles = q_ref.shape[1] // tq
    tk = tri_ref.shape[0]
    chains, rows, gw = qst_ref.shape
    groups = chains // (tiles * nb)
    pair = 2 * HEAD_DIM
    where = [(c // (nb * groups), (c // groups) % nb, (c % groups) * gw) for c in range(chains)]
    tile_chains = [[c for c in range(chains) if where[c][0] == t] for t in range(tiles)]
    tile_i = [tiles * pl.program_id(1) + t for t in range(tiles)]
    segs = []
    for q0, qn in ((0, tq // 2), (tq // 2, tq // 4), (3 * tq // 4, tq // 4)):
        segs.append((q0, qn, slice(GROUP_HEADS * q0, GROUP_HEADS * (q0 + qn))))
    (_, _, ra), (_, _, rb), (_, _, rc) = segs
    rab, rbc, full = slice(0, rb.stop), slice(rb.start, rows), slice(0, rows)

    step, last_step = pl.program_id(1), pl.num_programs(1) - 1
    step_keys = tiles * tq
    batches = pl.ds(pl.program_id(0) * nb, nb)

    def kv_copies(j):
        keys = pl.ds(pl.multiple_of(j * step_keys, step_keys), step_keys)
        return [pltpu.make_async_copy(src.at[batches, keys, :], dst.at[:, keys, :], kv_sem.at[n])
                for n, (src, dst) in enumerate(((k_hbm, k_ref), (v_hbm, v_ref)))]

    @pl.when(step == 0)
    def _():
        for copy in kv_copies(0):
            copy.start()
        for copy in kv_copies(0):
            copy.wait()

    @pl.when(step < last_step)
    def _():
        for copy in kv_copies(step + 1):
            copy.start()

    for c, (t, bl, l0) in enumerate(where):
        for q0, qn, rs in segs:
            q = q_ref[bl, t * tq + q0:t * tq + q0 + qn, l0:l0 + gw]
            lane = lax.broadcasted_iota(jnp.int32, (qn, gw), 1)
            for hh in range(GROUP_HEADS):
                in_head = (lane >= hh * HEAD_DIM) & (lane < (hh + 1) * HEAD_DIM)
                r0 = rs.start + hh * qn
                qst_ref[c, r0:r0 + qn, :] = jnp.where(in_head, q, jnp.zeros_like(q))

    def scores(c, rs, start, nk):
        _, bl, l0 = where[c]
        kj = k_ref[bl, pl.ds(start, nk), l0:l0 + gw]
        return lax.dot_general(qst_ref[c, rs, :], kj, (((1,), (1,)), ((), ())),
                               preferred_element_type=F32)

    def weights(z, carry):
        nk = z.shape[1]
        sp = jnp.maximum(z, 0.0) + jnp.log(1.0 + jnp.exp2(-jnp.abs(z))) * LOG2_E
        cs = jnp.dot(sp.astype(BF16), tri_ref[0:nk, 0:nk],
                     preferred_element_type=F32)
        total = cs[:, 0:1] + sp[:, 0:1]
        if carry is None:
            return jnp.exp2(z - (sp + cs)).astype(BF16), total
        return jnp.exp2(z - (sp + cs + carry)).astype(BF16), carry + total

    def accumulate(c, rs, ab, start, assign=False):
        _, bl, l0 = where[c]
        nk = ab.shape[1]
        for _, qn, seg in segs:
            if seg.start < rs.start or seg.stop > rs.stop:
                continue
            half = (seg.stop - seg.start) // 2
            for p in range(GROUP_HEADS // 2):
                vj = v_ref[bl, pl.ds(start, nk), l0 + p * pair:l0 + (p + 1) * pair]
                src = seg.start - rs.start + p * half
                av = jnp.dot(ab[src:src + half, :], vj, preferred_element_type=F32)
                dst = slice(seg.start + p * half, seg.start + (p + 1) * half)
                if assign:
                    acc_ref[c, dst, :] = av
                else:
                    acc_ref[c, dst, :] += av

    d0 = [pl.multiple_of(i * tk, tk) for i in tile_i]
    p0 = [pl.multiple_of(jnp.maximum(i - 1, 0) * tk, tk) for i in tile_i]
    no_prev = [jnp.where(i == 0, -MASKED_SCORE, 0.0).astype(F32) for i in tile_i]
    tile_of = [where[c][0] for c in range(chains)]
    z_da = [scores(c, ra, d0[tile_of[c]], tk // 2) + bias_a_ref[...] for c in range(chains)]
    z_dbc = [scores(c, rbc, d0[tile_of[c]], tk) + bias_bc_ref[...] for c in range(chains)]
    z_pab = [scores(c, rab, p0[tile_of[c]], tk) for c in range(chains)]
    low_c = []
    nb_rows = rb.stop - rb.start
    for c in range(chains):
        t = tile_of[c]
        ab, cr_a = weights(z_da[c], None)
        accumulate(c, ra, ab, d0[t], assign=True)
        ab, cr_bc = weights(z_dbc[c], None)
        accumulate(c, rbc, ab, d0[t], assign=True)
        cr_bc = cr_bc + no_prev[t]
        carry_ref[c, rc, :] = cr_bc[nb_rows:]
        low_c.append(jnp.min(cr_bc[nb_rows:]))
        carry_ab = jnp.concatenate([cr_a + no_prev[t], cr_bc[0:nb_rows]], axis=0)
        ab, cr = weights(z_pab[c], carry_ab)
        accumulate(c, rab, ab, p0[t])
        carry_ref[c, rab, :] = cr

    @pl.when(functools.reduce(jnp.minimum, low_c) < UNDERFLOW_BITS)
    def _():
        for c in range(chains):
            start = p0[tile_of[c]]
            ab, cr = weights(scores(c, rc, start, tk), carry_ref[c, rc, :])
            accumulate(c, rc, ab, start)
            carry_ref[c, rc, :] = cr

    for t in range(tiles):
        i = tile_i[t]

        def lowest_carry():
            return functools.reduce(jnp.minimum,
                                    [jnp.min(carry_ref[c]) for c in tile_chains[t]])

        def body(state):
            n, _ = state
            start = pl.multiple_of((i - 1 - n) * tk, tk)
            for c in tile_chains[t]:
                ab, cr = weights(scores(c, full, start, tk), carry_ref[c])
                accumulate(c, full, ab, start)
                carry_ref[c] = cr
            return n + 1, lowest_carry()

        def more(state):
            n, low = state
            return (n < i) & (low < UNDERFLOW_BITS)

        lax.while_loop(more, body, (jnp.int32(1), lowest_carry()))

    for t in range(tiles):
        for bl in range(nb):
            for q0, qn, rs in segs:
                lane_p = lax.broadcasted_iota(jnp.int32, (qn, pair), 1)
                def head_rows(c, hh):
                    r0 = rs.start + hh * qn
                    return acc_ref[c, r0:r0 + qn, :]
                group_chains = [c for c in tile_chains[t] if where[c][1] == bl]
                out = jnp.concatenate(
                    [jnp.where(lane_p < HEAD_DIM, head_rows(c, 2 * p), head_rows(c, 2 * p + 1))
                     for c in group_chains for p in range(GROUP_HEADS // 2)], axis=1)
                ms = _group_mean(out * out, gmat_ref[...])
                o_ref[bl, t * tq + q0:t * tq + q0 + qn, :] = (
                    out * lax.rsqrt(ms + EPS) * g_ref[...]).astype(BF16)

    @pl.when(step < last_step)
    def _():
        for copy in kv_copies(step + 1):
            copy.wait()


def _attention(q, k, v, tri, bias_a, bias_bc, g_attn_out, gmat):
    b, s, w = q.shape
    nb = ATTN_BATCHES
    gw = GROUP_HEADS * HEAD_DIM
    tq = ATTN_TILES * TQ
    chains = ATTN_TILES * nb * (w // gw)
    rows = GROUP_HEADS * TQ
    const = lambda bi, i: (0, 0)
    return pl.pallas_call(
        _attn_kernel,
        grid=(b // nb, s // tq),
        in_specs=[
            pl.BlockSpec((nb, tq, w), lambda bi, i: (bi, i, 0)),
            pl.BlockSpec(memory_space=pl.ANY),
            pl.BlockSpec(memory_space=pl.ANY),
            pl.BlockSpec(tri.shape, const),
            pl.BlockSpec(bias_a.shape, const),
            pl.BlockSpec(bias_bc.shape, const),
            pl.BlockSpec((1, w), const),
            pl.BlockSpec(gmat.shape, const),
        ],
        out_specs=pl.BlockSpec((nb, tq, w), lambda bi, i: (bi, i, 0)),
        out_shape=jax.ShapeDtypeStruct((b, s, w), BF16),
        scratch_shapes=[pltpu.VMEM((chains, rows, gw), BF16),
                        pltpu.VMEM((chains, rows, 2 * HEAD_DIM), F32),
                        pltpu.VMEM((chains, rows, 1), F32),
                        pltpu.VMEM((nb, s, w), BF16),
                        pltpu.VMEM((nb, s, w), BF16),
                        pltpu.SemaphoreType.DMA((2,))],
        compiler_params=pltpu.CompilerParams(
            dimension_semantics=("arbitrary", "arbitrary"),
            vmem_limit_bytes=ATTN_VMEM_LIMIT_BYTES),
        name="sb_attention",
    )(q, k, v, tri, bias_a, bias_bc, g_attn_out, gmat)


def _post_kernel(x_hbm, conv_hbm, attn_hbm, p_hbm, wo_hbm, gmlp_ref, wup_hbm, wdn_hbm, gple_ref,
                 wg_hbm, wp_hbm, gfin_ref, o_hbm, x_buf, conv_buf, attn_buf, p_buf, o_buf,
                 wo_ref, wup_ref, wdn_ref, wg_ref, wp_ref, in_sem, out_sem, w_sem, *,
                 apply_final):
    tm = x_buf.shape[1]
    n_tiles = x_hbm.shape[0] // tm
    w_conv = conv_buf.shape[2]
    dot = functools.partial(jnp.dot, preferred_element_type=F32)

    def tile_rows(i):
        return pl.ds(pl.multiple_of(i * tm, tm), tm)

    def in_copies(i, slot):
        pairs = ((x_hbm, x_buf), (conv_hbm, conv_buf), (attn_hbm, attn_buf), (p_hbm, p_buf))
        return [pltpu.make_async_copy(src.at[tile_rows(i), :], dst.at[slot], in_sem.at[j, slot])
                for j, (src, dst) in enumerate(pairs)]

    def out_copy(i, slot):
        return pltpu.make_async_copy(o_buf.at[slot], o_hbm.at[tile_rows(i), :], out_sem.at[slot])

    weights = ((wo_hbm, wo_ref), (wup_hbm, wup_ref), (wdn_hbm, wdn_ref), (wp_hbm, wp_ref),
               (wg_hbm, wg_ref))
    w_copies = {id(dst): pltpu.make_async_copy(src, dst, w_sem.at[j])
                for j, (src, dst) in enumerate(weights)}

    def compute(slot, arrive):
        subs = [slice(r0, r0 + POST_SUB_ROWS) for r0 in range(0, tm, POST_SUB_ROWS)]
        arrive(wo_ref)
        h = [x_buf[slot, rs, :] + dot(conv_buf[slot, rs, :], wo_ref[0:w_conv, :]) for rs in subs]
        h = [hs + dot(attn_buf[slot, rs, :], wo_ref[w_conv:, :]) for hs, rs in zip(h, subs)]
        m = [(hs * _rms_scale(hs) * gmlp_ref[...]).astype(BF16) for hs in h]
        arrive(wup_ref)
        r = [jnp.square(jnp.maximum(dot(ms, wup_ref[...]), 0.0)).astype(BF16) for ms in m]
        arrive(wdn_ref)
        h = [hs + dot(rr, wdn_ref[...]) for hs, rr in zip(h, r)]
        n = [(hs * _rms_scale(hs) * gple_ref[...]).astype(BF16) for hs in h]
        arrive(wp_ref)
        pp = [dot(p_buf[slot, rs, :].astype(BF16), wp_ref[...]) for rs in subs]
        arrive(wg_ref)
        gate = [jax.nn.sigmoid(dot(ns, wg_ref[...])) for ns in n]
        h = [hs + gs * ps for hs, gs, ps in zip(h, gate, pp)]
        for hs, rs in zip(h, subs):
            if apply_final:
                hs = hs * _rms_scale(hs) * gfin_ref[...]
            o_buf[slot, rs, :] = hs

    for copy in in_copies(0, 0):
        copy.start()
    for copy in w_copies.values():
        copy.start()
    for copy in in_copies(1, 1):
        copy.start()
    for copy in in_copies(0, 0):
        copy.wait()
    compute(0, lambda ref: w_copies[id(ref)].wait())
    out_copy(0, 0).start()

    def step(i, carry):
        slot = i % 2

        @pl.when(i + 1 < n_tiles)
        def _():
            for copy in in_copies(i + 1, 1 - slot):
                copy.start()

        for copy in in_copies(i, slot):
            copy.wait()

        @pl.when(i >= 2)
        def _():
            out_copy(i - 2, slot).wait()

        compute(slot, lambda ref: None)
        out_copy(i, slot).start()
        return carry

    lax.fori_loop(1, n_tiles, step, 0)
    out_copy(n_tiles - 2, n_tiles % 2).wait()
    out_copy(n_tiles - 1, (n_tiles - 1) % 2).wait()


def _post(x2, conv2, attn2, p2, w_out, g_mlp, w_up, w_down, g_ple, w_gate, w_proj, g_final,
          apply_final):
    t, d = x2.shape
    tm = TM_POST
    assert t // tm >= 2
    hbm = pl.BlockSpec(memory_space=pl.ANY)
    vmem = pl.BlockSpec(memory_space=pltpu.VMEM)
    tiles = lambda a, dtype: pltpu.VMEM((2, tm, a.shape[1]), dtype)
    return pl.pallas_call(
        functools.partial(_post_kernel, apply_final=apply_final),
        in_specs=[hbm, hbm, hbm, hbm, hbm, vmem, hbm, hbm, vmem, hbm, hbm, vmem],
        out_specs=hbm,
        out_shape=jax.ShapeDtypeStruct((t, d), F32),
        scratch_shapes=[tiles(x2, F32), tiles(conv2, BF16), tiles(attn2, BF16), tiles(p2, F32),
                        tiles(x2, F32)]
        + [pltpu.VMEM(w.shape, BF16) for w in (w_out, w_up, w_down, w_gate, w_proj)]
        + [pltpu.SemaphoreType.DMA((4, 2)),
           pltpu.SemaphoreType.DMA((2,)),
           pltpu.SemaphoreType.DMA((5,))],
        compiler_params=pltpu.CompilerParams(vmem_limit_bytes=ATTN_VMEM_LIMIT_BYTES),
        name="post",
    )(x2, conv2, attn2, p2, w_out, g_mlp, w_up, w_down, g_ple, w_gate, w_proj, g_final)


def _group_matrix():
    g = np.arange(MXU_TILE) // HEAD_DIM
    return jnp.asarray(np.where(g[:, None] == g[None, :], 1.0 / HEAD_DIM, 0.0), BF16)


def kernel(x, p, g_mix, w_in, conv_w, g_conv_out, g_attn_out, w_out, g_mlp, w_up, w_down,
           g_ple, w_ple_gate, w_ple_proj, g_final):
    b, s, d = x.shape
    depth = p.shape[0]
    w_conv = conv_w.shape[-1]
    w_attn = g_attn_out.shape[-1]
    assert s % TM_IN == 0 and s % TQ == 0 and TQ == TK and (b * s) % TM_POST == 0
    assert w_attn % (GROUP_HEADS * HEAD_DIM) == 0 and w_conv % MXU_TILE == 0
    assert w_attn % MXU_TILE == 0 and MXU_TILE % HEAD_DIM == 0
    assert b % ATTN_BATCHES == 0 and s % (ATTN_TILES * TQ) == 0

    idx = np.arange(TK)
    tri = jnp.asarray(idx[:, None] > idx[None, :], BF16)
    causal = np.where(idx[None, :] < idx[:, None], 0.0, MASKED_SCORE).astype(np.float32)
    per_head = lambda m: np.tile(m, (GROUP_HEADS, 1))
    bias_a = jnp.asarray(per_head(causal[:TQ // 2, :TK // 2]))
    bias_bc = jnp.asarray(np.concatenate([per_head(causal[TQ // 2:3 * TQ // 4]),
                                          per_head(causal[3 * TQ // 4:])], axis=0))
    gmat = _group_matrix()
    row = lambda g: g.reshape(1, -1)

    h = x
    for i in range(depth):
        (conv, q, k, v), (wo, wu, wd, wg, wp) = _mixer_in(
            h, row(g_mix[i]), w_in[i], row(conv_w[i]), row(g_conv_out[i]), gmat, w_conv, w_attn,
            [w_out[i], w_up[i], w_down[i], w_ple_gate[i], w_ple_proj[i]])
        attn = _attention(q, k, v, tri, bias_a, bias_bc, row(g_attn_out[i]), gmat)
        h = _post(h.reshape(b * s, d), conv.reshape(b * s, w_conv), attn.reshape(b * s, w_attn),
                  p[i].reshape(b * s, -1), wo, row(g_mlp[i]), wu, wd, row(g_ple[i]), wg, wp,
                  row(g_final), apply_final=(i == depth - 1)).reshape(b, s, d)
    return h
```

```python
import functools

import jax
import jax.numpy as jnp
import numpy as np
from jax import lax
from jax.experimental import pallas as pl
from jax.experimental.pallas import tpu as pltpu

HEAD_DIM = 64
CONV_K = 3
EPS = 1e-6
LOG2_E = 1.4426950408889634
UNDERFLOW_BITS = 150.0
MASKED_SCORE = -1e30

F32 = jnp.float32
BF16 = jnp.bfloat16

V7X_VMEM_BYTES = 64 * 1024 * 1024
VMEM_LIMIT_BYTES = V7X_VMEM_BYTES - 8 * 1024 * 1024
ATTN_VMEM_LIMIT_BYTES = V7X_VMEM_BYTES - 2 * 1024 * 1024
SUBLANES = 8
BF16_SUBLANES = 16
MXU_TILE = 256

TM_IN = 1024
IN_SUB_ROWS = 256
TQ = 256
TK = 256
GROUP_HEADS = 4
ATTN_BATCHES = 2
ATTN_TILES = 2
TM_POST = 512
POST_SUB_ROWS = 256
POST_UP_CHUNKS = 4


def _rms_scale(x):
    return lax.rsqrt(jnp.mean(x * x, axis=-1, keepdims=True) + EPS)


def _group_mean(sq, gmat):
    sqb = sq.astype(BF16)
    return jnp.concatenate(
        [jnp.dot(sqb[:, l0:l0 + MXU_TILE], gmat, preferred_element_type=F32)
         for l0 in range(0, sq.shape[1], MXU_TILE)], axis=1)


def _mixer_in_kernel(x_ref, g_ref, w32_ref, cw_ref, gco_ref, gmat_ref, *rest, w_conv, w_attn,
                     n_side):
    side_in, rest = rest[:n_side], rest[n_side:]
    conv_ref, q_ref, k_ref, v_ref = rest[:4]
    side_out = rest[4:4 + n_side]
    vbuf, w_ref = rest[4 + n_side:]
    tm = x_ref.shape[1]

    @pl.when((pl.program_id(0) == 0) & (pl.program_id(1) == 0))
    def _():
        for r0 in range(0, w32_ref.shape[0], IN_SUB_ROWS):
            w_ref[r0:r0 + IN_SUB_ROWS, :] = w32_ref[r0:r0 + IN_SUB_ROWS, :].astype(BF16)

    for src, dst in zip(side_in, side_out):
        dst[...] = src[...].astype(BF16)

    @pl.when(pl.program_id(1) == 0)
    def _():
        vbuf[0:SUBLANES, :] = jnp.zeros((SUBLANES, w_conv), F32)

    starts = list(range(0, tm, IN_SUB_ROWS))
    a = []
    for r0 in starts:
        x = x_ref[0, r0:r0 + IN_SUB_ROWS, :]
        a.append((x * _rms_scale(x) * g_ref[...]).astype(BF16))
    proj = [jnp.dot(as_, w_ref[...], preferred_element_type=F32) for as_ in a]

    o = 3 * w_conv
    for r0, pr in zip(starts, proj):
        rs = slice(r0, r0 + IN_SUB_ROWS)
        vbuf[SUBLANES + r0:SUBLANES + r0 + IN_SUB_ROWS, :] = (
            pr[:, w_conv:2 * w_conv] * pr[:, 2 * w_conv:3 * w_conv])
        q_ref[0, rs, :] = (pr[:, o:o + w_attn] * (HEAD_DIM ** -0.5 * LOG2_E)).astype(BF16)
        k_ref[0, rs, :] = pr[:, o + w_attn:o + 2 * w_attn].astype(BF16)
        v_ref[0, rs, :] = pr[:, o + 2 * w_attn:o + 3 * w_attn].astype(BF16)

    cw = [cw_ref[:, j * w_conv:(j + 1) * w_conv] for j in range(CONV_K)]
    for r0, pr in zip(starts, proj):
        v0 = vbuf[SUBLANES + r0:SUBLANES + r0 + IN_SUB_ROWS, :]
        v1 = vbuf[SUBLANES - 1 + r0:SUBLANES - 1 + r0 + IN_SUB_ROWS, :]
        v2 = vbuf[SUBLANES - 2 + r0:SUBLANES - 2 + r0 + IN_SUB_ROWS, :]
        c = pr[:, 0:w_conv] * (cw[0] * v2 + cw[1] * v1 + cw[2] * v0)
        ms = _group_mean(c * c, gmat_ref[...])
        conv_ref[0, r0:r0 + IN_SUB_ROWS, :] = (
            c * lax.rsqrt(ms + EPS) * gco_ref[...]).astype(BF16)
    vbuf[0:SUBLANES, :] = vbuf[tm:tm + SUBLANES, :]


def _mixer_in(x, g_mix, w_in, conv_w, g_conv_out, gmat, w_conv, w_attn, side_weights):
    b, s, d = x.shape
    tm = TM_IN
    n_i = s // tm
    steps = b * n_i
    const = lambda *_: (0, 0)
    tile = lambda bi, i: (bi, i, 0)
    out_sd = lambda w: jax.ShapeDtypeStruct((b, s, w), BF16)
    side_rows = [w.shape[0] // steps for w in side_weights]
    assert all(w.shape[0] % steps == 0 and r % BF16_SUBLANES == 0
               for w, r in zip(side_weights, side_rows))
    side_specs = [pl.BlockSpec((r, w.shape[1]), lambda bi, i: (bi * n_i + i, 0))
                  for w, r in zip(side_weights, side_rows)]
    outs = pl.pallas_call(
        functools.partial(_mixer_in_kernel, w_conv=w_conv, w_attn=w_attn,
                          n_side=len(side_weights)),
        grid=(b, n_i),
        in_specs=[
            pl.BlockSpec((1, tm, d), tile),
            pl.BlockSpec((1, d), const),
            pl.BlockSpec(w_in.shape, const, pipeline_mode=pl.Buffered(1)),
            pl.BlockSpec((1, CONV_K * w_conv), const),
            pl.BlockSpec((1, w_conv), const),
            pl.BlockSpec(gmat.shape, const),
        ] + side_specs,
        out_specs=[
            pl.BlockSpec((1, tm, w_conv), tile),
            pl.BlockSpec((1, tm, w_attn), tile),
            pl.BlockSpec((1, tm, w_attn), tile),
            pl.BlockSpec((1, tm, w_attn), tile),
        ] + side_specs,
        out_shape=[out_sd(w_conv), out_sd(w_attn), out_sd(w_attn), out_sd(w_attn)]
        + [jax.ShapeDtypeStruct(w.shape, BF16) for w in side_weights],
        scratch_shapes=[pltpu.VMEM((tm + SUBLANES, w_conv), F32),
                        pltpu.VMEM(w_in.shape, BF16)],
        compiler_params=pltpu.CompilerParams(
            dimension_semantics=("arbitrary", "arbitrary"),
            vmem_limit_bytes=VMEM_LIMIT_BYTES),
        name="mixer_in",
    )(x, g_mix, w_in, conv_w, g_conv_out, gmat, *side_weights)
    return outs[:4], outs[4:]


def _attn_kernel(q_ref, k_hbm, v_hbm, tri_ref, bias_a_ref, bias_bc_ref, g_ref, gmat_ref,
                 o_ref, qst_ref, acc_ref, carry_ref, k_ref, v_ref, kv_sem):
    nb, tq = q_ref.shape[0], TQ
    tiles = q_ref.shape[1] // tq
    tk = tri_ref.shape[0]
    chains, rows, gw = qst_ref.shape
    groups = chains // (tiles * nb)
    pair = 2 * HEAD_DIM
    where = [(c // (nb * groups), (c // groups) % nb, (c % groups) * gw) for c in range(chains)]
    tile_chains = [[c for c in range(chains) if where[c][0] == t] for t in range(tiles)]
    tile_i = [tiles * pl.program_id(1) + t for t in range(tiles)]
    segs = []
    for q0, qn in ((0, tq // 2), (tq // 2, tq // 4), (3 * tq // 4, tq // 4)):
        segs.append((q0, qn, slice(GROUP_HEADS * q0, GROUP_HEADS * (q0 + qn))))
    (_, _, ra), (_, _, rb), (_, _, rc) = segs
    rab, rbc, full = slice(0, rb.stop), slice(rb.start, rows), slice(0, rows)

    step, last_step = pl.program_id(1), pl.num_programs(1) - 1
    step_keys = tiles * tq
    batches = pl.ds(pl.program_id(0) * nb, nb)

    def kv_copies(j):
        keys = pl.ds(pl.multiple_of(j * step_keys, step_keys), step_keys)
        return [pltpu.make_async_copy(src.at[batches, keys, :], dst.at[:, keys, :], kv_sem.at[n])
                for n, (src, dst) in enumerate(((k_hbm, k_ref), (v_hbm, v_ref)))]

    @pl.when(step == 0)
    def _():
        for copy in kv_copies(0):
            copy.start()
        for copy in kv_copies(0):
            copy.wait()

    @pl.when(step < last_step)
    def _():
        for copy in kv_copies(step + 1):
            copy.start()

    for c, (t, bl, l0) in enumerate(where):
        for q0, qn, rs in segs:
            q = q_ref[bl, t * tq + q0:t * tq + q0 + qn, l0:l0 + gw]
            lane = lax.broadcasted_iota(jnp.int32, (qn, gw), 1)
            for hh in range(GROUP_HEADS):
                in_head = (lane >= hh * HEAD_DIM) & (lane < (hh + 1) * HEAD_DIM)
                r0 = rs.start + hh * qn
                qst_ref[c, r0:r0 + qn, :] = jnp.where(in_head, q, jnp.zeros_like(q))

    def scores(c, rs, start, nk):
        _, bl, l0 = where[c]
        kj = k_ref[bl, pl.ds(start, nk), l0:l0 + gw]
        return lax.dot_general(qst_ref[c, rs, :], kj, (((1,), (1,)), ((), ())),
                               preferred_element_type=F32)

    def weights(z, carry):
        nk = z.shape[1]
        sp = jnp.maximum(z, 0.0) + jnp.log(1.0 + jnp.exp2(-jnp.abs(z))) * LOG2_E
        cs = jnp.dot(sp.astype(BF16), tri_ref[0:nk, 0:nk],
                     preferred_element_type=F32)
        total = cs[:, 0:1] + sp[:, 0:1]
        if carry is None:
            return jnp.exp2(z - (sp + cs)).astype(BF16), total
        return jnp.exp2(z - (sp + cs + carry)).astype(BF16), carry + total

    def accumulate(c, rs, ab, start, assign=False):
        _, bl, l0 = where[c]
        nk = ab.shape[1]
        for _, qn, seg in segs:
            if seg.start < rs.start or seg.stop > rs.stop:
                continue
            half = (seg.stop - seg.start) // 2
            for p in range(GROUP_HEADS // 2):
                vj = v_ref[bl, pl.ds(start, nk), l0 + p * pair:l0 + (p + 1) * pair]
                src = seg.start - rs.start + p * half
                av = jnp.dot(ab[src:src + half, :], vj, preferred_element_type=F32)
                dst = slice(seg.start + p * half, seg.start + (p + 1) * half)
                if assign:
                    acc_ref[c, dst, :] = av
                else:
                    acc_ref[c, dst, :] += av

    d0 = [pl.multiple_of(i * tk, tk) for i in tile_i]
    p0 = [pl.multiple_of(jnp.maximum(i - 1, 0) * tk, tk) for i in tile_i]
    no_prev = [jnp.where(i == 0, -MASKED_SCORE, 0.0).astype(F32) for i in tile_i]
    tile_of = [where[c][0] for c in range(chains)]
    z_da = [scores(c, ra, d0[tile_of[c]], tk // 2) + bias_a_ref[...] for c in range(chains)]
    z_dbc = [scores(c, rbc, d0[tile_of[c]], tk) + bias_bc_ref[...] for c in range(chains)]
    z_pab = [scores(c, rab, p0[tile_of[c]], tk) for c in range(chains)]
    low_c = []
    nb_rows = rb.stop - rb.start
    for c in range(chains):
        t = tile_of[c]
        ab, cr_a = weights(z_da[c], None)
        accumulate(c, ra, ab, d0[t], assign=True)
        ab, cr_bc = weights(z_dbc[c], None)
        accumulate(c, rbc, ab, d0[t], assign=True)
        cr_bc = cr_bc + no_prev[t]
        carry_ref[c, rc, :] = cr_bc[nb_rows:]
        low_c.append(jnp.min(cr_bc[nb_rows:]))
        carry_ab = jnp.concatenate([cr_a + no_prev[t], cr_bc[0:nb_rows]], axis=0)
        ab, cr = weights(z_pab[c], carry_ab)
        accumulate(c, rab, ab, p0[t])
        carry_ref[c, rab, :] = cr

    @pl.when(functools.reduce(jnp.minimum, low_c) < UNDERFLOW_BITS)
    def _():
        for c in range(chains):
            start = p0[tile_of[c]]
            ab, cr = weights(scores(c, rc, start, tk), carry_ref[c, rc, :])
            accumulate(c, rc, ab, start)
            carry_ref[c, rc, :] = cr

    for t in range(tiles):
        i = tile_i[t]

        def lowest_carry():
            return functools.reduce(jnp.minimum,
                                    [jnp.min(carry_ref[c]) for c in tile_chains[t]])

        def body(state):
            n, _ = state
            start = pl.multiple_of((i - 1 - n) * tk, tk)
            for c in tile_chains[t]:
                ab, cr = weights(scores(c, full, start, tk), carry_ref[c])
                accumulate(c, full, ab, start)
                carry_ref[c] = cr
            return n + 1, lowest_carry()

        def more(state):
            n, low = state
            return (n < i) & (low < UNDERFLOW_BITS)

        lax.while_loop(more, body, (jnp.int32(1), lowest_carry()))

    for t in range(tiles):
        for bl in range(nb):
            for q0, qn, rs in segs:
                lane_p = lax.broadcasted_iota(jnp.int32, (qn, pair), 1)
                def head_rows(c, hh):
                    r0 = rs.start + hh * qn
                    return acc_ref[c, r0:r0 + qn, :]
                group_chains = [c for c in tile_chains[t] if where[c][1] == bl]
                out = jnp.concatenate(
                    [jnp.where(lane_p < HEAD_DIM, head_rows(c, 2 * p), head_rows(c, 2 * p + 1))
                     for c in group_chains for p in range(GROUP_HEADS // 2)], axis=1)
                ms = _group_mean(out * out, gmat_ref[...])
                o_ref[bl, t * tq + q0:t * tq + q0 + qn, :] = (
                    out * lax.rsqrt(ms + EPS) * g_ref[...]).astype(BF16)

    @pl.when(step < last_step)
    def _():
        for copy in kv_copies(step + 1):
            copy.wait()


def _attention(q, k, v, tri, bias_a, bias_bc, g_attn_out, gmat):
    b, s, w = q.shape
    nb = ATTN_BATCHES
    gw = GROUP_HEADS * HEAD_DIM
    tq = ATTN_TILES * TQ
    chains = ATTN_TILES * nb * (w // gw)
    rows = GROUP_HEADS * TQ
    const = lambda bi, i: (0, 0)
    return pl.pallas_call(
        _attn_kernel,
        grid=(b // nb, s // tq),
        in_specs=[
            pl.BlockSpec((nb, tq, w), lambda bi, i: (bi, i, 0)),
            pl.BlockSpec(memory_space=pl.ANY),
            pl.BlockSpec(memory_space=pl.ANY),
            pl.BlockSpec(tri.shape, const),
            pl.BlockSpec(bias_a.shape, const),
            pl.BlockSpec(bias_bc.shape, const),
            pl.BlockSpec((1, w), const),
            pl.BlockSpec(gmat.shape, const),
        ],
        out_specs=pl.BlockSpec((nb, tq, w), lambda bi, i: (bi, i, 0)),
        out_shape=jax.ShapeDtypeStruct((b, s, w), BF16),
        scratch_shapes=[pltpu.VMEM((chains, rows, gw), BF16),
                        pltpu.VMEM((chains, rows, 2 * HEAD_DIM), F32),
                        pltpu.VMEM((chains, rows, 1), F32),
                        pltpu.VMEM((nb, s, w), BF16),
                        pltpu.VMEM((nb, s, w), BF16),
                        pltpu.SemaphoreType.DMA((2,))],
        compiler_params=pltpu.CompilerParams(
            dimension_semantics=("arbitrary", "arbitrary"),
            vmem_limit_bytes=ATTN_VMEM_LIMIT_BYTES),
        name="sb_attention",
    )(q, k, v, tri, bias_a, bias_bc, g_attn_out, gmat)


def _post_kernel(x_hbm, conv_hbm, attn_hbm, p_hbm, wo_hbm, gmlp_ref, wup_hbm, wdn_hbm, gple_ref,
                 wg_hbm, wp_hbm, gfin_ref, o_hbm, x_buf, conv_buf, attn_buf, p_buf, o_buf,
                 wo_ref, wup_ref, wdn_ref, wg_ref, wp_ref, in_sem, out_sem, w_sem, *,
                 apply_final):
    tm = x_buf.shape[1]
    n_tiles = x_hbm.shape[0] // tm
    w_conv = conv_buf.shape[2]
    dot = functools.partial(jnp.dot, preferred_element_type=F32)

    def tile_rows(i):
        return pl.ds(pl.multiple_of(i * tm, tm), tm)

    def in_copies(i, slot):
        pairs = ((x_hbm, x_buf), (conv_hbm, conv_buf), (attn_hbm, attn_buf), (p_hbm, p_buf))
        return [pltpu.make_async_copy(src.at[tile_rows(i), :], dst.at[slot], in_sem.at[j, slot])
                for j, (src, dst) in enumerate(pairs)]

    def out_copy(i, slot):
        return pltpu.make_async_copy(o_buf.at[slot], o_hbm.at[tile_rows(i), :], out_sem.at[slot])

    up_cols = [slice(c0, c0 + wup_ref.shape[1] // POST_UP_CHUNKS)
               for c0 in range(0, wup_ref.shape[1], wup_ref.shape[1] // POST_UP_CHUNKS)]
    weights = ([("out", wo_hbm, wo_ref)]
               + [(("up", j), wup_hbm.at[:, cs], wup_ref.at[:, cs]) for j, cs in enumerate(up_cols)]
               + [("down", wdn_hbm, wdn_ref), ("proj", wp_hbm, wp_ref), ("gate", wg_hbm, wg_ref)])
    w_copies = {name: pltpu.make_async_copy(src, dst, w_sem.at[j])
                for j, (name, src, dst) in enumerate(weights)}

    def compute(slot, arrive, up_chunks):
        subs = [slice(r0, r0 + POST_SUB_ROWS) for r0 in range(0, tm, POST_SUB_ROWS)]
        arrive("out")
        h = [x_buf[slot, rs, :] + dot(conv_buf[slot, rs, :], wo_ref[0:w_conv, :]) for rs in subs]
        h = [hs + dot(attn_buf[slot, rs, :], wo_ref[w_conv:, :]) for hs, rs in zip(h, subs)]
        m = [(hs * _rms_scale(hs) * gmlp_ref[...]).astype(BF16) for hs in h]
        parts = []
        for cs in up_chunks:
            for j, chunk in enumerate(up_cols):
                if cs.start <= chunk.start < cs.stop:
                    arrive(("up", j))
            parts.append([jnp.square(jnp.maximum(dot(ms, wup_ref[:, cs]), 0.0)).astype(BF16)
                          for ms in m])
        r = [jnp.concatenate(list(ps), axis=1) if len(ps) > 1 else ps[0] for ps in zip(*parts)]
        arrive("down")
        h = [hs + dot(rr, wdn_ref[...]) for hs, rr in zip(h, r)]
        n = [(hs * _rms_scale(hs) * gple_ref[...]).astype(BF16) for hs in h]
        arrive("proj")
        pp = [dot(p_buf[slot, rs, :].astype(BF16), wp_ref[...]) for rs in subs]
        arrive("gate")
        gate = [jax.nn.sigmoid(dot(ns, wg_ref[...])) for ns in n]
        h = [hs + gs * ps for hs, gs, ps in zip(h, gate, pp)]
        for hs, rs in zip(h, subs):
            if apply_final:
                hs = hs * _rms_scale(hs) * gfin_ref[...]
            o_buf[slot, rs, :] = hs

    for copy in in_copies(0, 0):
        copy.start()
    for copy in w_copies.values():
        copy.start()
    for copy in in_copies(1, 1):
        copy.start()
    for copy in in_copies(0, 0):
        copy.wait()
    compute(0, lambda name: w_copies[name].wait(), up_cols)
    out_copy(0, 0).start()

    def step(i, carry):
        slot = i % 2

        @pl.when(i + 1 < n_tiles)
        def _():
            for copy in in_copies(i + 1, 1 - slot):
                copy.start()

        for copy in in_copies(i, slot):
            copy.wait()

        @pl.when(i >= 2)
        def _():
            out_copy(i - 2, slot).wait()

        compute(slot, lambda name: None, [slice(0, wup_ref.shape[1])])
        out_copy(i, slot).start()
        return carry

    lax.fori_loop(1, n_tiles, step, 0)
    out_copy(n_tiles - 2, n_tiles % 2).wait()
    out_copy(n_tiles - 1, (n_tiles - 1) % 2).wait()


def _post(x2, conv2, attn2, p2, w_out, g_mlp, w_up, w_down, g_ple, w_gate, w_proj, g_final,
          apply_final):
    t, d = x2.shape
    tm = TM_POST
    assert t // tm >= 2
    hbm = pl.BlockSpec(memory_space=pl.ANY)
    vmem = pl.BlockSpec(memory_space=pltpu.VMEM)
    tiles = lambda a, dtype: pltpu.VMEM((2, tm, a.shape[1]), dtype)
    return pl.pallas_call(
        functools.partial(_post_kernel, apply_final=apply_final),
        in_specs=[hbm, hbm, hbm, hbm, hbm, vmem, hbm, hbm, vmem, hbm, hbm, vmem],
        out_specs=hbm,
        out_shape=jax.ShapeDtypeStruct((t, d), F32),
        scratch_shapes=[tiles(x2, F32), tiles(conv2, BF16), tiles(attn2, BF16), tiles(p2, F32),
                        tiles(x2, F32)]
        + [pltpu.VMEM(w.shape, BF16) for w in (w_out, w_up, w_down, w_gate, w_proj)]
        + [pltpu.SemaphoreType.DMA((4, 2)),
           pltpu.SemaphoreType.DMA((2,)),
           pltpu.SemaphoreType.DMA((4 + POST_UP_CHUNKS,))],
        compiler_params=pltpu.CompilerParams(vmem_limit_bytes=ATTN_VMEM_LIMIT_BYTES),
        name="post",
    )(x2, conv2, attn2, p2, w_out, g_mlp, w_up, w_down, g_ple, w_gate, w_proj, g_final)


def _group_matrix():
    g = np.arange(MXU_TILE) // HEAD_DIM
    return jnp.asarray(np.where(g[:, None] == g[None, :], 1.0 / HEAD_DIM, 0.0), BF16)


def kernel(x, p, g_mix, w_in, conv_w, g_conv_out, g_attn_out, w_out, g_mlp, w_up, w_down,
           g_ple, w_ple_gate, w_ple_proj, g_final):
    b, s, d = x.shape
    depth = p.shape[0]
    w_conv = conv_w.shape[-1]
    w_attn = g_attn_out.shape[-1]
    assert s % TM_IN == 0 and s % TQ == 0 and TQ == TK and (b * s) % TM_POST == 0
    assert w_attn % (GROUP_HEADS * HEAD_DIM) == 0 and w_conv % MXU_TILE == 0
    assert w_attn % MXU_TILE == 0 and MXU_TILE % HEAD_DIM == 0
    assert b % ATTN_BATCHES == 0 and s % (ATTN_TILES * TQ) == 0

    idx = np.arange(TK)
    tri = jnp.asarray(idx[:, None] > idx[None, :], BF16)
    causal = np.where(idx[None, :] < idx[:, None], 0.0, MASKED_SCORE).astype(np.float32)
    per_head = lambda m: np.tile(m, (GROUP_HEADS, 1))
    bias_a = jnp.asarray(per_head(causal[:TQ // 2, :TK // 2]))
    bias_bc = jnp.asarray(np.concatenate([per_head(causal[TQ // 2:3 * TQ // 4]),
                                          per_head(causal[3 * TQ // 4:])], axis=0))
    gmat = _group_matrix()
    row = lambda g: g.reshape(1, -1)

    h = x
    for i in range(depth):
        (conv, q, k, v), (wo, wu, wd, wg, wp) = _mixer_in(
            h, row(g_mix[i]), w_in[i], row(conv_w[i]), row(g_conv_out[i]), gmat, w_conv, w_attn,
            [w_out[i], w_up[i], w_down[i], w_ple_gate[i], w_ple_proj[i]])
        attn = _attention(q, k, v, tri, bias_a, bias_bc, row(g_attn_out[i]), gmat)
        h = _post(h.reshape(b * s, d), conv.reshape(b * s, w_conv), attn.reshape(b * s, w_attn),
                  p[i].reshape(b * s, -1), wo, row(g_mlp[i]), wu, wd, row(g_ple[i]), wg, wp,
                  row(g_final), apply_final=(i == depth - 1)).reshape(b, s, d)
    return h
```

```python
import functools

import jax
import jax.numpy as jnp
import numpy as np
from jax import lax
from jax.experimental import pallas as pl
from jax.experimental.pallas import tpu as pltpu

HEAD_DIM = 64
CONV_K = 3
EPS = 1e-6
LOG2_E = 1.4426950408889634
UNDERFLOW_BITS = 150.0
MASKED_SCORE = -1e30

F32 = jnp.float32
BF16 = jnp.bfloat16

V7X_VMEM_BYTES = 64 * 1024 * 1024
VMEM_LIMIT_BYTES = V7X_VMEM_BYTES - 8 * 1024 * 1024
ATTN_VMEM_LIMIT_BYTES = V7X_VMEM_BYTES - 2 * 1024 * 1024
SUBLANES = 8
BF16_SUBLANES = 16
MXU_TILE = 256

TM_IN = 1024
IN_SUB_ROWS = 256
TQ = 256
TK = 256
GROUP_HEADS = 4
ATTN_BATCHES = 2
ATTN_TILES = 2
TM_POST = 512
POST_SUB_ROWS = 256


def _rms_scale(x):
    return lax.rsqrt(jnp.mean(x * x, axis=-1, keepdims=True) + EPS)


def _group_mean(sq, gmat):
    sqb = sq.astype(BF16)
    return jnp.concatenate(
        [jnp.dot(sqb[:, l0:l0 + MXU_TILE], gmat, preferred_element_type=F32)
         for l0 in range(0, sq.shape[1], MXU_TILE)], axis=1)


def _mixer_in_kernel(x_ref, g_ref, w32_ref, cw_ref, gco_ref, gmat_ref, *rest, w_conv, w_attn,
                     n_side):
    side_in, rest = rest[:n_side], rest[n_side:]
    conv_ref, q_ref, k_ref, v_ref = rest[:4]
    side_out = rest[4:4 + n_side]
    vbuf, w_ref = rest[4 + n_side:]
    tm = x_ref.shape[1]

    @pl.when((pl.program_id(0) == 0) & (pl.program_id(1) == 0))
    def _():
        for r0 in range(0, w32_ref.shape[0], IN_SUB_ROWS):
            w_ref[r0:r0 + IN_SUB_ROWS, :] = w32_ref[r0:r0 + IN_SUB_ROWS, :].astype(BF16)

    for src, dst in zip(side_in, side_out):
        dst[...] = src[...].astype(BF16)

    @pl.when(pl.program_id(1) == 0)
    def _():
        vbuf[0:SUBLANES, :] = jnp.zeros((SUBLANES, w_conv), F32)

    starts = list(range(0, tm, IN_SUB_ROWS))
    a = []
    for r0 in starts:
        x = x_ref[0, r0:r0 + IN_SUB_ROWS, :]
        a.append((x * _rms_scale(x) * g_ref[...]).astype(BF16))
    proj = [jnp.dot(as_, w_ref[...], preferred_element_type=F32) for as_ in a]

    o = 3 * w_conv
    for r0, pr in zip(starts, proj):
        rs = slice(r0, r0 + IN_SUB_ROWS)
        vbuf[SUBLANES + r0:SUBLANES + r0 + IN_SUB_ROWS, :] = (
            pr[:, w_conv:2 * w_conv] * pr[:, 2 * w_conv:3 * w_conv])
        q_ref[0, rs, :] = (pr[:, o:o + w_attn] * (HEAD_DIM ** -0.5 * LOG2_E)).astype(BF16)
        k_ref[0, rs, :] = pr[:, o + w_attn:o + 2 * w_attn].astype(BF16)
        v_ref[0, rs, :] = pr[:, o + 2 * w_attn:o + 3 * w_attn].astype(BF16)

    cw = [cw_ref[:, j * w_conv:(j + 1) * w_conv] for j in range(CONV_K)]
    for r0, pr in zip(starts, proj):
        v0 = vbuf[SUBLANES + r0:SUBLANES + r0 + IN_SUB_ROWS, :]
        v1 = vbuf[SUBLANES - 1 + r0:SUBLANES - 1 + r0 + IN_SUB_ROWS, :]
        v2 = vbuf[SUBLANES - 2 + r0:SUBLANES - 2 + r0 + IN_SUB_ROWS, :]
        c = pr[:, 0:w_conv] * (cw[0] * v2 + cw[1] * v1 + cw[2] * v0)
        ms = _group_mean(c * c, gmat_ref[...])
        conv_ref[0, r0:r0 + IN_SUB_ROWS, :] = (
            c * lax.rsqrt(ms + EPS) * gco_ref[...]).astype(BF16)
    vbuf[0:SUBLANES, :] = vbuf[tm:tm + SUBLANES, :]


def _mixer_in(x, g_mix, w_in, conv_w, g_conv_out, gmat, w_conv, w_attn, side_weights):
    b, s, d = x.shape
    tm = TM_IN
    n_i = s // tm
    steps = b * n_i
    const = lambda *_: (0, 0)
    tile = lambda bi, i: (bi, i, 0)
    out_sd = lambda w: jax.ShapeDtypeStruct((b, s, w), BF16)
    side_rows = [w.shape[0] // steps for w in side_weights]
    assert all(w.shape[0] % steps == 0 and r % BF16_SUBLANES == 0
               for w, r in zip(side_weights, side_rows))
    side_specs = [pl.BlockSpec((r, w.shape[1]), lambda bi, i: (bi * n_i + i, 0))
                  for w, r in zip(side_weights, side_rows)]
    outs = pl.pallas_call(
        functools.partial(_mixer_in_kernel, w_conv=w_conv, w_attn=w_attn,
                          n_side=len(side_weights)),
        grid=(b, n_i),
        in_specs=[
            pl.BlockSpec((1, tm, d), tile),
            pl.BlockSpec((1, d), const),
            pl.BlockSpec(w_in.shape, const, pipeline_mode=pl.Buffered(1)),
            pl.BlockSpec((1, CONV_K * w_conv), const),
            pl.BlockSpec((1, w_conv), const),
            pl.BlockSpec(gmat.shape, const),
        ] + side_specs,
        out_specs=[
            pl.BlockSpec((1, tm, w_conv), tile),
            pl.BlockSpec((1, tm, w_attn), tile),
            pl.BlockSpec((1, tm, w_attn), tile),
            pl.BlockSpec((1, tm, w_attn), tile),
        ] + side_specs,
        out_shape=[out_sd(w_conv), out_sd(w_attn), out_sd(w_attn), out_sd(w_attn)]
        + [jax.ShapeDtypeStruct(w.shape, BF16) for w in side_weights],
        scratch_shapes=[pltpu.VMEM((tm + SUBLANES, w_conv), F32),
                        pltpu.VMEM(w_in.shape, BF16)],
        compiler_params=pltpu.CompilerParams(
            dimension_semantics=("arbitrary", "arbitrary"),
            vmem_limit_bytes=VMEM_LIMIT_BYTES),
        name="mixer_in",
    )(x, g_mix, w_in, conv_w, g_conv_out, gmat, *side_weights)
    return outs[:4], outs[4:]


def _attn_kernel(q_ref, k_hbm, v_hbm, tri_ref, bias_a_ref, bias_bc_ref, g_ref, gmat_ref,
                 o_ref, qst_ref, acc_ref, carry_ref, k_ref, v_ref, kv_sem):
    nb, tq = q_ref.shape[0], TQ
    tiles = q_ref.shape[1] // tq
    tk = tri_ref.shape[0]
    chains, rows, gw = qst_ref.shape
    groups = chains // (tiles * nb)
    pair = 2 * HEAD_DIM
    where = [(c // (nb * groups), (c // groups) % nb, (c % groups) * gw) for c in range(chains)]
    tile_chains = [[c for c in range(chains) if where[c][0] == t] for t in range(tiles)]
    tile_i = [tiles * pl.program_id(1) + t for t in range(tiles)]
    segs = []
    for q0, qn in ((0, tq // 2), (tq // 2, tq // 4), (3 * tq // 4, tq // 4)):
        segs.append((q0, qn, slice(GROUP_HEADS * q0, GROUP_HEADS * (q0 + qn))))
    (_, _, ra), (_, _, rb), (_, _, rc) = segs
    rab, rbc, full = slice(0, rb.stop), slice(rb.start, rows), slice(0, rows)

    step, last_step = pl.program_id(1), pl.num_programs(1) - 1
    step_keys = tiles * tq
    batches = pl.ds(pl.program_id(0) * nb, nb)

    def kv_copies(j):
        keys = pl.ds(pl.multiple_of(j * step_keys, step_keys), step_keys)
        return [pltpu.make_async_copy(src.at[batches, keys, :], dst.at[:, keys, :], kv_sem.at[n])
                for n, (src, dst) in enumerate(((k_hbm, k_ref), (v_hbm, v_ref)))]

    @pl.when(step == 0)
    def _():
        for copy in kv_copies(0):
            copy.start()
        for copy in kv_copies(0):
            copy.wait()

    @pl.when(step < last_step)
    def _():
        for copy in kv_copies(step + 1):
            copy.start()

    for c, (t, bl, l0) in enumerate(where):
        for q0, qn, rs in segs:
            q = q_ref[bl, t * tq + q0:t * tq + q0 + qn, l0:l0 + gw]
            lane = lax.broadcasted_iota(jnp.int32, (qn, gw), 1)
            for hh in range(GROUP_HEADS):
                in_head = (lane >= hh * HEAD_DIM) & (lane < (hh + 1) * HEAD_DIM)
                r0 = rs.start + hh * qn
                qst_ref[c, r0:r0 + qn, :] = jnp.where(in_head, q, jnp.zeros_like(q))

    def scores(c, rs, start, nk):
        _, bl, l0 = where[c]
        kj = k_ref[bl, pl.ds(start, nk), l0:l0 + gw]
        return lax.dot_general(qst_ref[c, rs, :], kj, (((1,), (1,)), ((), ())),
                               preferred_element_type=F32)

    def weights(z, carry):
        nk = z.shape[1]
        sp = jnp.maximum(z, 0.0) + jnp.log(1.0 + jnp.exp2(-jnp.abs(z))) * LOG2_E
        cs = jnp.dot(sp.astype(BF16), tri_ref[0:nk, 0:nk],
                     preferred_element_type=F32)
        total = cs[:, 0:1] + sp[:, 0:1]
        if carry is None:
            return jnp.exp2(z - (sp + cs)).astype(BF16), total
        return jnp.exp2(z - (sp + cs + carry)).astype(BF16), carry + total

    def accumulate(c, rs, ab, start, assign=False):
        _, bl, l0 = where[c]
        nk = ab.shape[1]
        for _, qn, seg in segs:
            if seg.start < rs.start or seg.stop > rs.stop:
                continue
            half = (seg.stop - seg.start) // 2
            for p in range(GROUP_HEADS // 2):
                vj = v_ref[bl, pl.ds(start, nk), l0 + p * pair:l0 + (p + 1) * pair]
                src = seg.start - rs.start + p * half
                av = jnp.dot(ab[src:src + half, :], vj, preferred_element_type=F32)
                dst = slice(seg.start + p * half, seg.start + (p + 1) * half)
                if assign:
                    acc_ref[c, dst, :] = av
                else:
                    acc_ref[c, dst, :] += av

    d0 = [pl.multiple_of(i * tk, tk) for i in tile_i]
    p0 = [pl.multiple_of(jnp.maximum(i - 1, 0) * tk, tk) for i in tile_i]
    no_prev = [jnp.where(i == 0, -MASKED_SCORE, 0.0).astype(F32) for i in tile_i]
    tile_of = [where[c][0] for c in range(chains)]
    z_da = [scores(c, ra, d0[tile_of[c]], tk // 2) + bias_a_ref[...] for c in range(chains)]
    z_dbc = [scores(c, rbc, d0[tile_of[c]], tk) + bias_bc_ref[...] for c in range(chains)]
    z_pab = [scores(c, rab, p0[tile_of[c]], tk) for c in range(chains)]
    low_c = []
    nb_rows = rb.stop - rb.start
    for c in range(chains):
        t = tile_of[c]
        ab, cr_a = weights(z_da[c], None)
        accumulate(c, ra, ab, d0[t], assign=True)
        ab, cr_bc = weights(z_dbc[c], None)
        accumulate(c, rbc, ab, d0[t], assign=True)
        cr_bc = cr_bc + no_prev[t]
        carry_ref[c, rc, :] = cr_bc[nb_rows:]
        low_c.append(jnp.min(cr_bc[nb_rows:]))
        carry_ab = jnp.concatenate([cr_a + no_prev[t], cr_bc[0:nb_rows]], axis=0)
        ab, cr = weights(z_pab[c], carry_ab)
        accumulate(c, rab, ab, p0[t])
        carry_ref[c, rab, :] = cr

    @pl.when(functools.reduce(jnp.minimum, low_c) < UNDERFLOW_BITS)
    def _():
        for c in range(chains):
            start = p0[tile_of[c]]
            ab, cr = weights(scores(c, rc, start, tk), carry_ref[c, rc, :])
            accumulate(c, rc, ab, start)
            carry_ref[c, rc, :] = cr

    for t in range(tiles):
        i = tile_i[t]

        def lowest_carry():
            return functools.reduce(jnp.minimum,
                                    [jnp.min(carry_ref[c]) for c in tile_chains[t]])

        def body(state):
            n, _ = state
            start = pl.multiple_of((i - 1 - n) * tk, tk)
            for c in tile_chains[t]:
                ab, cr = weights(scores(c, full, start, tk), carry_ref[c])
                accumulate(c, full, ab, start)
                carry_ref[c] = cr
            return n + 1, lowest_carry()

        def more(state):
            n, low = state
            return (n < i) & (low < UNDERFLOW_BITS)

        lax.while_loop(more, body, (jnp.int32(1), lowest_carry()))

    for t in range(tiles):
        for bl in range(nb):
            for q0, qn, rs in segs:
                lane_p = lax.broadcasted_iota(jnp.int32, (qn, pair), 1)
                def head_rows(c, hh):
                    r0 = rs.start + hh * qn
                    return acc_ref[c, r0:r0 + qn, :]
                group_chains = [c for c in tile_chains[t] if where[c][1] == bl]
                out = jnp.concatenate(
                    [jnp.where(lane_p < HEAD_DIM, head_rows(c, 2 * p), head_rows(c, 2 * p + 1))
                     for c in group_chains for p in range(GROUP_HEADS // 2)], axis=1)
                ms = _group_mean(out * out, gmat_ref[...])
                o_ref[bl, t * tq + q0:t * tq + q0 + qn, :] = (
                    out * lax.rsqrt(ms + EPS) * g_ref[...]).astype(BF16)

    @pl.when(step < last_step)
    def _():
        for copy in kv_copies(step + 1):
            copy.wait()


def _attention(q, k, v, tri, bias_a, bias_bc, g_attn_out, gmat):
    b, s, w = q.shape
    nb = ATTN_BATCHES
    gw = GROUP_HEADS * HEAD_DIM
    tq = ATTN_TILES * TQ
    chains = ATTN_TILES * nb * (w // gw)
    rows = GROUP_HEADS * TQ
    const = lambda bi, i: (0, 0)
    return pl.pallas_call(
        _attn_kernel,
        grid=(b // nb, s // tq),
        in_specs=[
            pl.BlockSpec((nb, tq, w), lambda bi, i: (bi, i, 0)),
            pl.BlockSpec(memory_space=pl.ANY),
            pl.BlockSpec(memory_space=pl.ANY),
            pl.BlockSpec(tri.shape, const),
            pl.BlockSpec(bias_a.shape, const),
            pl.BlockSpec(bias_bc.shape, const),
            pl.BlockSpec((1, w), const),
            pl.BlockSpec(gmat.shape, const),
        ],
        out_specs=pl.BlockSpec((nb, tq, w), lambda bi, i: (bi, i, 0)),
        out_shape=jax.ShapeDtypeStruct((b, s, w), BF16),
        scratch_shapes=[pltpu.VMEM((chains, rows, gw), BF16),
                        pltpu.VMEM((chains, rows, 2 * HEAD_DIM), F32),
                        pltpu.VMEM((chains, rows, 1), F32),
                        pltpu.VMEM((nb, s, w), BF16),
                        pltpu.VMEM((nb, s, w), BF16),
                        pltpu.SemaphoreType.DMA((2,))],
        compiler_params=pltpu.CompilerParams(
            dimension_semantics=("arbitrary", "arbitrary"),
            vmem_limit_bytes=ATTN_VMEM_LIMIT_BYTES),
        name="sb_attention",
    )(q, k, v, tri, bias_a, bias_bc, g_attn_out, gmat)


def _post_kernel(x_hbm, conv_hbm, attn_hbm, p_hbm, wo_hbm, gmlp_ref, wup_hbm, wdn_hbm, gple_ref,
                 wg_hbm, wp_hbm, gfin_ref, o_hbm, x_buf, conv_buf, attn_buf, p_buf, o_buf,
                 wo_ref, wup_ref, wdn_ref, wg_ref, wp_ref, in_sem, out_sem, w_sem, *,
                 apply_final):
    tm = x_buf.shape[1]
    n_tiles = x_hbm.shape[0] // tm
    w_conv = conv_buf.shape[2]
    dot = functools.partial(jnp.dot, preferred_element_type=F32)

    def tile_rows(i):
        return pl.ds(pl.multiple_of(i * tm, tm), tm)

    def in_copies(i, slot):
        pairs = ((x_hbm, x_buf), (conv_hbm, conv_buf), (attn_hbm, attn_buf), (p_hbm, p_buf))
        return [pltpu.make_async_copy(src.at[tile_rows(i), :], dst.at[slot], in_sem.at[j, slot])
                for j, (src, dst) in enumerate(pairs)]

    def out_copy(i, slot):
        return pltpu.make_async_copy(o_buf.at[slot], o_hbm.at[tile_rows(i), :], out_sem.at[slot])

    weights = ((wo_hbm, wo_ref), (wup_hbm, wup_ref), (wdn_hbm, wdn_ref), (wp_hbm, wp_ref),
               (wg_hbm, wg_ref))
    w_copies = {id(dst): pltpu.make_async_copy(src, dst, w_sem.at[j])
                for j, (src, dst) in enumerate(weights)}

    def compute(slot, arrive):
        subs = [slice(r0, r0 + POST_SUB_ROWS) for r0 in range(0, tm, POST_SUB_ROWS)]
        arrive(wo_ref)
        h = [x_buf[slot, rs, :] + dot(conv_buf[slot, rs, :], wo_ref[0:w_conv, :]) for rs in subs]
        h = [hs + dot(attn_buf[slot, rs, :], wo_ref[w_conv:, :]) for hs, rs in zip(h, subs)]
        m = [(hs * _rms_scale(hs) * gmlp_ref[...]).astype(BF16) for hs in h]
        arrive(wup_ref)
        r = [jnp.square(jnp.maximum(dot(ms, wup_ref[...]), 0.0)).astype(BF16) for ms in m]
        arrive(wdn_ref)
        h = [hs + dot(rr, wdn_ref[...]) for hs, rr in zip(h, r)]
        n = [(hs * _rms_scale(hs) * gple_ref[...]).astype(BF16) for hs in h]
        arrive(wp_ref)
        pp = [dot(p_buf[slot, rs, :].astype(BF16), wp_ref[...]) for rs in subs]
        arrive(wg_ref)
        gate = [jax.nn.sigmoid(dot(ns, wg_ref[...])) for ns in n]
        h = [hs + gs * ps for hs, gs, ps in zip(h, gate, pp)]
        for hs, rs in zip(h, subs):
            if apply_final:
                hs = hs * _rms_scale(hs) * gfin_ref[...]
            o_buf[slot, rs, :] = hs

    for copy in in_copies(0, 0):
        copy.start()
    for j, copy in enumerate(w_copies.values()):
        copy.start(priority=j % 2)
    for copy in in_copies(1, 1):
        copy.start()
    for copy in in_copies(0, 0):
        copy.wait()
    compute(0, lambda ref: w_copies[id(ref)].wait())
    out_copy(0, 0).start()

    def step(i, carry):
        slot = i % 2

        @pl.when(i + 1 < n_tiles)
        def _():
            for copy in in_copies(i + 1, 1 - slot):
                copy.start()

        for copy in in_copies(i, slot):
            copy.wait()

        @pl.when(i >= 2)
        def _():
            out_copy(i - 2, slot).wait()

        compute(slot, lambda ref: None)
        out_copy(i, slot).start()
        return carry

    lax.fori_loop(1, n_tiles, step, 0)
    out_copy(n_tiles - 2, n_tiles % 2).wait()
    out_copy(n_tiles - 1, (n_tiles - 1) % 2).wait()


def _post(x2, conv2, attn2, p2, w_out, g_mlp, w_up, w_down, g_ple, w_gate, w_proj, g_final,
          apply_final):
    t, d = x2.shape
    tm = TM_POST
    assert t // tm >= 2
    hbm = pl.BlockSpec(memory_space=pl.ANY)
    vmem = pl.BlockSpec(memory_space=pltpu.VMEM)
    tiles = lambda a, dtype: pltpu.VMEM((2, tm, a.shape[1]), dtype)
    return pl.pallas_call(
        functools.partial(_post_kernel, apply_final=apply_final),
        in_specs=[hbm, hbm, hbm, hbm, hbm, vmem, hbm, hbm, vmem, hbm, hbm, vmem],
        out_specs=hbm,
        out_shape=jax.ShapeDtypeStruct((t, d), F32),
        scratch_shapes=[tiles(x2, F32), tiles(conv2, BF16), tiles(attn2, BF16), tiles(p2, F32),
                        tiles(x2, F32)]
        + [pltpu.VMEM(w.shape, BF16) for w in (w_out, w_up, w_down, w_gate, w_proj)]
        + [pltpu.SemaphoreType.DMA((4, 2)),
           pltpu.SemaphoreType.DMA((2,)),
           pltpu.SemaphoreType.DMA((5,))],
        compiler_params=pltpu.CompilerParams(vmem_limit_bytes=ATTN_VMEM_LIMIT_BYTES),
        name="post",
    )(x2, conv2, attn2, p2, w_out, g_mlp, w_up, w_down, g_ple, w_gate, w_proj, g_final)


def _group_matrix():
    g = np.arange(MXU_TILE) // HEAD_DIM
    return jnp.asarray(np.where(g[:, None] == g[None, :], 1.0 / HEAD_DIM, 0.0), BF16)


def kernel(x, p, g_mix, w_in, conv_w, g_conv_out, g_attn_out, w_out, g_mlp, w_up, w_down,
           g_ple, w_ple_gate, w_ple_proj, g_final):
    b, s, d = x.shape
    depth = p.shape[0]
    w_conv = conv_w.shape[-1]
    w_attn = g_attn_out.shape[-1]
    assert s % TM_IN == 0 and s % TQ == 0 and TQ == TK and (b * s) % TM_POST == 0
    assert w_attn % (GROUP_HEADS * HEAD_DIM) == 0 and w_conv % MXU_TILE == 0
    assert w_attn % MXU_TILE == 0 and MXU_TILE % HEAD_DIM == 0
    assert b % ATTN_BATCHES == 0 and s % (ATTN_TILES * TQ) == 0

    idx = np.arange(TK)
    tri = jnp.asarray(idx[:, None] > idx[None, :], BF16)
    causal = np.where(idx[None, :] < idx[:, None], 0.0, MASKED_SCORE).astype(np.float32)
    per_head = lambda m: np.tile(m, (GROUP_HEADS, 1))
    bias_a = jnp.asarray(per_head(causal[:TQ // 2, :TK // 2]))
    bias_bc = jnp.asarray(np.concatenate([per_head(causal[TQ // 2:3 * TQ // 4]),
                                          per_head(causal[3 * TQ // 4:])], axis=0))
    gmat = _group_matrix()
    row = lambda g: g.reshape(1, -1)

    h = x
    for i in range(depth):
        (conv, q, k, v), (wo, wu, wd, wg, wp) = _mixer_in(
            h, row(g_mix[i]), w_in[i], row(conv_w[i]), row(g_conv_out[i]), gmat, w_conv, w_attn,
            [w_out[i], w_up[i], w_down[i], w_ple_gate[i], w_ple_proj[i]])
        attn = _attention(q, k, v, tri, bias_a, bias_bc, row(g_attn_out[i]), gmat)
        h = _post(h.reshape(b * s, d), conv.reshape(b * s, w_conv), attn.reshape(b * s, w_attn),
                  p[i].reshape(b * s, -1), wo, row(g_mlp[i]), wu, wd, row(g_ple[i]), wg, wp,
                  row(g_final), apply_final=(i == depth - 1)).reshape(b, s, d)
    return h
```
